```python
import jax, jax.numpy as jnp
from jax import lax
import numpy as np

D_MODEL = 2048
BATCH = 1
SEQ = 16384
DEPTH = 1

ATT_HEAD_DIM = 128
ATT_HEADS = D_MODEL // ATT_HEAD_DIM
ATT_KV_HEADS = ATT_HEADS // 4
ATT_GROUP = ATT_HEADS // ATT_KV_HEADS
WINDOW = 128
BLOCK = 128
ROPE_THETA = 10000.0

RWKV_HEAD_DIM = 64
RWKV_HEADS = D_MODEL // RWKV_HEAD_DIM
RWKV_DIM = RWKV_HEADS * RWKV_HEAD_DIM
DECAY_LORA = max(32, int(round(1.8 * D_MODEL ** 0.5 / 32)) * 32)
ICLR_LORA = max(32, int(round(1.8 * D_MODEL ** 0.5 / 32)) * 32)
GATE_LORA = max(32, int(round(0.6 * D_MODEL ** 0.8 / 32)) * 32)

D_FF = 4 * D_MODEL
NORM_EPS = 1e-6
GN_EPS = 64e-5
MASK_VALUE = -1e30

Q_COLS = ATT_HEADS * ATT_HEAD_DIM
KV_COLS = ATT_KV_HEADS * ATT_HEAD_DIM
SHIFT_SIZES = [RWKV_DIM, RWKV_DIM, RWKV_DIM, DECAY_LORA, DECAY_LORA, ICLR_LORA, ICLR_LORA, GATE_LORA]
SHIFT_COLS = sum(SHIFT_SIZES)
IN_SIZES = [Q_COLS, KV_COLS, KV_COLS, SHIFT_COLS, D_MODEL, D_MODEL]
IN_COLS = sum(IN_SIZES)

kernel_name = 'hybrid_swa_rwkv7_sandwich_block'


def _split(t, sizes):
    idx = [int(i) for i in np.cumsum(sizes)[:-1]]
    return jnp.split(t, idx, axis=-1)


def rms_norm(x, gain):
    x32 = x.astype(jnp.float32)
    y = x32 * lax.rsqrt(jnp.mean(x32 * x32, axis=-1, keepdims=True) + NORM_EPS)
    return (y * gain.astype(jnp.float32)).astype(x.dtype)


def rope_tables(T):
    pos = jnp.arange(T, dtype=jnp.float32)
    inv_freq = ROPE_THETA ** (-jnp.arange(0, ATT_HEAD_DIM, 2, dtype=jnp.float32) / ATT_HEAD_DIM)
    ang = pos[:, None] * inv_freq[None, :]
    return jnp.cos(ang), jnp.sin(ang)


def apply_rope(x, cos, sin):
    x1, x2 = jnp.split(x.astype(jnp.float32), 2, axis=-1)
    return jnp.concatenate([x1 * cos - x2 * sin, x2 * cos + x1 * sin], axis=-1).astype(x.dtype)


def banded_gqa_attention(q, k, v, sink):
    B, T = q.shape[0], q.shape[1]
    nb = T // BLOCK
    qb = q.reshape(B, nb, BLOCK, ATT_KV_HEADS, ATT_GROUP, ATT_HEAD_DIM)

    def windows(t):
        tp = jnp.pad(t, ((0, 0), (BLOCK, BLOCK), (0, 0), (0, 0)))
        tp = tp.reshape(B, nb + 2, BLOCK, ATT_KV_HEADS, ATT_HEAD_DIM)
        return jnp.concatenate([tp[:, :-2], tp[:, 1:-1], tp[:, 2:]], axis=2)

    kw, vw = windows(k), windows(v)
    scale = ATT_HEAD_DIM ** -0.5
    s = jnp.einsum('bnqkgd,bnskd->bnkgqs', qb, kw, preferred_element_type=jnp.float32) * scale
    qi = jnp.arange(BLOCK)[:, None]
    sj = jnp.arange(3 * BLOCK)[None, :]
    in_window = jnp.abs(sj - BLOCK - qi) <= WINDOW
    kpos = jnp.arange(nb)[:, None] * BLOCK - BLOCK + jnp.arange(3 * BLOCK)[None, :]
    in_range = (kpos >= 0) & (kpos < T)
    valid = in_window[None, :, :] & in_range[:, None, :]
    s = jnp.where(valid[None, :, None, None], s, MASK_VALUE)
    sink_col = jnp.broadcast_to(
        sink.astype(jnp.float32).reshape(ATT_KV_HEADS, ATT_GROUP)[None, None, :, :, None, None],
        s.shape[:-1] + (1,))
    p = jax.nn.softmax(jnp.concatenate([s, sink_col], axis=-1), axis=-1)[..., :-1]
    o = jnp.einsum('bnkgqs,bnskd->bnqkgd', p.astype(vw.dtype), vw, preferred_element_type=jnp.float32)
    return o.reshape(B, T, ATT_HEADS * ATT_HEAD_DIM)


def centred_token_shift(c, mu):
    prev = jnp.pad(c[:, :-1], ((0, 0), (1, 0), (0, 0)))
    nxt = jnp.pad(c[:, 1:], ((0, 0), (0, 1), (0, 0)))
    return c + mu * (0.5 * (prev + nxt) - c)


def wkv7_scan(r, decay, k, v, a_vec, b_vec, reverse):
    B = r.shape[0]
    xs = tuple(jnp.moveaxis(t, 1, 0) for t in (r, decay, k, v, a_vec, b_vec))
    S0 = jnp.zeros((B, RWKV_HEADS, RWKV_HEAD_DIM, RWKV_HEAD_DIM), jnp.float32)

    def step(S, inp):
        r_t, w_t, k_t, v_t, a_t, b_t = inp
        sa = jnp.einsum('bhij,bhj->bhi', S, a_t)
        S = S * w_t[:, :, None, :] + sa[..., None] * b_t[:, :, None, :] + v_t[..., None] * k_t[:, :, None, :]
        y = jnp.einsum('bhij,bhj->bhi', S, r_t)
        return S, y

    _, ys = lax.scan(step, S0, xs, reverse=reverse)
    return jnp.moveaxis(ys, 0, 1)


def rwkv7_direction(r, k, v, kk, wd, ad, w0, w_up, a0, a_up, k_a, reverse):
    B, T = r.shape[0], r.shape[1]
    heads = lambda t: t.reshape(B, T, RWKV_HEADS, RWKV_HEAD_DIM)
    w_log = -jax.nn.softplus(-(w0.astype(jnp.float32) + jnp.tanh(wd) @ w_up.astype(jnp.float32))) - 0.5
    decay = jnp.exp(-jnp.exp(w_log))
    a = jax.nn.sigmoid(a0.astype(jnp.float32) + ad @ a_up.astype(jnp.float32))
    k_mod = k * (1.0 + (a - 1.0) * k_a.astype(jnp.float32))
    y = wkv7_scan(heads(r), heads(decay), heads(k_mod), heads(v), -kk, kk * heads(a), reverse)
    return y, k_mod


def rwkv7_bidirectional(cols, mu, w0_fwd, w_up_fwd, w0_bwd, w_up_bwd, a0_fwd, a_up_fwd,
                        a0_bwd, a_up_bwd, g_up, k_k, k_a, r_k, ln_x_gain, ln_x_bias):
    c = centred_token_shift(cols.astype(jnp.float32), mu.astype(jnp.float32))
    r, k, v, wd_f, wd_b, ad_f, ad_b, gd = _split(c, SHIFT_SIZES)
    B, T = r.shape[0], r.shape[1]
    heads = lambda t: t.reshape(B, T, RWKV_HEADS, RWKV_HEAD_DIM)
    kk = heads(k * k_k.astype(jnp.float32))
    kk = kk / jnp.maximum(jnp.sqrt(jnp.sum(kk * kk, axis=-1, keepdims=True)), 1e-12)
    y_f, k_f = rwkv7_direction(r, k, v, kk, wd_f, ad_f, w0_fwd, w_up_fwd, a0_fwd, a_up_fwd, k_a, False)
    y_b, k_b = rwkv7_direction(r, k, v, kk, wd_b, ad_b, w0_bwd, w_up_bwd, a0_bwd, a_up_bwd, k_a, True)
    y = y_f + y_b
    mean = jnp.mean(y, axis=-1, keepdims=True)
    var = jnp.mean(jnp.square(y - mean), axis=-1, keepdims=True)
    y = ((y - mean) * lax.rsqrt(var + GN_EPS)).reshape(B, T, RWKV_DIM)
    y = y * ln_x_gain.astype(jnp.float32) + ln_x_bias.astype(jnp.float32)
    k_bonus = heads(0.5 * (k_f + k_b))
    r_k_h = r_k.astype(jnp.float32).reshape(RWKV_HEADS, RWKV_HEAD_DIM)
    bonus = jnp.sum(heads(r) * k_bonus * r_k_h, axis=-1, keepdims=True) * heads(v)
    g = jax.nn.sigmoid(gd) @ g_up.astype(jnp.float32)
    return (y + bonus.reshape(B, T, RWKV_DIM)) * g


def setup_inputs(seed: int = 0) -> dict:
    key = jax.random.key(seed)
    ks = jax.random.split(key, 26)
    f32 = jnp.float32
    nrm = lambda k, shape, scale: scale * jax.random.normal(k, shape, f32)
    gain = lambda k: 1.0 + 0.05 * jax.random.normal(k, (DEPTH, D_MODEL), f32)
    return {
        'x': jax.random.normal(ks[0], (BATCH, SEQ, D_MODEL), f32),
        'norm_pre_mix': gain(ks[1]),
        'w_in': nrm(ks[2], (DEPTH, D_MODEL, IN_COLS), D_MODEL ** -0.5),
        'mu_shift': jax.random.uniform(ks[3], (DEPTH, SHIFT_COLS), f32),
        'attn_sink': nrm(ks[4], (DEPTH, ATT_HEADS), 0.5),
        'w0_fwd': jax.random.uniform(ks[5], (DEPTH, RWKV_DIM), f32, -4.0, 0.5),
        'w_up_fwd': nrm(ks[6], (DEPTH, DECAY_LORA, RWKV_DIM), 0.1),
        'w0_bwd': jax.random.uniform(ks[7], (DEPTH, RWKV_DIM), f32, -4.0, 0.5),
        'w_up_bwd': nrm(ks[8], (DEPTH, DECAY_LORA, RWKV_DIM), 0.1),
        'a0_fwd': nrm(ks[9], (DEPTH, RWKV_DIM), 0.1),
        'a_up_fwd': nrm(ks[10], (DEPTH, ICLR_LORA, RWKV_DIM), 0.1),
        'a0_bwd': nrm(ks[11], (DEPTH, RWKV_DIM), 0.1),
        'a_up_bwd': nrm(ks[12], (DEPTH, ICLR_LORA, RWKV_DIM), 0.1),
        'g_up': nrm(ks[13], (DEPTH, GATE_LORA, RWKV_DIM), GATE_LORA ** -0.5),
        'k_k': 0.85 + nrm(ks[14], (DEPTH, RWKV_DIM), 0.05),
        'k_a': 1.0 + nrm(ks[15], (DEPTH, RWKV_DIM), 0.05),
        'r_k': nrm(ks[16], (DEPTH, RWKV_DIM), 0.1),
        'ln_x_gain': gain(ks[17]),
        'ln_x_bias': nrm(ks[18], (DEPTH, RWKV_DIM), 0.01),
        'w_out': nrm(ks[19], (DEPTH, D_MODEL, D_MODEL), D_MODEL ** -0.5),
        'norm_post_mix': gain(ks[20]),
        'norm_pre_ffn': gain(ks[21]),
        'w_ffn_up': nrm(ks[22], (DEPTH, D_MODEL, D_FF), D_MODEL ** -0.5),
        'w_ffn_down': nrm(ks[23], (DEPTH, D_FF, D_MODEL), D_FF ** -0.5),
        'norm_post_ffn': gain(ks[24]),
    }


def reference(x, norm_pre_mix, w_in, mu_shift, attn_sink, w0_fwd, w_up_fwd, w0_bwd, w_up_bwd,
              a0_fwd, a_up_fwd, a0_bwd, a_up_bwd, g_up, k_k, k_a, r_k, ln_x_gain, ln_x_bias,
              w_out, norm_post_mix, norm_pre_ffn, w_ffn_up, w_ffn_down, norm_post_ffn):
    B, T, _ = x.shape
    cos, sin = rope_tables(T)
    h = x
    for l in range(DEPTH):
        xn = rms_norm(h, norm_pre_mix[l])
        cols = xn @ w_in[l]
        q, k, v, rw_cols, gate_attn, gate_rwkv = _split(cols, IN_SIZES)
        q = apply_rope(q.reshape(B, T, ATT_KV_HEADS, ATT_GROUP, ATT_HEAD_DIM),
                       cos[None, :, None, None, :], sin[None, :, None, None, :])
        k = apply_rope(k.reshape(B, T, ATT_KV_HEADS, ATT_HEAD_DIM),
                       cos[None, :, None, :], sin[None, :, None, :])
        v = v.reshape(B, T, ATT_KV_HEADS, ATT_HEAD_DIM)
        o_attn = banded_gqa_attention(q, k, v, attn_sink[l])
        o_rwkv = rwkv7_bidirectional(rw_cols, mu_shift[l], w0_fwd[l], w_up_fwd[l], w0_bwd[l],
                                     w_up_bwd[l], a0_fwd[l], a_up_fwd[l], a0_bwd[l], a_up_bwd[l],
                                     g_up[l], k_k[l], k_a[l], r_k[l], ln_x_gain[l], ln_x_bias[l])
        merged = (jax.nn.sigmoid(gate_attn.astype(jnp.float32)) * o_attn
                  + jax.nn.sigmoid(gate_rwkv.astype(jnp.float32)) * o_rwkv)
        mix = merged.astype(x.dtype) @ w_out[l]
        h = h + rms_norm(mix, norm_post_mix[l])
        hn = rms_norm(h, norm_pre_ffn[l])
        ff = jnp.square(jax.nn.relu(hn @ w_ffn_up[l])) @ w_ffn_down[l]
        h = h + rms_norm(ff, norm_post_ffn[l])
    return h
```

```python
import functools

import jax
import jax.numpy as jnp
import numpy as np
from jax import lax
from jax.experimental import pallas as pl
from jax.experimental.pallas import tpu as pltpu

F32 = jnp.float32
BF16 = jnp.bfloat16
LANES = 128
SUBLANES = 8

D_MODEL = 2048
ATT_HEAD_DIM = 128
ATT_HEADS = 16
ATT_KV_HEADS = 4
ATT_GROUP = 4
WINDOW = 128
BLOCK = 128
ROPE_THETA = 10000.0
RWKV_HEAD_DIM = 64
RWKV_DIM = 2048
N_PAIRS = RWKV_DIM // LANES
DECAY_LORA = 96
ICLR_LORA = 96
GATE_LORA = 256
LORA_PAD = 128
LORA_COLS = 1024
D_FF = 4 * D_MODEL
NORM_EPS = 1e-6
GN_EPS = 64e-5
MASK_VALUE = -1e30
KV_COLS = ATT_KV_HEADS * ATT_HEAD_DIM

CB_Q, CB_GA, CB_GR, CB_R, CB_K, CB_V, CB_AK, CB_AV, CB_LORA = 0, 16, 32, 48, 64, 80, 96, 100, 104
IN_COLS_PAD = 112 * LANES

CHUNK = 128
VMEM_LIMIT = 56 * 1024 * 1024

NN = (((1,), (0,)), ((), ()))
NT = (((1,), (1,)), ((), ()))
TN = (((0,), (0,)), ((), ()))


def _mm(a, b, dims=NN):
    return lax.dot_general(a.astype(BF16), b.astype(BF16), dims, preferred_element_type=F32)


def _mm_f32(a, b, dims=NN):
    return lax.dot_general(a, b, dims, precision=lax.Precision.HIGHEST, preferred_element_type=F32)


def _cparams(sem):
    return pltpu.CompilerParams(dimension_semantics=sem, vmem_limit_bytes=VMEM_LIMIT)


def _inproj_kernel(x_ref, g_ref, w_ref, o_ref, xn_ref):
    @pl.when(pl.program_id(1) == 0)
    def _():
        x = x_ref[...]
        ms = jnp.mean(x * x, axis=-1, keepdims=True)
        xn_ref[...] = (x * lax.rsqrt(ms + NORM_EPS) * g_ref[...]).astype(BF16)

    o_ref[...] = jnp.dot(xn_ref[...], w_ref[...], preferred_element_type=F32)


def _inproj(x, gain, w, tm=512, tn=2048):
    T = x.shape[0]
    n = w.shape[1]
    return pl.pallas_call(
        _inproj_kernel,
        grid=(T // tm, n // tn),
        in_specs=[
            pl.BlockSpec((tm, D_MODEL), lambda i, j: (i, 0)),
            pl.BlockSpec((1, D_MODEL), lambda i, j: (0, 0)),
            pl.BlockSpec((D_MODEL, tn), lambda i, j: (0, j)),
        ],
        out_specs=pl.BlockSpec((tm, tn), lambda i, j: (i, j)),
        out_shape=jax.ShapeDtypeStruct((T, n), F32),
        scratch_shapes=[pltpu.VMEM((tm, D_MODEL), BF16)],
        compiler_params=_cparams(("parallel", "arbitrary")),
        name="inproj",
    )(x, gain, w)


def _rope(x, c, s):
    return x * c + pltpu.roll(x, ATT_HEAD_DIM // 2, 1) * s


def _attn_kernel(sink_ref, q_ref, kp_ref, kc_ref, kn_ref, vp_ref, vc_ref, vn_ref, gate_ref,
                 cc_ref, sc_ref, cp_ref, sp_ref, cn_ref, sn_ref, o_ref, *, seq):
    i = pl.program_id(0)
    cc, sc = cc_ref[...], sc_ref[...]
    cp, sp = cp_ref[...], sp_ref[...]
    cn, sn = cn_ref[...], sn_ref[...]
    qi = lax.broadcasted_iota(jnp.int32, (BLOCK, 3 * BLOCK), 0)
    sj = lax.broadcasted_iota(jnp.int32, (BLOCK, 3 * BLOCK), 1)
    kpos = (i - 1) * BLOCK + sj
    valid = (jnp.abs(sj - BLOCK - qi) <= WINDOW) & (kpos >= 0) & (kpos < seq)
    valid4 = jnp.concatenate([valid] * ATT_GROUP, axis=0)
    rowg = lax.broadcasted_iota(jnp.int32, (ATT_GROUP * BLOCK, 1), 0) // BLOCK
    scale = ATT_HEAD_DIM ** -0.5
    for g in range(ATT_KV_HEADS):
        ks = slice(g * ATT_HEAD_DIM, (g + 1) * ATT_HEAD_DIM)
        kw = jnp.concatenate([_rope(kp_ref[:, ks], cp, sp), _rope(kc_ref[:, ks], cc, sc),
                              _rope(kn_ref[:, ks], cn, sn)], axis=0)
        vw = jnp.concatenate([vp_ref[:, ks], vc_ref[:, ks], vn_ref[:, ks]], axis=0)
        heads = [g * ATT_GROUP + hh for hh in range(ATT_GROUP)]
        q4 = jnp.concatenate(
            [_rope(q_ref[:, h * ATT_HEAD_DIM:(h + 1) * ATT_HEAD_DIM], cc, sc) for h in heads], axis=0) * scale
        s = _mm(q4, kw, NT)
        s = jnp.where(valid4, s, MASK_VALUE)
        sink = jnp.zeros((ATT_GROUP * BLOCK, 1), F32)
        for hh, h in enumerate(heads):
            sink = jnp.where(rowg == hh, sink_ref[h], sink)
        m = jnp.maximum(jnp.max(s, axis=-1, keepdims=True), sink)
        p = jnp.exp(s - m)
        den = jnp.sum(p, axis=-1, keepdims=True) + jnp.exp(sink - m)
        o = _mm(p, vw) / den
        for hh, h in enumerate(heads):
            cs = slice(h * ATT_HEAD_DIM, (h + 1) * ATT_HEAD_DIM)
            o_ref[:, cs] = o[hh * BLOCK:(hh + 1) * BLOCK] * jax.nn.sigmoid(gate_ref[:, cs])


def _attention(cols, sink, cos2, sin2):
    T = cols.shape[0]
    nb = T // BLOCK
    prev = lambda i: (jnp.maximum(i - 1, 0), 0)
    cur = lambda i: (i, 0)
    nxt = lambda i: (jnp.minimum(i + 1, nb - 1), 0)
    kcb, vcb = CB_AK * LANES // KV_COLS, CB_AV * LANES // KV_COLS
    col = lambda f, cb: (lambda i: (f(i)[0], cb))
    tab = pl.BlockSpec
    return pl.pallas_call(
        functools.partial(_attn_kernel, seq=T),
        grid=(nb,),
        in_specs=[
            pl.BlockSpec(memory_space=pltpu.SMEM),
            pl.BlockSpec((BLOCK, D_MODEL), col(cur, CB_Q // 16)),
            pl.BlockSpec((BLOCK, KV_COLS), col(prev, kcb)),
            pl.BlockSpec((BLOCK, KV_COLS), col(cur, kcb)),
            pl.BlockSpec((BLOCK, KV_COLS), col(nxt, kcb)),
            pl.BlockSpec((BLOCK, KV_COLS), col(prev, vcb)),
            pl.BlockSpec((BLOCK, KV_COLS), col(cur, vcb)),
            pl.BlockSpec((BLOCK, KV_COLS), col(nxt, vcb)),
            pl.BlockSpec((BLOCK, D_MODEL), col(cur, CB_GA // 16)),
            tab((BLOCK, LANES), cur), tab((BLOCK, LANES), cur),
            tab((BLOCK, LANES), prev), tab((BLOCK, LANES), prev),
            tab((BLOCK, LANES), nxt), tab((BLOCK, LANES), nxt),
        ],
        out_specs=pl.BlockSpec((BLOCK, D_MODEL), cur),
        out_shape=jax.ShapeDtypeStruct((T, D_MODEL), F32),
        compiler_params=_cparams(("parallel",)),
        name="attention",
    )(sink, cols, cols, cols, cols, cols, cols, cols, cols, cos2, sin2, cos2, sin2, cos2, sin2)


def _head_sum(x, ones_bd):
    return _mm_f32(x, ones_bd)


def _prep_kernel(r_ref, rp_ref, rn_ref, k_ref, kp_ref, kn_ref, v_ref, vp_ref, vn_ref,
                 lo_ref, lop_ref, lon_ref, mur_ref, muk_ref, muv_ref, mul_ref,
                 w0f_ref, w0b_ref, a0f_ref, a0b_ref, kk_ref, ka_ref, rk_ref,
                 wuf_ref, wub_ref, auf_ref, aub_ref, gup_ref, bd_ref,
                 r_o, v_o, kk_o, lwf_o, bf_o, kf_o, lwb_o, bb_o, kb_o, g_o, bonus_o):
    i = pl.program_id(0)
    first = i == 0
    last = i == pl.num_programs(0) - 1
    tm = r_ref.shape[0]

    def shift(c_ref, p_ref, n_ref, mu):
        c = c_ref[...]
        row = lax.broadcasted_iota(jnp.int32, c.shape, 0)
        prow = jnp.where(first, 0.0, p_ref[SUBLANES - 1:SUBLANES, :])
        nrow = jnp.where(last, 0.0, n_ref[0:1, :])
        prev = jnp.where(row == 0, prow, pltpu.roll(c, 1, 0))
        nxt = jnp.where(row == tm - 1, nrow, pltpu.roll(c, tm - 1, 0))
        return c + mu * (0.5 * (prev + nxt) - c)

    r = shift(r_ref, rp_ref, rn_ref, mur_ref[...])
    k = shift(k_ref, kp_ref, kn_ref, muk_ref[...])
    v = shift(v_ref, vp_ref, vn_ref, muv_ref[...])
    lo = shift(lo_ref, lop_ref, lon_ref, mul_ref[...])
    ones_bd = bd_ref[...]

    kk = k * kk_ref[...]
    kk = kk / jnp.maximum(jnp.sqrt(_head_sum(kk * kk, ones_bd)), 1e-12)
    k_a = ka_ref[...]

    def direction(wd, ad, w0, wu, a0, au):
        w_log = -jax.nn.softplus(-(w0 + _mm_f32(jnp.tanh(wd), wu))) - 0.5
        lw = -jnp.exp(w_log)
        a = jax.nn.sigmoid(a0 + _mm_f32(ad, au))
        k_mod = k * (1.0 + (a - 1.0) * k_a)
        return lw, kk * a, k_mod

    P = LORA_PAD
    lwf, bf, kf = direction(lo[:, 0:P], lo[:, 2 * P:3 * P], w0f_ref[...], wuf_ref[...], a0f_ref[...], auf_ref[...])
    lwb, bb, kb = direction(lo[:, P:2 * P], lo[:, 3 * P:4 * P], w0b_ref[...], wub_ref[...], a0b_ref[...], aub_ref[...])
    g = _mm_f32(jax.nn.sigmoid(lo[:, 4 * P:4 * P + GATE_LORA]), gup_ref[...])
    bonus = _head_sum(r * (0.5 * (kf + kb)) * rk_ref[...], ones_bd) * v

    r_o[...] = r
    v_o[...] = v
    kk_o[...] = kk
    lwf_o[...] = lwf
    bf_o[...] = bf
    kf_o[...] = kf
    lwb_o[...] = lwb
    bb_o[...] = bb
    kb_o[...] = kb
    g_o[...] = g
    bonus_o[...] = bonus


def _rwkv_prep(cols, mu_rkv, mu_lora, vecs, w_up_f, w_up_b, a_up_f, a_up_b, g_up, ones_bd, tm=512):
    T = cols.shape[0]
    nt = T // tm
    hb = tm // SUBLANES
    nhb = T // SUBLANES
    main = lambda cb: pl.BlockSpec((tm, LANES), lambda i, j: (i, cb + j))
    hprev = lambda cb: pl.BlockSpec((SUBLANES, LANES), lambda i, j: (jnp.maximum(i * hb - 1, 0), cb + j))
    hnext = lambda cb: pl.BlockSpec((SUBLANES, LANES), lambda i, j: (jnp.minimum((i + 1) * hb, nhb - 1), cb + j))
    lcb = CB_LORA * LANES // LORA_COLS
    vec = lambda off: pl.BlockSpec((1, LANES), lambda i, j: (0, off + j))
    up = lambda rows: pl.BlockSpec((rows, LANES), lambda i, j: (0, j))
    in_specs = []
    for cb in (CB_R, CB_K, CB_V):
        in_specs += [main(cb), hprev(cb), hnext(cb)]
    in_specs += [
        pl.BlockSpec((tm, LORA_COLS), lambda i, j: (i, lcb)),
        pl.BlockSpec((SUBLANES, LORA_COLS), lambda i, j: (jnp.maximum(i * hb - 1, 0), lcb)),
        pl.BlockSpec((SUBLANES, LORA_COLS), lambda i, j: (jnp.minimum((i + 1) * hb, nhb - 1), lcb)),
        vec(0), vec(N_PAIRS), vec(2 * N_PAIRS),
        pl.BlockSpec((1, LORA_COLS), lambda i, j: (0, 0)),
    ]
    in_specs += [vec(0)] * 7
    in_specs += [up(LORA_PAD)] * 4 + [up(GATE_LORA)]
    in_specs += [pl.BlockSpec((LANES, LANES), lambda i, j: (0, 0))]
    out = jax.ShapeDtypeStruct((T, RWKV_DIM), F32)
    return pl.pallas_call(
        _prep_kernel,
        grid=(nt, N_PAIRS),
        in_specs=in_specs,
        out_specs=[pl.BlockSpec((tm, LANES), lambda i, j: (i, j))] * 11,
        out_shape=[out] * 11,
        compiler_params=_cparams(("parallel", "arbitrary")),
        name="rwkv_prep",
    )(cols, cols, cols, cols, cols, cols, cols, cols, cols, cols, cols, cols,
      mu_rkv, mu_rkv, mu_rkv, mu_lora, *vecs, w_up_f, w_up_b, a_up_f, a_up_b, g_up, ones_bd)


def _tri_inverse(a):
    L = a.shape[0]
    row = lax.broadcasted_iota(jnp.int32, (L, L), 0)
    col = lax.broadcasted_iota(jnp.int32, (L, L), 1)
    same = lambda sh: (row >> sh) == (col >> sh)
    d = jnp.where(row == col, 1.0, 0.0) + jnp.where(same(1), a, 0.0)
    sh = 1
    while (1 << sh) < L:
        off = jnp.where(same(sh + 1) & jnp.logical_not(same(sh)), a, 0.0)
        d = d + _mm(d, _mm(off, d))
        sh += 1
    return d


def _chunk_direction(r, v, kk, lw, b, k, s, reverse):
    L = r.shape[0]
    row = lax.broadcasted_iota(jnp.int32, (L, L), 0)
    col = lax.broadcasted_iota(jnp.int32, (L, L), 1)
    if reverse:
        incl, strict = col >= row, col > row
    else:
        incl, strict = col <= row, col < row
    cum = _mm_f32(jnp.where(incl, 1.0, 0.0), lw)
    tot = cum[0:1] if reverse else cum[L - 1:L]
    mid = L // 2 if reverse else L // 2 - 1
    rho = cum[mid:mid + 1]
    e1 = jnp.exp(cum - rho)
    e2 = jnp.exp(rho - cum)
    ew = jnp.exp(-lw)
    er = jnp.exp(rho)
    et = jnp.exp(tot - rho)
    a_t = -kk * e1 * ew
    r_t = r * e1
    b_t = b * e2
    k_t = k * e2
    lane = lax.broadcasted_iota(jnp.int32, (1, LANES), 1)
    h0 = lane < RWKV_HEAD_DIM
    zero = jnp.zeros_like(a_t)
    lhs = jnp.concatenate([jnp.where(h0, a_t, zero), jnp.where(h0, zero, a_t),
                           jnp.where(h0, r_t, zero), jnp.where(h0, zero, r_t)], axis=0)
    rhs = jnp.concatenate([b_t, k_t], axis=0)
    gram = _mm(lhs, rhs, NT)
    strict2 = jnp.concatenate([strict, strict], axis=1)
    incl2 = jnp.concatenate([incl, incl], axis=1)
    a_bk = [jnp.where(strict2, gram[hh * L:(hh + 1) * L], 0.0) for hh in range(2)]
    r_bk = [jnp.where(incl2, gram[(2 + hh) * L:(3 + hh) * L], 0.0) for hh in range(2)]
    t_inv = [_tri_inverse(a_bk[hh][:, :L]) for hh in range(2)]
    xs = _mm(jnp.concatenate([a_t * er, r_t * er], axis=0), s, NT)
    akv = [_mm(a_bk[hh][:, L:], v) for hh in range(2)]
    x = xs[:L] + jnp.where(h0, akv[0], akv[1])
    u = jnp.where(h0, _mm(t_inv[0], x), _mm(t_inv[1], x))
    z = jnp.concatenate([u, v], axis=0)
    y = xs[L:] + jnp.where(h0, _mm(r_bk[0], z), _mm(r_bk[1], z))
    w = jnp.concatenate([b_t * et, k_t * et], axis=0)
    srow = lax.broadcasted_iota(jnp.int32, (LANES, LANES), 0)
    scol = lax.broadcasted_iota(jnp.int32, (LANES, LANES), 1)
    same_head = (srow >= RWKV_HEAD_DIM) == (scol >= RWKV_HEAD_DIM)
    s_new = s * (er * et) + jnp.where(same_head, _mm(z, w, TN), 0.0)
    return y, s_new


def _scan_kernel(rf_ref, vf_ref, kkf_ref, lwf_ref, bf_ref, kf_ref,
                 rb_ref, vb_ref, kkb_ref, lwb_ref, bb_ref, kb_ref,
                 yf_ref, yb_ref, s_ref):
    @pl.when(pl.program_id(1) == 0)
    def _():
        s_ref[...] = jnp.zeros_like(s_ref)

    yf, sf = _chunk_direction(rf_ref[...], vf_ref[...], kkf_ref[...], lwf_ref[...], bf_ref[...], kf_ref[...],
                              s_ref[0], False)
    yb, sb = _chunk_direction(rb_ref[...], vb_ref[...], kkb_ref[...], lwb_ref[...], bb_ref[...], kb_ref[...],
                              s_ref[1], True)
    yf_ref[...] = yf
    yb_ref[...] = yb
    s_ref[0] = sf
    s_ref[1] = sb


def _rwkv_scan(r, v, kk, lwf, bf, kf, lwb, bb, kb):
    T = r.shape[0]
    nc = T // CHUNK
    fwd = pl.BlockSpec((CHUNK, LANES), lambda p, c: (c, p))
    bwd = pl.BlockSpec((CHUNK, LANES), lambda p, c: (nc - 1 - c, p))
    out = jax.ShapeDtypeStruct((T, RWKV_DIM), F32)
    return pl.pallas_call(
        _scan_kernel,
        grid=(N_PAIRS, nc),
        in_specs=[fwd] * 6 + [bwd] * 6,
        out_specs=[fwd, bwd],
        out_shape=[out, out],
        scratch_shapes=[pltpu.VMEM((2, LANES, LANES), F32)],
        compiler_params=_cparams(("parallel", "arbitrary")),
        name="rwkv_scan",
    )(r, v, kk, lwf, bf, kf, r, v, kk, lwb, bb, kb)


def _post_kernel(yf_ref, yb_ref, bonus_ref, g_ref, att_ref, gr_ref, gain_ref, bias_ref, bd_ref, o_ref):
    ones_bd = bd_ref[...]
    y = yf_ref[...] + yb_ref[...]
    inv_n = 1.0 / RWKV_HEAD_DIM
    mean = _head_sum(y, ones_bd) * inv_n
    d = y - mean
    var = _head_sum(d * d, ones_bd) * inv_n
    yn = d * lax.rsqrt(var + GN_EPS) * gain_ref[...] + bias_ref[...]
    o_rwkv = (yn + bonus_ref[...]) * g_ref[...]
    o_ref[...] = (att_ref[...] + jax.nn.sigmoid(gr_ref[...]) * o_rwkv).astype(o_ref.dtype)


def _rwkv_post(yf, yb, bonus, g, att, cols, gain, bias, ones_bd, tm=1024):
    T = yf.shape[0]
    blk = pl.BlockSpec((tm, LANES), lambda i, j: (i, j))
    vec = pl.BlockSpec((1, LANES), lambda i, j: (0, j))
    return pl.pallas_call(
        _post_kernel,
        grid=(T // tm, N_PAIRS),
        in_specs=[blk, blk, blk, blk, blk,
                  pl.BlockSpec((tm, LANES), lambda i, j: (i, CB_GR + j)),
                  vec, vec, pl.BlockSpec((LANES, LANES), lambda i, j: (0, 0))],
        out_specs=blk,
        out_shape=jax.ShapeDtypeStruct((T, D_MODEL), BF16),
        compiler_params=_cparams(("parallel", "parallel")),
        name="rwkv_post",
    )(yf, yb, bonus, g, att, cols, gain, bias, ones_bd)


def _rms(x, gain):
    return x * lax.rsqrt(jnp.mean(x * x, axis=-1, keepdims=True) + NORM_EPS) * gain


def _outproj_kernel(m_ref, w_ref, x_ref, gpost_ref, gpre_ref, h_ref, hn_ref):
    mix = jnp.dot(m_ref[...], w_ref[...], preferred_element_type=F32)
    h = x_ref[...] + _rms(mix, gpost_ref[...])
    h_ref[...] = h
    hn_ref[...] = _rms(h, gpre_ref[...]).astype(BF16)


def _outproj(merged, w_out, x, g_post, g_pre, tm=256):
    T = x.shape[0]
    row = pl.BlockSpec((tm, D_MODEL), lambda i: (i, 0))
    vec = pl.BlockSpec((1, D_MODEL), lambda i: (0, 0))
    return pl.pallas_call(
        _outproj_kernel,
        grid=(T // tm,),
        in_specs=[row, pl.BlockSpec((D_MODEL, D_MODEL), lambda i: (0, 0)), row, vec, vec],
        out_specs=[row, row],
        out_shape=[jax.ShapeDtypeStruct((T, D_MODEL), F32), jax.ShapeDtypeStruct((T, D_MODEL), BF16)],
        compiler_params=_cparams(("parallel",)),
        name="outproj",
    )(merged, w_out, x, g_post, g_pre)


def _ffn_kernel(hn_ref, wu_ref, wd_ref, h_ref, g_ref, o_ref, acc_ref):
    j = pl.program_id(1)

    @pl.when(j == 0)
    def _():
        acc_ref[...] = jnp.zeros_like(acc_ref)

    up = jnp.dot(hn_ref[...], wu_ref[...], preferred_element_type=F32)
    act = jnp.square(jnp.maximum(up, 0.0)).astype(BF16)
    acc_ref[...] += jnp.dot(act, wd_ref[...], preferred_element_type=F32)

    @pl.when(j == pl.num_programs(1) - 1)
    def _():
        o_ref[...] = h_ref[...] + _rms(acc_ref[...], g_ref[...])


def _ffn(hn, w_up, w_down, h, gain, tm=512, tf=512):
    T = h.shape[0]
    row = pl.BlockSpec((tm, D_MODEL), lambda i, j: (i, 0))
    return pl.pallas_call(
        _ffn_kernel,
        grid=(T // tm, D_FF // tf),
        in_specs=[row,
                  pl.BlockSpec((D_MODEL, tf), lambda i, j: (0, j)),
                  pl.BlockSpec((tf, D_MODEL), lambda i, j: (j, 0)),
                  row,
                  pl.BlockSpec((1, D_MODEL), lambda i, j: (0, 0))],
        out_specs=row,
        out_shape=jax.ShapeDtypeStruct((T, D_MODEL), F32),
        scratch_shapes=[pltpu.VMEM((tm, D_MODEL), F32)],
        compiler_params=_cparams(("parallel", "arbitrary")),
        name="ffn",
    )(hn, w_up, w_down, h, gain)


def _pad_cols(t, n):
    return jnp.pad(t, ((0, 0), (0, n - t.shape[1])))


def _pad_rows(t, n):
    return jnp.pad(t, ((0, n - t.shape[0]), (0, 0)))


def _split_cols(t, sizes):
    idx = [int(i) for i in np.cumsum(sizes)[:-1]]
    return jnp.split(t, idx, axis=-1)


def _permute_in_cols(t):
    shift_sizes = [RWKV_DIM] * 3 + [DECAY_LORA] * 2 + [ICLR_LORA] * 2 + [GATE_LORA]
    q, ak, av, rw, ga, gr = _split_cols(t, [D_MODEL, KV_COLS, KV_COLS, sum(shift_sizes), D_MODEL, D_MODEL])
    r, k, v, wdf, wdb, adf, adb, gd = _split_cols(rw, shift_sizes)
    lora = jnp.concatenate([_pad_cols(p, LORA_PAD) for p in (wdf, wdb, adf, adb)] + [gd], axis=1)
    return jnp.concatenate([q, ga, gr, r, k, v, ak, av, _pad_cols(lora, LORA_COLS)], axis=1)


def _rope_tables(T):
    pos = jnp.arange(T, dtype=F32)
    inv_freq = ROPE_THETA ** (-jnp.arange(0, ATT_HEAD_DIM, 2, dtype=F32) / ATT_HEAD_DIM)
    ang = pos[:, None] * inv_freq[None, :]
    cos, sin = jnp.cos(ang), jnp.sin(ang)
    return jnp.concatenate([cos, cos], axis=1), jnp.concatenate([-sin, sin], axis=1)


def kernel(x, norm_pre_mix, w_in, mu_shift, attn_sink, w0_fwd, w_up_fwd, w0_bwd, w_up_bwd, a0_fwd, a_up_fwd, a0_bwd, a_up_bwd, g_up, k_k, k_a, r_k, ln_x_gain, ln_x_bias, w_out, norm_post_mix, norm_pre_ffn, w_ffn_up, w_ffn_down, norm_post_ffn):
    B, T, _ = x.shape
    depth = w_in.shape[0]
    cos2, sin2 = _rope_tables(T)
    lane = np.arange(LANES)
    ones_bd = jnp.asarray((lane[:, None] // RWKV_HEAD_DIM) == (lane[None, :] // RWKV_HEAD_DIM), F32)
    row = lambda t: t.reshape(1, -1)
    outs = []
    for bi in range(B):
        h = x[bi]
        for l in range(depth):
            w_perm = _permute_in_cols(w_in[l]).astype(BF16)
            mu = mu_shift[l].reshape(1, -1)
            shift_sizes = [RWKV_DIM] * 3 + [DECAY_LORA] * 2 + [ICLR_LORA] * 2 + [GATE_LORA]
            mr, mk, mv, m1, m2, m3, m4, mg = _split_cols(mu, shift_sizes)
            mu_rkv = jnp.concatenate([mr, mk, mv], axis=1)
            mu_lora = _pad_cols(jnp.concatenate([_pad_cols(p, LORA_PAD) for p in (m1, m2, m3, m4)] + [mg], axis=1),
                                LORA_COLS)
            cols = _inproj(h, row(norm_pre_mix[l]), w_perm)
            att = _attention(cols, attn_sink[l], cos2, sin2)
            vecs = [row(t[l]) for t in (w0_fwd, w0_bwd, a0_fwd, a0_bwd, k_k, k_a, r_k)]
            r, v, kk, lwf, bf, kf, lwb, bb, kb, g, bonus = _rwkv_prep(
                cols, mu_rkv, mu_lora, vecs,
                _pad_rows(w_up_fwd[l], LORA_PAD), _pad_rows(w_up_bwd[l], LORA_PAD),
                _pad_rows(a_up_fwd[l], LORA_PAD), _pad_rows(a_up_bwd[l], LORA_PAD), g_up[l], ones_bd)
            yf, yb = _rwkv_scan(r, v, kk, lwf, bf, kf, lwb, bb, kb)
            merged = _rwkv_post(yf, yb, bonus, g, att, cols, row(ln_x_gain[l]), row(ln_x_bias[l]), ones_bd)
            h, hn = _outproj(merged, w_out[l].astype(BF16), h, row(norm_post_mix[l]), row(norm_pre_ffn[l]))
            h = _ffn(hn, w_ffn_up[l].astype(BF16), w_ffn_down[l].astype(BF16), h, row(norm_post_ffn[l]))
        outs.append(h)
    return jnp.stack(outs, axis=0)
```

```python
import functools

import jax
import jax.numpy as jnp
import numpy as np
from jax import lax
from jax.experimental import pallas as pl
from jax.experimental.pallas import tpu as pltpu

F32 = jnp.float32
BF16 = jnp.bfloat16
LANES = 128
SUBLANES = 8

D_MODEL = 2048
ATT_HEAD_DIM = 128
ATT_HEADS = 16
ATT_KV_HEADS = 4
ATT_GROUP = 4
WINDOW = 128
BLOCK = 128
ROPE_THETA = 10000.0
RWKV_HEAD_DIM = 64
RWKV_DIM = 2048
N_PAIRS = RWKV_DIM // LANES
DECAY_LORA = 96
ICLR_LORA = 96
GATE_LORA = 256
LORA_PAD = 128
LORA_COLS = 1024
D_FF = 4 * D_MODEL
NORM_EPS = 1e-6
GN_EPS = 64e-5
MASK_VALUE = -1e30
KV_COLS = ATT_KV_HEADS * ATT_HEAD_DIM

CB_Q, CB_GA, CB_GR, CB_R, CB_K, CB_V, CB_AK, CB_AV, CB_LORA = 0, 16, 32, 48, 64, 80, 96, 100, 104
IN_COLS_PAD = 112 * LANES

CHUNK = 128
VMEM_LIMIT = 56 * 1024 * 1024

NN = (((1,), (0,)), ((), ()))
NT = (((1,), (1,)), ((), ()))
TN = (((0,), (0,)), ((), ()))


def _mm(a, b, dims=NN):
    return lax.dot_general(a.astype(BF16), b.astype(BF16), dims, preferred_element_type=F32)


def _mm_f32(a, b, dims=NN):
    return lax.dot_general(a, b, dims, precision=lax.Precision.HIGHEST, preferred_element_type=F32)


def _cparams(sem):
    return pltpu.CompilerParams(dimension_semantics=sem, vmem_limit_bytes=VMEM_LIMIT)


def _inproj_kernel(x_ref, g_ref, w_ref, o_ref, xn_ref):
    @pl.when(pl.program_id(1) == 0)
    def _():
        x = x_ref[...]
        ms = jnp.mean(x * x, axis=-1, keepdims=True)
        xn_ref[...] = (x * lax.rsqrt(ms + NORM_EPS) * g_ref[...]).astype(BF16)

    o_ref[...] = jnp.dot(xn_ref[...], w_ref[...], preferred_element_type=F32)


def _inproj(x, gain, w, tm=512, tn=2048):
    T = x.shape[0]
    n = w.shape[1]
    return pl.pallas_call(
        _inproj_kernel,
        grid=(T // tm, n // tn),
        in_specs=[
            pl.BlockSpec((tm, D_MODEL), lambda i, j: (i, 0)),
            pl.BlockSpec((1, D_MODEL), lambda i, j: (0, 0)),
            pl.BlockSpec((D_MODEL, tn), lambda i, j: (0, j)),
        ],
        out_specs=pl.BlockSpec((tm, tn), lambda i, j: (i, j)),
        out_shape=jax.ShapeDtypeStruct((T, n), F32),
        scratch_shapes=[pltpu.VMEM((tm, D_MODEL), BF16)],
        compiler_params=_cparams(("parallel", "arbitrary")),
        name="inproj",
    )(x, gain, w)


def _rope(x, c, s):
    return x * c + pltpu.roll(x, ATT_HEAD_DIM // 2, 1) * s


def _attn_kernel(sink_ref, q_ref, kp_ref, kc_ref, kn_ref, vp_ref, vc_ref, vn_ref, gate_ref,
                 cc_ref, sc_ref, cp_ref, sp_ref, cn_ref, sn_ref, o_ref, *, seq):
    i = pl.program_id(0)
    cc, sc = cc_ref[...], sc_ref[...]
    cp, sp = cp_ref[...], sp_ref[...]
    cn, sn = cn_ref[...], sn_ref[...]
    qi = lax.broadcasted_iota(jnp.int32, (BLOCK, 3 * BLOCK), 0)
    sj = lax.broadcasted_iota(jnp.int32, (BLOCK, 3 * BLOCK), 1)
    kpos = (i - 1) * BLOCK + sj
    valid = (jnp.abs(sj - BLOCK - qi) <= WINDOW) & (kpos >= 0) & (kpos < seq)
    valid4 = jnp.concatenate([valid] * ATT_GROUP, axis=0)
    rowg = lax.broadcasted_iota(jnp.int32, (ATT_GROUP * BLOCK, 1), 0) // BLOCK
    scale = ATT_HEAD_DIM ** -0.5
    for g in range(ATT_KV_HEADS):
        ks = slice(g * ATT_HEAD_DIM, (g + 1) * ATT_HEAD_DIM)
        kw = jnp.concatenate([_rope(kp_ref[:, ks], cp, sp), _rope(kc_ref[:, ks], cc, sc),
                              _rope(kn_ref[:, ks], cn, sn)], axis=0)
        vw = jnp.concatenate([vp_ref[:, ks], vc_ref[:, ks], vn_ref[:, ks]], axis=0)
        heads = [g * ATT_GROUP + hh for hh in range(ATT_GROUP)]
        q4 = jnp.concatenate(
            [_rope(q_ref[:, h * ATT_HEAD_DIM:(h + 1) * ATT_HEAD_DIM], cc, sc) for h in heads], axis=0) * scale
        s = _mm(q4, kw, NT)
        s = jnp.where(valid4, s, MASK_VALUE)
        sink = jnp.zeros((ATT_GROUP * BLOCK, 1), F32)
        for hh, h in enumerate(heads):
            sink = jnp.where(rowg == hh, sink_ref[h], sink)
        m = jnp.maximum(jnp.max(s, axis=-1, keepdims=True), sink)
        p = jnp.exp(s - m)
        den = jnp.sum(p, axis=-1, keepdims=True) + jnp.exp(sink - m)
        o = _mm(p, vw) / den
        for hh, h in enumerate(heads):
            cs = slice(h * ATT_HEAD_DIM, (h + 1) * ATT_HEAD_DIM)
            o_ref[:, cs] = o[hh * BLOCK:(hh + 1) * BLOCK] * jax.nn.sigmoid(gate_ref[:, cs])


def _attention(cols, sink, cos2, sin2):
    T = cols.shape[0]
    nb = T // BLOCK
    prev = lambda i: (jnp.maximum(i - 1, 0), 0)
    cur = lambda i: (i, 0)
    nxt = lambda i: (jnp.minimum(i + 1, nb - 1), 0)
    kcb, vcb = CB_AK * LANES // KV_COLS, CB_AV * LANES // KV_COLS
    col = lambda f, cb: (lambda i: (f(i)[0], cb))
    tab = pl.BlockSpec
    return pl.pallas_call(
        functools.partial(_attn_kernel, seq=T),
        grid=(nb,),
        in_specs=[
            pl.BlockSpec(memory_space=pltpu.SMEM),
            pl.BlockSpec((BLOCK, D_MODEL), col(cur, CB_Q // 16)),
            pl.BlockSpec((BLOCK, KV_COLS), col(prev, kcb)),
            pl.BlockSpec((BLOCK, KV_COLS), col(cur, kcb)),
            pl.BlockSpec((BLOCK, KV_COLS), col(nxt, kcb)),
            pl.BlockSpec((BLOCK, KV_COLS), col(prev, vcb)),
            pl.BlockSpec((BLOCK, KV_COLS), col(cur, vcb)),
            pl.BlockSpec((BLOCK, KV_COLS), col(nxt, vcb)),
            pl.BlockSpec((BLOCK, D_MODEL), col(cur, CB_GA // 16)),
            tab((BLOCK, LANES), cur), tab((BLOCK, LANES), cur),
            tab((BLOCK, LANES), prev), tab((BLOCK, LANES), prev),
            tab((BLOCK, LANES), nxt), tab((BLOCK, LANES), nxt),
        ],
        out_specs=pl.BlockSpec((BLOCK, D_MODEL), cur),
        out_shape=jax.ShapeDtypeStruct((T, D_MODEL), F32),
        compiler_params=_cparams(("parallel",)),
        name="attention",
    )(sink, cols, cols, cols, cols, cols, cols, cols, cols, cos2, sin2, cos2, sin2, cos2, sin2)


def _head_sum(x, ones_bd):
    return _mm_f32(x, ones_bd)


def _prep_kernel(r_ref, rp_ref, rn_ref, k_ref, kp_ref, kn_ref, v_ref, vp_ref, vn_ref,
                 lo_ref, lop_ref, lon_ref, mur_ref, muk_ref, muv_ref, mul_ref,
                 w0f_ref, w0b_ref, a0f_ref, a0b_ref, kk_ref, ka_ref, rk_ref,
                 wuf_ref, wub_ref, auf_ref, aub_ref, gup_ref, bd_ref,
                 r_o, v_o, kk_o, lwf_o, bf_o, kf_o, lwb_o, bb_o, kb_o, g_o, bonus_o):
    i = pl.program_id(0)
    first = i == 0
    last = i == pl.num_programs(0) - 1
    tm = r_ref.shape[0]

    def shift(c_ref, p_ref, n_ref, mu):
        c = c_ref[...]
        row = lax.broadcasted_iota(jnp.int32, c.shape, 0)
        prow = jnp.where(first, 0.0, p_ref[SUBLANES - 1:SUBLANES, :])
        nrow = jnp.where(last, 0.0, n_ref[0:1, :])
        prev = jnp.where(row == 0, prow, pltpu.roll(c, 1, 0))
        nxt = jnp.where(row == tm - 1, nrow, pltpu.roll(c, tm - 1, 0))
        return c + mu * (0.5 * (prev + nxt) - c)

    r = shift(r_ref, rp_ref, rn_ref, mur_ref[...])
    k = shift(k_ref, kp_ref, kn_ref, muk_ref[...])
    v = shift(v_ref, vp_ref, vn_ref, muv_ref[...])
    lo = shift(lo_ref, lop_ref, lon_ref, mul_ref[...])
    ones_bd = bd_ref[...]

    kk = k * kk_ref[...]
    kk = kk / jnp.maximum(jnp.sqrt(_head_sum(kk * kk, ones_bd)), 1e-12)
    k_a = ka_ref[...]

    def direction(wd, ad, w0, wu, a0, au):
        w_log = -jax.nn.softplus(-(w0 + _mm_f32(jnp.tanh(wd), wu))) - 0.5
        lw = -jnp.exp(w_log)
        a = jax.nn.sigmoid(a0 + _mm_f32(ad, au))
        k_mod = k * (1.0 + (a - 1.0) * k_a)
        return lw, kk * a, k_mod

    P = LORA_PAD
    lwf, bf, kf = direction(lo[:, 0:P], lo[:, 2 * P:3 * P], w0f_ref[...], wuf_ref[...], a0f_ref[...], auf_ref[...])
    lwb, bb, kb = direction(lo[:, P:2 * P], lo[:, 3 * P:4 * P], w0b_ref[...], wub_ref[...], a0b_ref[...], aub_ref[...])
    g = _mm_f32(jax.nn.sigmoid(lo[:, 4 * P:4 * P + GATE_LORA]), gup_ref[...])
    bonus = _head_sum(r * (0.5 * (kf + kb)) * rk_ref[...], ones_bd) * v

    r_o[...] = r
    v_o[...] = v
    kk_o[...] = kk
    lwf_o[...] = lwf
    bf_o[...] = bf
    kf_o[...] = kf
    lwb_o[...] = lwb
    bb_o[...] = bb
    kb_o[...] = kb
    g_o[...] = g
    bonus_o[...] = bonus


def _rwkv_prep(cols, mu_rkv, mu_lora, vecs, w_up_f, w_up_b, a_up_f, a_up_b, g_up, ones_bd, tm=512):
    T = cols.shape[0]
    nt = T // tm
    hb = tm // SUBLANES
    nhb = T // SUBLANES
    main = lambda cb: pl.BlockSpec((tm, LANES), lambda i, j: (i, cb + j))
    hprev = lambda cb: pl.BlockSpec((SUBLANES, LANES), lambda i, j: (jnp.maximum(i * hb - 1, 0), cb + j))
    hnext = lambda cb: pl.BlockSpec((SUBLANES, LANES), lambda i, j: (jnp.minimum((i + 1) * hb, nhb - 1), cb + j))
    lcb = CB_LORA * LANES // LORA_COLS
    vec = lambda off: pl.BlockSpec((1, LANES), lambda i, j: (0, off + j))
    up = lambda rows: pl.BlockSpec((rows, LANES), lambda i, j: (0, j))
    in_specs = []
    for cb in (CB_R, CB_K, CB_V):
        in_specs += [main(cb), hprev(cb), hnext(cb)]
    in_specs += [
        pl.BlockSpec((tm, LORA_COLS), lambda i, j: (i, lcb)),
        pl.BlockSpec((SUBLANES, LORA_COLS), lambda i, j: (jnp.maximum(i * hb - 1, 0), lcb)),
        pl.BlockSpec((SUBLANES, LORA_COLS), lambda i, j: (jnp.minimum((i + 1) * hb, nhb - 1), lcb)),
        vec(0), vec(N_PAIRS), vec(2 * N_PAIRS),
        pl.BlockSpec((1, LORA_COLS), lambda i, j: (0, 0)),
    ]
    in_specs += [vec(0)] * 7
    in_specs += [up(LORA_PAD)] * 4 + [up(GATE_LORA)]
    in_specs += [pl.BlockSpec((LANES, LANES), lambda i, j: (0, 0))]
    out = jax.ShapeDtypeStruct((T, RWKV_DIM), F32)
    return pl.pallas_call(
        _prep_kernel,
        grid=(nt, N_PAIRS),
        in_specs=in_specs,
        out_specs=[pl.BlockSpec((tm, LANES), lambda i, j: (i, j))] * 11,
        out_shape=[out] * 11,
        compiler_params=_cparams(("parallel", "arbitrary")),
        name="rwkv_prep",
    )(cols, cols, cols, cols, cols, cols, cols, cols, cols, cols, cols, cols,
      mu_rkv, mu_rkv, mu_rkv, mu_lora, *vecs, w_up_f, w_up_b, a_up_f, a_up_b, g_up, ones_bd)


def _tri_inverse_all(mats):
    L = mats[0].shape[0]
    row = lax.broadcasted_iota(jnp.int32, (L, L), 0)
    col = lax.broadcasted_iota(jnp.int32, (L, L), 1)
    same = lambda sh: (row >> sh) == (col >> sh)
    eye = jnp.where(row == col, 1.0, 0.0)
    ds = [eye + jnp.where(same(1), a, 0.0) for a in mats]
    sh = 1
    while (1 << sh) < L:
        level = same(sh + 1) & jnp.logical_not(same(sh))
        offs = [jnp.where(level, a, 0.0).astype(BF16) for a in mats]
        dbs = [d.astype(BF16) for d in ds]
        ts = [_mm(o, db) for o, db in zip(offs, dbs)]
        ds = [d + _mm(db, t) for d, db, t in zip(ds, dbs, ts)]
        sh += 1
    return ds


def _chunk_all(insts):
    L = insts[0][0].shape[0]
    n_inst = len(insts)
    row = lax.broadcasted_iota(jnp.int32, (L, L), 0)
    col = lax.broadcasted_iota(jnp.int32, (L, L), 1)
    incl = {False: col <= row, True: col >= row}
    strict = {False: col < row, True: col > row}
    ones = {rev: jnp.where(incl[rev], 1.0, 0.0) for rev in (False, True)}
    incl2 = {rev: jnp.concatenate([incl[rev]] * 2, axis=1) for rev in (False, True)}
    strict2 = {rev: jnp.concatenate([strict[rev]] * 2, axis=1) for rev in (False, True)}
    lane = lax.broadcasted_iota(jnp.int32, (1, LANES), 1)
    h0 = lane < RWKV_HEAD_DIM
    srow = lax.broadcasted_iota(jnp.int32, (LANES, LANES), 0)
    scol = lax.broadcasted_iota(jnp.int32, (LANES, LANES), 1)
    same_head = (srow >= RWKV_HEAD_DIM) == (scol >= RWKV_HEAD_DIM)

    cums = [_mm_f32(ones[inst[7]], inst[3]) for inst in insts]
    pre = []
    for (r, v, kk, lw, b, k, s, rev), cum in zip(insts, cums):
        tot = cum[0:1] if rev else cum[L - 1:L]
        mid = L // 2 if rev else L // 2 - 1
        rho = cum[mid:mid + 1]
        e1 = jnp.exp(cum - rho)
        e2 = jnp.exp(rho - cum)
        er = jnp.exp(rho)
        et = jnp.exp(tot - rho)
        a_t = -kk * e1 * jnp.exp(-lw)
        r_t = r * e1
        b_t = b * e2
        k_t = k * e2
        pre.append((a_t, r_t, b_t, k_t, er, et))
    grams = []
    for a_t, r_t, b_t, k_t, _, _ in pre:
        zero = jnp.zeros_like(a_t)
        lhs = jnp.concatenate([jnp.where(h0, a_t, zero), jnp.where(h0, zero, a_t),
                               jnp.where(h0, r_t, zero), jnp.where(h0, zero, r_t)], axis=0)
        rhs = jnp.concatenate([b_t, k_t], axis=0)
        grams.append(_mm(lhs, rhs, NT))
    a_bk, r_bk = [], []
    for inst, gram in zip(insts, grams):
        rev = inst[7]
        a_bk.append([jnp.where(strict2[rev], gram[hh * L:(hh + 1) * L], 0.0) for hh in range(2)])
        r_bk.append([jnp.where(incl2[rev], gram[(2 + hh) * L:(3 + hh) * L], 0.0) for hh in range(2)])
    t_inv = _tri_inverse_all([a_bk[n][hh][:, :L] for n in range(n_inst) for hh in range(2)])
    xs = [_mm(jnp.concatenate([a_t * er, r_t * er], axis=0), inst[6], NT)
          for inst, (a_t, r_t, _, _, er, _) in zip(insts, pre)]
    akv = [[_mm(a_bk[n][hh][:, L:], inst[1]) for hh in range(2)] for n, inst in enumerate(insts)]
    x = [xs[n][:L] + jnp.where(h0, akv[n][0], akv[n][1]) for n in range(n_inst)]
    u = [jnp.where(h0, _mm(t_inv[2 * n], x[n]), _mm(t_inv[2 * n + 1], x[n])) for n in range(n_inst)]
    z = [jnp.concatenate([u[n], inst[1]], axis=0) for n, inst in enumerate(insts)]
    out = []
    for n, inst in enumerate(insts):
        _, _, b_t, k_t, er, et = pre[n]
        y = xs[n][L:] + jnp.where(h0, _mm(r_bk[n][0], z[n]), _mm(r_bk[n][1], z[n]))
        w = jnp.concatenate([b_t * et, k_t * et], axis=0)
        s_new = inst[6] * (er * et) + jnp.where(same_head, _mm(z[n], w, TN), 0.0)
        out.append((y, s_new))
    return out


def _scan_kernel(rf_ref, vf_ref, kkf_ref, lwf_ref, bf_ref, kf_ref,
                 rb_ref, vb_ref, kkb_ref, lwb_ref, bb_ref, kb_ref,
                 yf_ref, yb_ref, s_ref):
    @pl.when(pl.program_id(1) == 0)
    def _():
        s_ref[...] = jnp.zeros_like(s_ref)

    n_pairs = s_ref.shape[1]
    insts = []
    for p in range(n_pairs):
        cs = slice(p * LANES, (p + 1) * LANES)
        insts.append((rf_ref[:, cs], vf_ref[:, cs], kkf_ref[:, cs], lwf_ref[:, cs], bf_ref[:, cs], kf_ref[:, cs],
                      s_ref[0, p], False))
        insts.append((rb_ref[:, cs], vb_ref[:, cs], kkb_ref[:, cs], lwb_ref[:, cs], bb_ref[:, cs], kb_ref[:, cs],
                      s_ref[1, p], True))
    res = _chunk_all(insts)
    for p in range(n_pairs):
        cs = slice(p * LANES, (p + 1) * LANES)
        yf_ref[:, cs], s_ref[0, p] = res[2 * p]
        yb_ref[:, cs], s_ref[1, p] = res[2 * p + 1]


def _rwkv_scan(r, v, kk, lwf, bf, kf, lwb, bb, kb, pairs_per_step=4):
    T = r.shape[0]
    nc = T // CHUNK
    width = pairs_per_step * LANES
    fwd = pl.BlockSpec((CHUNK, width), lambda p, c: (c, p))
    bwd = pl.BlockSpec((CHUNK, width), lambda p, c: (nc - 1 - c, p))
    out = jax.ShapeDtypeStruct((T, RWKV_DIM), F32)
    return pl.pallas_call(
        _scan_kernel,
        grid=(N_PAIRS // pairs_per_step, nc),
        in_specs=[fwd] * 6 + [bwd] * 6,
        out_specs=[fwd, bwd],
        out_shape=[out, out],
        scratch_shapes=[pltpu.VMEM((2, pairs_per_step, LANES, LANES), F32)],
        compiler_params=_cparams(("parallel", "arbitrary")),
        name="rwkv_scan",
    )(r, v, kk, lwf, bf, kf, r, v, kk, lwb, bb, kb)


def _post_kernel(yf_ref, yb_ref, bonus_ref, g_ref, att_ref, gr_ref, gain_ref, bias_ref, bd_ref, o_ref):
    ones_bd = bd_ref[...]
    y = yf_ref[...] + yb_ref[...]
    inv_n = 1.0 / RWKV_HEAD_DIM
    mean = _head_sum(y, ones_bd) * inv_n
    d = y - mean
    var = _head_sum(d * d, ones_bd) * inv_n
    yn = d * lax.rsqrt(var + GN_EPS) * gain_ref[...] + bias_ref[...]
    o_rwkv = (yn + bonus_ref[...]) * g_ref[...]
    o_ref[...] = (att_ref[...] + jax.nn.sigmoid(gr_ref[...]) * o_rwkv).astype(o_ref.dtype)


def _rwkv_post(yf, yb, bonus, g, att, cols, gain, bias, ones_bd, tm=1024):
    T = yf.shape[0]
    blk = pl.BlockSpec((tm, LANES), lambda i, j: (i, j))
    vec = pl.BlockSpec((1, LANES), lambda i, j: (0, j))
    return pl.pallas_call(
        _post_kernel,
        grid=(T // tm, N_PAIRS),
        in_specs=[blk, blk, blk, blk, blk,
                  pl.BlockSpec((tm, LANES), lambda i, j: (i, CB_GR + j)),
                  vec, vec, pl.BlockSpec((LANES, LANES), lambda i, j: (0, 0))],
        out_specs=blk,
        out_shape=jax.ShapeDtypeStruct((T, D_MODEL), BF16),
        compiler_params=_cparams(("parallel", "parallel")),
        name="rwkv_post",
    )(yf, yb, bonus, g, att, cols, gain, bias, ones_bd)


def _rms(x, gain):
    return x * lax.rsqrt(jnp.mean(x * x, axis=-1, keepdims=True) + NORM_EPS) * gain


def _outproj_kernel(m_ref, w_ref, x_ref, gpost_ref, gpre_ref, h_ref, hn_ref):
    mix = jnp.dot(m_ref[...], w_ref[...], preferred_element_type=F32)
    h = x_ref[...] + _rms(mix, gpost_ref[...])
    h_ref[...] = h
    hn_ref[...] = _rms(h, gpre_ref[...]).astype(BF16)


def _outproj(merged, w_out, x, g_post, g_pre, tm=256):
    T = x.shape[0]
    row = pl.BlockSpec((tm, D_MODEL), lambda i: (i, 0))
    vec = pl.BlockSpec((1, D_MODEL), lambda i: (0, 0))
    return pl.pallas_call(
        _outproj_kernel,
        grid=(T // tm,),
        in_specs=[row, pl.BlockSpec((D_MODEL, D_MODEL), lambda i: (0, 0)), row, vec, vec],
        out_specs=[row, row],
        out_shape=[jax.ShapeDtypeStruct((T, D_MODEL), F32), jax.ShapeDtypeStruct((T, D_MODEL), BF16)],
        compiler_params=_cparams(("parallel",)),
        name="outproj",
    )(merged, w_out, x, g_post, g_pre)


def _ffn_kernel(hn_ref, wu_ref, wd_ref, h_ref, g_ref, o_ref, acc_ref):
    j = pl.program_id(1)

    @pl.when(j == 0)
    def _():
        acc_ref[...] = jnp.zeros_like(acc_ref)

    up = jnp.dot(hn_ref[...], wu_ref[...], preferred_element_type=F32)
    act = jnp.square(jnp.maximum(up, 0.0)).astype(BF16)
    acc_ref[...] += jnp.dot(act, wd_ref[...], preferred_element_type=F32)

    @pl.when(j == pl.num_programs(1) - 1)
    def _():
        o_ref[...] = h_ref[...] + _rms(acc_ref[...], g_ref[...])


def _ffn(hn, w_up, w_down, h, gain, tm=512, tf=512):
    T = h.shape[0]
    row = pl.BlockSpec((tm, D_MODEL), lambda i, j: (i, 0))
    return pl.pallas_call(
        _ffn_kernel,
        grid=(T // tm, D_FF // tf),
        in_specs=[row,
                  pl.BlockSpec((D_MODEL, tf), lambda i, j: (0, j)),
                  pl.BlockSpec((tf, D_MODEL), lambda i, j: (j, 0)),
                  row,
                  pl.BlockSpec((1, D_MODEL), lambda i, j: (0, 0))],
        out_specs=row,
        out_shape=jax.ShapeDtypeStruct((T, D_MODEL), F32),
        scratch_shapes=[pltpu.VMEM((tm, D_MODEL), F32)],
        compiler_params=_cparams(("parallel", "arbitrary")),
        name="ffn",
    )(hn, w_up, w_down, h, gain)


def _pad_cols(t, n):
    return jnp.pad(t, ((0, 0), (0, n - t.shape[1])))


def _pad_rows(t, n):
    return jnp.pad(t, ((0, n - t.shape[0]), (0, 0)))


def _split_cols(t, sizes):
    idx = [int(i) for i in np.cumsum(sizes)[:-1]]
    return jnp.split(t, idx, axis=-1)


def _permute_in_cols(t):
    shift_sizes = [RWKV_DIM] * 3 + [DECAY_LORA] * 2 + [ICLR_LORA] * 2 + [GATE_LORA]
    q, ak, av, rw, ga, gr = _split_cols(t, [D_MODEL, KV_COLS, KV_COLS, sum(shift_sizes), D_MODEL, D_MODEL])
    r, k, v, wdf, wdb, adf, adb, gd = _split_cols(rw, shift_sizes)
    lora = jnp.concatenate([_pad_cols(p, LORA_PAD) for p in (wdf, wdb, adf, adb)] + [gd], axis=1)
    return jnp.concatenate([q, ga, gr, r, k, v, ak, av, _pad_cols(lora, LORA_COLS)], axis=1)


def _rope_tables(T):
    pos = jnp.arange(T, dtype=F32)
    inv_freq = ROPE_THETA ** (-jnp.arange(0, ATT_HEAD_DIM, 2, dtype=F32) / ATT_HEAD_DIM)
    ang = pos[:, None] * inv_freq[None, :]
    cos, sin = jnp.cos(ang), jnp.sin(ang)
    return jnp.concatenate([cos, cos], axis=1), jnp.concatenate([-sin, sin], axis=1)


def kernel(x, norm_pre_mix, w_in, mu_shift, attn_sink, w0_fwd, w_up_fwd, w0_bwd, w_up_bwd, a0_fwd, a_up_fwd, a0_bwd, a_up_bwd, g_up, k_k, k_a, r_k, ln_x_gain, ln_x_bias, w_out, norm_post_mix, norm_pre_ffn, w_ffn_up, w_ffn_down, norm_post_ffn):
    B, T, _ = x.shape
    depth = w_in.shape[0]
    cos2, sin2 = _rope_tables(T)
    lane = np.arange(LANES)
    ones_bd = jnp.asarray((lane[:, None] // RWKV_HEAD_DIM) == (lane[None, :] // RWKV_HEAD_DIM), F32)
    row = lambda t: t.reshape(1, -1)
    outs = []
    for bi in range(B):
        h = x[bi]
        for l in range(depth):
            w_perm = _permute_in_cols(w_in[l]).astype(BF16)
            mu = mu_shift[l].reshape(1, -1)
            shift_sizes = [RWKV_DIM] * 3 + [DECAY_LORA] * 2 + [ICLR_LORA] * 2 + [GATE_LORA]
            mr, mk, mv, m1, m2, m3, m4, mg = _split_cols(mu, shift_sizes)
            mu_rkv = jnp.concatenate([mr, mk, mv], axis=1)
            mu_lora = _pad_cols(jnp.concatenate([_pad_cols(p, LORA_PAD) for p in (m1, m2, m3, m4)] + [mg], axis=1),
                                LORA_COLS)
            cols = _inproj(h, row(norm_pre_mix[l]), w_perm)
            att = _attention(cols, attn_sink[l], cos2, sin2)
            vecs = [row(t[l]) for t in (w0_fwd, w0_bwd, a0_fwd, a0_bwd, k_k, k_a, r_k)]
            r, v, kk, lwf, bf, kf, lwb, bb, kb, g, bonus = _rwkv_prep(
                cols, mu_rkv, mu_lora, vecs,
                _pad_rows(w_up_fwd[l], LORA_PAD), _pad_rows(w_up_bwd[l], LORA_PAD),
                _pad_rows(a_up_fwd[l], LORA_PAD), _pad_rows(a_up_bwd[l], LORA_PAD), g_up[l], ones_bd)
            yf, yb = _rwkv_scan(r, v, kk, lwf, bf, kf, lwb, bb, kb)
            merged = _rwkv_post(yf, yb, bonus, g, att, cols, row(ln_x_gain[l]), row(ln_x_bias[l]), ones_bd)
            h, hn = _outproj(merged, w_out[l].astype(BF16), h, row(norm_post_mix[l]), row(norm_pre_ffn[l]))
            h = _ffn(hn, w_ffn_up[l].astype(BF16), w_ffn_down[l].astype(BF16), h, row(norm_post_ffn[l]))
        outs.append(h)
    return jnp.stack(outs, axis=0)
```

```python
import functools

import jax
import jax.numpy as jnp
import numpy as np
from jax import lax
from jax.experimental import pallas as pl
from jax.experimental.pallas import tpu as pltpu

F32 = jnp.float32
BF16 = jnp.bfloat16
LANES = 128
SUBLANES = 8

D_MODEL = 2048
ATT_HEAD_DIM = 128
ATT_HEADS = 16
ATT_KV_HEADS = 4
ATT_GROUP = 4
WINDOW = 128
BLOCK = 128
ROPE_THETA = 10000.0
RWKV_HEAD_DIM = 64
RWKV_DIM = 2048
N_PAIRS = RWKV_DIM // LANES
DECAY_LORA = 96
ICLR_LORA = 96
GATE_LORA = 256
LORA_PAD = 128
LORA_COLS = 1024
D_FF = 4 * D_MODEL
NORM_EPS = 1e-6
GN_EPS = 64e-5
MASK_VALUE = -1e30
KV_COLS = ATT_KV_HEADS * ATT_HEAD_DIM

CB_Q, CB_GA, CB_GR, CB_R, CB_K, CB_V, CB_AK, CB_AV, CB_LORA = 0, 16, 32, 48, 64, 80, 96, 100, 104
IN_COLS_PAD = 112 * LANES

CHUNK = 128
VMEM_LIMIT = 56 * 1024 * 1024

NN = (((1,), (0,)), ((), ()))
NT = (((1,), (1,)), ((), ()))
TN = (((0,), (0,)), ((), ()))


def _mm(a, b, dims=NN):
    return lax.dot_general(a.astype(BF16), b.astype(BF16), dims, preferred_element_type=F32)


def _split_bf16(x, parts):
    out = []
    for _ in range(parts - 1):
        hi = x.astype(BF16)
        out.append(hi)
        x = x - hi.astype(F32)
    out.append(x.astype(BF16))
    return out


def _mm_exact_lhs(a, b, parts):
    a = a.astype(BF16)
    acc = None
    for term in _split_bf16(b, parts):
        p = lax.dot_general(a, term, NN, preferred_element_type=F32)
        acc = p if acc is None else acc + p
    return acc


def _mm_exact_rhs(a, b, parts):
    b = b.astype(BF16)
    acc = None
    for term in _split_bf16(a, parts):
        p = lax.dot_general(term, b, NN, preferred_element_type=F32)
        acc = p if acc is None else acc + p
    return acc


def _cparams(sem):
    return pltpu.CompilerParams(dimension_semantics=sem, vmem_limit_bytes=VMEM_LIMIT)


def _inproj_kernel(x_ref, g_ref, w_ref, o_ref, xn_ref):
    @pl.when(pl.program_id(1) == 0)
    def _():
        x = x_ref[...]
        ms = jnp.mean(x * x, axis=-1, keepdims=True)
        xn_ref[...] = (x * lax.rsqrt(ms + NORM_EPS) * g_ref[...]).astype(BF16)

    o_ref[...] = jnp.dot(xn_ref[...], w_ref[...], preferred_element_type=F32)


def _inproj(x, gain, w, tm=512, tn=2048):
    T = x.shape[0]
    n = w.shape[1]
    return pl.pallas_call(
        _inproj_kernel,
        grid=(T // tm, n // tn),
        in_specs=[
            pl.BlockSpec((tm, D_MODEL), lambda i, j: (i, 0)),
            pl.BlockSpec((1, D_MODEL), lambda i, j: (0, 0)),
            pl.BlockSpec((D_MODEL, tn), lambda i, j: (0, j)),
        ],
        out_specs=pl.BlockSpec((tm, tn), lambda i, j: (i, j)),
        out_shape=jax.ShapeDtypeStruct((T, n), F32),
        scratch_shapes=[pltpu.VMEM((tm, D_MODEL), BF16)],
        compiler_params=_cparams(("parallel", "arbitrary")),
        name="inproj",
    )(x, gain, w)


def _rope(x, c, s):
    return x * c + pltpu.roll(x, ATT_HEAD_DIM // 2, 1) * s


def _attn_kernel(sink_ref, q_ref, kp_ref, kc_ref, kn_ref, vp_ref, vc_ref, vn_ref, gate_ref,
                 cc_ref, sc_ref, cp_ref, sp_ref, cn_ref, sn_ref, o_ref, *, seq):
    i = pl.program_id(0)
    cc, sc = cc_ref[...], sc_ref[...]
    cp, sp = cp_ref[...], sp_ref[...]
    cn, sn = cn_ref[...], sn_ref[...]
    qi = lax.broadcasted_iota(jnp.int32, (BLOCK, 3 * BLOCK), 0)
    sj = lax.broadcasted_iota(jnp.int32, (BLOCK, 3 * BLOCK), 1)
    kpos = (i - 1) * BLOCK + sj
    valid = (jnp.abs(sj - BLOCK - qi) <= WINDOW) & (kpos >= 0) & (kpos < seq)
    valid4 = jnp.concatenate([valid] * ATT_GROUP, axis=0)
    rowg = lax.broadcasted_iota(jnp.int32, (ATT_GROUP * BLOCK, 1), 0) // BLOCK
    scale = ATT_HEAD_DIM ** -0.5
    for g in range(ATT_KV_HEADS):
        ks = slice(g * ATT_HEAD_DIM, (g + 1) * ATT_HEAD_DIM)
        kw = jnp.concatenate([_rope(kp_ref[:, ks], cp, sp), _rope(kc_ref[:, ks], cc, sc),
                              _rope(kn_ref[:, ks], cn, sn)], axis=0)
        vw = jnp.concatenate([vp_ref[:, ks], vc_ref[:, ks], vn_ref[:, ks]], axis=0)
        heads = [g * ATT_GROUP + hh for hh in range(ATT_GROUP)]
        q4 = jnp.concatenate(
            [_rope(q_ref[:, h * ATT_HEAD_DIM:(h + 1) * ATT_HEAD_DIM], cc, sc) for h in heads], axis=0) * scale
        s = _mm(q4, kw, NT)
        s = jnp.where(valid4, s, MASK_VALUE)
        sink = jnp.zeros((ATT_GROUP * BLOCK, 1), F32)
        for hh, h in enumerate(heads):
            sink = jnp.where(rowg == hh, sink_ref[h], sink)
        m = jnp.maximum(jnp.max(s, axis=-1, keepdims=True), sink)
        p = jnp.exp(s - m)
        den = jnp.sum(p, axis=-1, keepdims=True) + jnp.exp(sink - m)
        o = _mm(p, vw) / den
        for hh, h in enumerate(heads):
            cs = slice(h * ATT_HEAD_DIM, (h + 1) * ATT_HEAD_DIM)
            o_ref[:, cs] = o[hh * BLOCK:(hh + 1) * BLOCK] * jax.nn.sigmoid(gate_ref[:, cs])


def _attention(cols, sink, cos2, sin2):
    T = cols.shape[0]
    nb = T // BLOCK
    prev = lambda i: (jnp.maximum(i - 1, 0), 0)
    cur = lambda i: (i, 0)
    nxt = lambda i: (jnp.minimum(i + 1, nb - 1), 0)
    kcb, vcb = CB_AK * LANES // KV_COLS, CB_AV * LANES // KV_COLS
    col = lambda f, cb: (lambda i: (f(i)[0], cb))
    tab = pl.BlockSpec
    return pl.pallas_call(
        functools.partial(_attn_kernel, seq=T),
        grid=(nb,),
        in_specs=[
            pl.BlockSpec(memory_space=pltpu.SMEM),
            pl.BlockSpec((BLOCK, D_MODEL), col(cur, CB_Q // 16)),
            pl.BlockSpec((BLOCK, KV_COLS), col(prev, kcb)),
            pl.BlockSpec((BLOCK, KV_COLS), col(cur, kcb)),
            pl.BlockSpec((BLOCK, KV_COLS), col(nxt, kcb)),
            pl.BlockSpec((BLOCK, KV_COLS), col(prev, vcb)),
            pl.BlockSpec((BLOCK, KV_COLS), col(cur, vcb)),
            pl.BlockSpec((BLOCK, KV_COLS), col(nxt, vcb)),
            pl.BlockSpec((BLOCK, D_MODEL), col(cur, CB_GA // 16)),
            tab((BLOCK, LANES), cur), tab((BLOCK, LANES), cur),
            tab((BLOCK, LANES), prev), tab((BLOCK, LANES), prev),
            tab((BLOCK, LANES), nxt), tab((BLOCK, LANES), nxt),
        ],
        out_specs=pl.BlockSpec((BLOCK, D_MODEL), cur),
        out_shape=jax.ShapeDtypeStruct((T, D_MODEL), F32),
        compiler_params=_cparams(("parallel",)),
        name="attention",
    )(sink, cols, cols, cols, cols, cols, cols, cols, cols, cos2, sin2, cos2, sin2, cos2, sin2)


def _head_sum(x, ones_bd):
    return _mm_exact_rhs(x, ones_bd, 2)


def _prep_kernel(r_ref, rp_ref, rn_ref, k_ref, kp_ref, kn_ref, v_ref, vp_ref, vn_ref,
                 lo_ref, lop_ref, lon_ref, mur_ref, muk_ref, muv_ref, mul_ref,
                 w0f_ref, w0b_ref, a0f_ref, a0b_ref, kk_ref, ka_ref, rk_ref,
                 wuf_ref, wub_ref, auf_ref, aub_ref, gup_ref, bd_ref,
                 r_o, v_o, kk_o, lwf_o, bf_o, kf_o, lwb_o, bb_o, kb_o, g_o, bonus_o):
    i = pl.program_id(0)
    first = i == 0
    last = i == pl.num_programs(0) - 1
    tm = r_ref.shape[0]

    def shift(c_ref, p_ref, n_ref, mu):
        c = c_ref[...]
        row = lax.broadcasted_iota(jnp.int32, c.shape, 0)
        prow = jnp.where(first, 0.0, p_ref[SUBLANES - 1:SUBLANES, :])
        nrow = jnp.where(last, 0.0, n_ref[0:1, :])
        prev = jnp.where(row == 0, prow, pltpu.roll(c, 1, 0))
        nxt = jnp.where(row == tm - 1, nrow, pltpu.roll(c, tm - 1, 0))
        return c + mu * (0.5 * (prev + nxt) - c)

    r = shift(r_ref, rp_ref, rn_ref, mur_ref[...])
    k = shift(k_ref, kp_ref, kn_ref, muk_ref[...])
    v = shift(v_ref, vp_ref, vn_ref, muv_ref[...])
    lo = shift(lo_ref, lop_ref, lon_ref, mul_ref[...])
    ones_bd = bd_ref[...]

    kk = k * kk_ref[...]
    kk = kk / jnp.maximum(jnp.sqrt(_head_sum(kk * kk, ones_bd)), 1e-12)
    k_a = ka_ref[...]

    def direction(wd, ad, w0, wu, a0, au):
        w_log = -jax.nn.softplus(-(w0 + _mm(jnp.tanh(wd), wu))) - 0.5
        lw = -jnp.exp(w_log)
        a = jax.nn.sigmoid(a0 + _mm(ad, au))
        k_mod = k * (1.0 + (a - 1.0) * k_a)
        return lw, kk * a, k_mod

    P = LORA_PAD
    lwf, bf, kf = direction(lo[:, 0:P], lo[:, 2 * P:3 * P], w0f_ref[...], wuf_ref[...], a0f_ref[...], auf_ref[...])
    lwb, bb, kb = direction(lo[:, P:2 * P], lo[:, 3 * P:4 * P], w0b_ref[...], wub_ref[...], a0b_ref[...], aub_ref[...])
    g = _mm(jax.nn.sigmoid(lo[:, 4 * P:4 * P + GATE_LORA]), gup_ref[...])
    bonus = _head_sum(r * (0.5 * (kf + kb)) * rk_ref[...], ones_bd) * v

    r_o[...] = r
    v_o[...] = v
    kk_o[...] = kk
    lwf_o[...] = lwf
    bf_o[...] = bf
    kf_o[...] = kf
    lwb_o[...] = lwb
    bb_o[...] = bb
    kb_o[...] = kb
    g_o[...] = g
    bonus_o[...] = bonus


def _rwkv_prep(cols, mu_rkv, mu_lora, vecs, w_up_f, w_up_b, a_up_f, a_up_b, g_up, ones_bd, tm=512):
    T = cols.shape[0]
    nt = T // tm
    hb = tm // SUBLANES
    nhb = T // SUBLANES
    main = lambda cb: pl.BlockSpec((tm, LANES), lambda i, j: (i, cb + j))
    hprev = lambda cb: pl.BlockSpec((SUBLANES, LANES), lambda i, j: (jnp.maximum(i * hb - 1, 0), cb + j))
    hnext = lambda cb: pl.BlockSpec((SUBLANES, LANES), lambda i, j: (jnp.minimum((i + 1) * hb, nhb - 1), cb + j))
    lcb = CB_LORA * LANES // LORA_COLS
    vec = lambda off: pl.BlockSpec((1, LANES), lambda i, j: (0, off + j))
    up = lambda rows: pl.BlockSpec((rows, LANES), lambda i, j: (0, j))
    in_specs = []
    for cb in (CB_R, CB_K, CB_V):
        in_specs += [main(cb), hprev(cb), hnext(cb)]
    in_specs += [
        pl.BlockSpec((tm, LORA_COLS), lambda i, j: (i, lcb)),
        pl.BlockSpec((SUBLANES, LORA_COLS), lambda i, j: (jnp.maximum(i * hb - 1, 0), lcb)),
        pl.BlockSpec((SUBLANES, LORA_COLS), lambda i, j: (jnp.minimum((i + 1) * hb, nhb - 1), lcb)),
        vec(0), vec(N_PAIRS), vec(2 * N_PAIRS),
        pl.BlockSpec((1, LORA_COLS), lambda i, j: (0, 0)),
    ]
    in_specs += [vec(0)] * 7
    in_specs += [up(LORA_PAD)] * 4 + [up(GATE_LORA)]
    in_specs += [pl.BlockSpec((LANES, LANES), lambda i, j: (0, 0))]
    out = jax.ShapeDtypeStruct((T, RWKV_DIM), F32)
    return pl.pallas_call(
        _prep_kernel,
        grid=(nt, N_PAIRS),
        in_specs=in_specs,
        out_specs=[pl.BlockSpec((tm, LANES), lambda i, j: (i, j))] * 11,
        out_shape=[out] * 11,
        compiler_params=_cparams(("parallel", "arbitrary")),
        name="rwkv_prep",
    )(cols, cols, cols, cols, cols, cols, cols, cols, cols, cols, cols, cols,
      mu_rkv, mu_rkv, mu_rkv, mu_lora, *vecs, w_up_f, w_up_b, a_up_f, a_up_b, g_up, ones_bd)


def _tri_inverse_all(mats):
    L = mats[0].shape[0]
    row = lax.broadcasted_iota(jnp.int32, (L, L), 0)
    col = lax.broadcasted_iota(jnp.int32, (L, L), 1)
    same = lambda sh: (row >> sh) == (col >> sh)
    eye = jnp.where(row == col, 1.0, 0.0)
    ds = [eye + jnp.where(same(1), a, 0.0) for a in mats]
    sh = 1
    while (1 << sh) < L:
        level = same(sh + 1) & jnp.logical_not(same(sh))
        offs = [jnp.where(level, a, 0.0).astype(BF16) for a in mats]
        dbs = [d.astype(BF16) for d in ds]
        ts = [_mm(o, db) for o, db in zip(offs, dbs)]
        ds = [d + _mm(db, t) for d, db, t in zip(ds, dbs, ts)]
        sh += 1
    return ds


def _chunk_all(insts):
    L = insts[0][0].shape[0]
    n_inst = len(insts)
    row = lax.broadcasted_iota(jnp.int32, (L, L), 0)
    col = lax.broadcasted_iota(jnp.int32, (L, L), 1)
    incl = {False: col <= row, True: col >= row}
    strict = {False: col < row, True: col > row}
    ones = {rev: jnp.where(incl[rev], 1.0, 0.0) for rev in (False, True)}
    incl2 = {rev: jnp.concatenate([incl[rev]] * 2, axis=1) for rev in (False, True)}
    strict2 = {rev: jnp.concatenate([strict[rev]] * 2, axis=1) for rev in (False, True)}
    lane = lax.broadcasted_iota(jnp.int32, (1, LANES), 1)
    h0 = lane < RWKV_HEAD_DIM
    srow = lax.broadcasted_iota(jnp.int32, (LANES, LANES), 0)
    scol = lax.broadcasted_iota(jnp.int32, (LANES, LANES), 1)
    same_head = (srow >= RWKV_HEAD_DIM) == (scol >= RWKV_HEAD_DIM)

    cums = [_mm_exact_lhs(ones[inst[7]], inst[3], 3) for inst in insts]
    pre = []
    for (r, v, kk, lw, b, k, s, rev), cum in zip(insts, cums):
        tot = cum[0:1] if rev else cum[L - 1:L]
        mid = L // 2 if rev else L // 2 - 1
        rho = cum[mid:mid + 1]
        e1 = jnp.exp(cum - rho)
        e2 = jnp.exp(rho - cum)
        er = jnp.exp(rho)
        et = jnp.exp(tot - rho)
        a_t = -kk * e1 * jnp.exp(-lw)
        r_t = r * e1
        b_t = b * e2
        k_t = k * e2
        pre.append((a_t, r_t, b_t, k_t, er, et))
    grams = []
    for a_t, r_t, b_t, k_t, _, _ in pre:
        zero = jnp.zeros_like(a_t)
        lhs = jnp.concatenate([jnp.where(h0, a_t, zero), jnp.where(h0, zero, a_t),
                               jnp.where(h0, r_t, zero), jnp.where(h0, zero, r_t)], axis=0)
        rhs = jnp.concatenate([b_t, k_t], axis=0)
        grams.append(_mm(lhs, rhs, NT))
    a_bk, r_bk = [], []
    for inst, gram in zip(insts, grams):
        rev = inst[7]
        a_bk.append([jnp.where(strict2[rev], gram[hh * L:(hh + 1) * L], 0.0) for hh in range(2)])
        r_bk.append([jnp.where(incl2[rev], gram[(2 + hh) * L:(3 + hh) * L], 0.0) for hh in range(2)])
    t_inv = _tri_inverse_all([a_bk[n][hh][:, :L] for n in range(n_inst) for hh in range(2)])
    xs = [_mm(jnp.concatenate([a_t * er, r_t * er], axis=0), inst[6], NT)
          for inst, (a_t, r_t, _, _, er, _) in zip(insts, pre)]
    akv = [[_mm(a_bk[n][hh][:, L:], inst[1]) for hh in range(2)] for n, inst in enumerate(insts)]
    x = [xs[n][:L] + jnp.where(h0, akv[n][0], akv[n][1]) for n in range(n_inst)]
    u = [jnp.where(h0, _mm(t_inv[2 * n], x[n]), _mm(t_inv[2 * n + 1], x[n])) for n in range(n_inst)]
    z = [jnp.concatenate([u[n], inst[1]], axis=0) for n, inst in enumerate(insts)]
    out = []
    for n, inst in enumerate(insts):
        _, _, b_t, k_t, er, et = pre[n]
        y = xs[n][L:] + jnp.where(h0, _mm(r_bk[n][0], z[n]), _mm(r_bk[n][1], z[n]))
        w = jnp.concatenate([b_t * et, k_t * et], axis=0)
        s_new = inst[6] * (er * et) + jnp.where(same_head, _mm(z[n], w, TN), 0.0)
        out.append((y, s_new))
    return out


def _scan_kernel(rf_ref, vf_ref, kkf_ref, lwf_ref, bf_ref, kf_ref,
                 rb_ref, vb_ref, kkb_ref, lwb_ref, bb_ref, kb_ref,
                 yf_ref, yb_ref, s_ref):
    @pl.when(pl.program_id(1) == 0)
    def _():
        s_ref[...] = jnp.zeros_like(s_ref)

    n_pairs = s_ref.shape[1]
    insts = []
    for p in range(n_pairs):
        cs = slice(p * LANES, (p + 1) * LANES)
        insts.append((rf_ref[:, cs], vf_ref[:, cs], kkf_ref[:, cs], lwf_ref[:, cs], bf_ref[:, cs], kf_ref[:, cs],
                      s_ref[0, p], False))
        insts.append((rb_ref[:, cs], vb_ref[:, cs], kkb_ref[:, cs], lwb_ref[:, cs], bb_ref[:, cs], kb_ref[:, cs],
                      s_ref[1, p], True))
    res = _chunk_all(insts)
    for p in range(n_pairs):
        cs = slice(p * LANES, (p + 1) * LANES)
        yf_ref[:, cs], s_ref[0, p] = res[2 * p]
        yb_ref[:, cs], s_ref[1, p] = res[2 * p + 1]


def _rwkv_scan(r, v, kk, lwf, bf, kf, lwb, bb, kb, pairs_per_step=4):
    T = r.shape[0]
    nc = T // CHUNK
    width = pairs_per_step * LANES
    fwd = pl.BlockSpec((CHUNK, width), lambda p, c: (c, p))
    bwd = pl.BlockSpec((CHUNK, width), lambda p, c: (nc - 1 - c, p))
    out = jax.ShapeDtypeStruct((T, RWKV_DIM), F32)
    return pl.pallas_call(
        _scan_kernel,
        grid=(N_PAIRS // pairs_per_step, nc),
        in_specs=[fwd] * 6 + [bwd] * 6,
        out_specs=[fwd, bwd],
        out_shape=[out, out],
        scratch_shapes=[pltpu.VMEM((2, pairs_per_step, LANES, LANES), F32)],
        compiler_params=_cparams(("parallel", "arbitrary")),
        name="rwkv_scan",
    )(r, v, kk, lwf, bf, kf, r, v, kk, lwb, bb, kb)


def _post_kernel(yf_ref, yb_ref, bonus_ref, g_ref, att_ref, gr_ref, gain_ref, bias_ref, bd_ref, o_ref):
    ones_bd = bd_ref[...]
    y = yf_ref[...] + yb_ref[...]
    inv_n = 1.0 / RWKV_HEAD_DIM
    mean = _head_sum(y, ones_bd) * inv_n
    d = y - mean
    var = _head_sum(d * d, ones_bd) * inv_n
    yn = d * lax.rsqrt(var + GN_EPS) * gain_ref[...] + bias_ref[...]
    o_rwkv = (yn + bonus_ref[...]) * g_ref[...]
    o_ref[...] = (att_ref[...] + jax.nn.sigmoid(gr_ref[...]) * o_rwkv).astype(o_ref.dtype)


def _rwkv_post(yf, yb, bonus, g, att, cols, gain, bias, ones_bd, tm=1024):
    T = yf.shape[0]
    blk = pl.BlockSpec((tm, LANES), lambda i, j: (i, j))
    vec = pl.BlockSpec((1, LANES), lambda i, j: (0, j))
    return pl.pallas_call(
        _post_kernel,
        grid=(T // tm, N_PAIRS),
        in_specs=[blk, blk, blk, blk, blk,
                  pl.BlockSpec((tm, LANES), lambda i, j: (i, CB_GR + j)),
                  vec, vec, pl.BlockSpec((LANES, LANES), lambda i, j: (0, 0))],
        out_specs=blk,
        out_shape=jax.ShapeDtypeStruct((T, D_MODEL), BF16),
        compiler_params=_cparams(("parallel", "parallel")),
        name="rwkv_post",
    )(yf, yb, bonus, g, att, cols, gain, bias, ones_bd)


def _rms(x, gain):
    return x * lax.rsqrt(jnp.mean(x * x, axis=-1, keepdims=True) + NORM_EPS) * gain


def _outproj_kernel(m_ref, w_ref, x_ref, gpost_ref, gpre_ref, h_ref, hn_ref):
    mix = jnp.dot(m_ref[...], w_ref[...], preferred_element_type=F32)
    h = x_ref[...] + _rms(mix, gpost_ref[...])
    h_ref[...] = h
    hn_ref[...] = _rms(h, gpre_ref[...]).astype(BF16)


def _outproj(merged, w_out, x, g_post, g_pre, tm=256):
    T = x.shape[0]
    row = pl.BlockSpec((tm, D_MODEL), lambda i: (i, 0))
    vec = pl.BlockSpec((1, D_MODEL), lambda i: (0, 0))
    return pl.pallas_call(
        _outproj_kernel,
        grid=(T // tm,),
        in_specs=[row, pl.BlockSpec((D_MODEL, D_MODEL), lambda i: (0, 0)), row, vec, vec],
        out_specs=[row, row],
        out_shape=[jax.ShapeDtypeStruct((T, D_MODEL), F32), jax.ShapeDtypeStruct((T, D_MODEL), BF16)],
        compiler_params=_cparams(("parallel",)),
        name="outproj",
    )(merged, w_out, x, g_post, g_pre)


def _ffn_kernel(hn_ref, wu_ref, wd_ref, h_ref, g_ref, o_ref, acc_ref):
    j = pl.program_id(1)

    @pl.when(j == 0)
    def _():
        acc_ref[...] = jnp.zeros_like(acc_ref)

    up = jnp.dot(hn_ref[...], wu_ref[...], preferred_element_type=F32)
    act = jnp.square(jnp.maximum(up, 0.0)).astype(BF16)
    acc_ref[...] += jnp.dot(act, wd_ref[...], preferred_element_type=F32)

    @pl.when(j == pl.num_programs(1) - 1)
    def _():
        o_ref[...] = h_ref[...] + _rms(acc_ref[...], g_ref[...])


def _ffn(hn, w_up, w_down, h, gain, tm=512, tf=512):
    T = h.shape[0]
    row = pl.BlockSpec((tm, D_MODEL), lambda i, j: (i, 0))
    return pl.pallas_call(
        _ffn_kernel,
        grid=(T // tm, D_FF // tf),
        in_specs=[row,
                  pl.BlockSpec((D_MODEL, tf), lambda i, j: (0, j)),
                  pl.BlockSpec((tf, D_MODEL), lambda i, j: (j, 0)),
                  row,
                  pl.BlockSpec((1, D_MODEL), lambda i, j: (0, 0))],
        out_specs=row,
        out_shape=jax.ShapeDtypeStruct((T, D_MODEL), F32),
        scratch_shapes=[pltpu.VMEM((tm, D_MODEL), F32)],
        compiler_params=_cparams(("parallel", "arbitrary")),
        name="ffn",
    )(hn, w_up, w_down, h, gain)


def _pad_cols(t, n):
    return jnp.pad(t, ((0, 0), (0, n - t.shape[1])))


def _pad_rows(t, n):
    return jnp.pad(t, ((0, n - t.shape[0]), (0, 0)))


def _split_cols(t, sizes):
    idx = [int(i) for i in np.cumsum(sizes)[:-1]]
    return jnp.split(t, idx, axis=-1)


def _permute_in_cols(t):
    shift_sizes = [RWKV_DIM] * 3 + [DECAY_LORA] * 2 + [ICLR_LORA] * 2 + [GATE_LORA]
    q, ak, av, rw, ga, gr = _split_cols(t, [D_MODEL, KV_COLS, KV_COLS, sum(shift_sizes), D_MODEL, D_MODEL])
    r, k, v, wdf, wdb, adf, adb, gd = _split_cols(rw, shift_sizes)
    lora = jnp.concatenate([_pad_cols(p, LORA_PAD) for p in (wdf, wdb, adf, adb)] + [gd], axis=1)
    return jnp.concatenate([q, ga, gr, r, k, v, ak, av, _pad_cols(lora, LORA_COLS)], axis=1)


def _rope_tables(T):
    pos = jnp.arange(T, dtype=F32)
    inv_freq = ROPE_THETA ** (-jnp.arange(0, ATT_HEAD_DIM, 2, dtype=F32) / ATT_HEAD_DIM)
    ang = pos[:, None] * inv_freq[None, :]
    cos, sin = jnp.cos(ang), jnp.sin(ang)
    return jnp.concatenate([cos, cos], axis=1), jnp.concatenate([-sin, sin], axis=1)


def kernel(x, norm_pre_mix, w_in, mu_shift, attn_sink, w0_fwd, w_up_fwd, w0_bwd, w_up_bwd, a0_fwd, a_up_fwd, a0_bwd, a_up_bwd, g_up, k_k, k_a, r_k, ln_x_gain, ln_x_bias, w_out, norm_post_mix, norm_pre_ffn, w_ffn_up, w_ffn_down, norm_post_ffn):
    B, T, _ = x.shape
    depth = w_in.shape[0]
    cos2, sin2 = _rope_tables(T)
    lane = np.arange(LANES)
    ones_bd = jnp.asarray((lane[:, None] // RWKV_HEAD_DIM) == (lane[None, :] // RWKV_HEAD_DIM), BF16)
    row = lambda t: t.reshape(1, -1)
    outs = []
    for bi in range(B):
        h = x[bi]
        for l in range(depth):
            w_perm = _permute_in_cols(w_in[l]).astype(BF16)
            mu = mu_shift[l].reshape(1, -1)
            shift_sizes = [RWKV_DIM] * 3 + [DECAY_LORA] * 2 + [ICLR_LORA] * 2 + [GATE_LORA]
            mr, mk, mv, m1, m2, m3, m4, mg = _split_cols(mu, shift_sizes)
            mu_rkv = jnp.concatenate([mr, mk, mv], axis=1)
            mu_lora = _pad_cols(jnp.concatenate([_pad_cols(p, LORA_PAD) for p in (m1, m2, m3, m4)] + [mg], axis=1),
                                LORA_COLS)
            cols = _inproj(h, row(norm_pre_mix[l]), w_perm)
            att = _attention(cols, attn_sink[l], cos2, sin2)
            vecs = [row(t[l]) for t in (w0_fwd, w0_bwd, a0_fwd, a0_bwd, k_k, k_a, r_k)]
            r, v, kk, lwf, bf, kf, lwb, bb, kb, g, bonus = _rwkv_prep(
                cols, mu_rkv, mu_lora, vecs,
                *[_pad_rows(t[l], LORA_PAD).astype(BF16) for t in (w_up_fwd, w_up_bwd, a_up_fwd, a_up_bwd)],
                g_up[l].astype(BF16), ones_bd)
            yf, yb = _rwkv_scan(r, v, kk, lwf, bf, kf, lwb, bb, kb)
            merged = _rwkv_post(yf, yb, bonus, g, att, cols, row(ln_x_gain[l]), row(ln_x_bias[l]), ones_bd)
            h, hn = _outproj(merged, w_out[l].astype(BF16), h, row(norm_post_mix[l]), row(norm_pre_ffn[l]))
            h = _ffn(hn, w_ffn_up[l].astype(BF16), w_ffn_down[l].astype(BF16), h, row(norm_post_ffn[l]))
        outs.append(h)
    return jnp.stack(outs, axis=0)
```

```python
import functools

import jax
import jax.numpy as jnp
import numpy as np
from jax import lax
from jax.experimental import pallas as pl
from jax.experimental.pallas import tpu as pltpu

F32 = jnp.float32
BF16 = jnp.bfloat16
LANES = 128
SUBLANES = 8

D_MODEL = 2048
ATT_HEAD_DIM = 128
ATT_HEADS = 16
ATT_KV_HEADS = 4
ATT_GROUP = 4
WINDOW = 128
BLOCK = 128
ROPE_THETA = 10000.0
RWKV_HEAD_DIM = 64
RWKV_DIM = 2048
N_PAIRS = RWKV_DIM // LANES
DECAY_LORA = 96
ICLR_LORA = 96
GATE_LORA = 256
LORA_PAD = 128
LORA_COLS = 1024
D_FF = 4 * D_MODEL
NORM_EPS = 1e-6
GN_EPS = 64e-5
MASK_VALUE = -1e30
KV_COLS = ATT_KV_HEADS * ATT_HEAD_DIM

CB_Q, CB_GA, CB_GR, CB_R, CB_K, CB_V, CB_AK, CB_AV, CB_LORA = 0, 16, 32, 48, 64, 80, 96, 100, 104
IN_COLS_PAD = 112 * LANES

CHUNK = 64
VMEM_LIMIT = 56 * 1024 * 1024

NN = (((1,), (0,)), ((), ()))
NT = (((1,), (1,)), ((), ()))
TN = (((0,), (0,)), ((), ()))


def _mm(a, b, dims=NN):
    return lax.dot_general(a.astype(BF16), b.astype(BF16), dims, preferred_element_type=F32)


def _split_bf16(x, parts):
    out = []
    for _ in range(parts - 1):
        hi = x.astype(BF16)
        out.append(hi)
        x = x - hi.astype(F32)
    out.append(x.astype(BF16))
    return out


def _mm_exact_lhs(a, b, parts):
    a = a.astype(BF16)
    acc = None
    for term in _split_bf16(b, parts):
        p = lax.dot_general(a, term, NN, preferred_element_type=F32)
        acc = p if acc is None else acc + p
    return acc


def _mm_exact_rhs(a, b, parts):
    b = b.astype(BF16)
    acc = None
    for term in _split_bf16(a, parts):
        p = lax.dot_general(term, b, NN, preferred_element_type=F32)
        acc = p if acc is None else acc + p
    return acc


def _cparams(sem):
    return pltpu.CompilerParams(dimension_semantics=sem, vmem_limit_bytes=VMEM_LIMIT)


def _inproj_kernel(x_ref, g_ref, w_ref, o_ref, xn_ref):
    @pl.when(pl.program_id(1) == 0)
    def _():
        x = x_ref[...]
        ms = jnp.mean(x * x, axis=-1, keepdims=True)
        xn_ref[...] = (x * lax.rsqrt(ms + NORM_EPS) * g_ref[...]).astype(BF16)

    o_ref[...] = jnp.dot(xn_ref[...], w_ref[...], preferred_element_type=F32)


def _inproj(x, gain, w, tm=512, tn=2048):
    T = x.shape[0]
    n = w.shape[1]
    return pl.pallas_call(
        _inproj_kernel,
        grid=(T // tm, n // tn),
        in_specs=[
            pl.BlockSpec((tm, D_MODEL), lambda i, j: (i, 0)),
            pl.BlockSpec((1, D_MODEL), lambda i, j: (0, 0)),
            pl.BlockSpec((D_MODEL, tn), lambda i, j: (0, j)),
        ],
        out_specs=pl.BlockSpec((tm, tn), lambda i, j: (i, j)),
        out_shape=jax.ShapeDtypeStruct((T, n), F32),
        scratch_shapes=[pltpu.VMEM((tm, D_MODEL), BF16)],
        compiler_params=_cparams(("parallel", "arbitrary")),
        name="inproj",
    )(x, gain, w)


def _rope(x, c, s):
    return x * c + pltpu.roll(x, ATT_HEAD_DIM // 2, 1) * s


def _attn_kernel(sink_ref, q_ref, kp_ref, kc_ref, kn_ref, vp_ref, vc_ref, vn_ref, gate_ref,
                 cc_ref, sc_ref, cp_ref, sp_ref, cn_ref, sn_ref, o_ref, *, seq):
    i = pl.program_id(0)
    cc, sc = cc_ref[...], sc_ref[...]
    cp, sp = cp_ref[...], sp_ref[...]
    cn, sn = cn_ref[...], sn_ref[...]
    qi = lax.broadcasted_iota(jnp.int32, (BLOCK, 3 * BLOCK), 0)
    sj = lax.broadcasted_iota(jnp.int32, (BLOCK, 3 * BLOCK), 1)
    kpos = (i - 1) * BLOCK + sj
    valid = (jnp.abs(sj - BLOCK - qi) <= WINDOW) & (kpos >= 0) & (kpos < seq)
    valid4 = jnp.concatenate([valid] * ATT_GROUP, axis=0)
    rowg = lax.broadcasted_iota(jnp.int32, (ATT_GROUP * BLOCK, 1), 0) // BLOCK
    scale = ATT_HEAD_DIM ** -0.5
    for g in range(ATT_KV_HEADS):
        ks = slice(g * ATT_HEAD_DIM, (g + 1) * ATT_HEAD_DIM)
        kw = jnp.concatenate([_rope(kp_ref[:, ks], cp, sp), _rope(kc_ref[:, ks], cc, sc),
                              _rope(kn_ref[:, ks], cn, sn)], axis=0)
        vw = jnp.concatenate([vp_ref[:, ks], vc_ref[:, ks], vn_ref[:, ks]], axis=0)
        heads = [g * ATT_GROUP + hh for hh in range(ATT_GROUP)]
        q4 = jnp.concatenate(
            [_rope(q_ref[:, h * ATT_HEAD_DIM:(h + 1) * ATT_HEAD_DIM], cc, sc) for h in heads], axis=0) * scale
        s = _mm(q4, kw, NT)
        s = jnp.where(valid4, s, MASK_VALUE)
        sink = jnp.zeros((ATT_GROUP * BLOCK, 1), F32)
        for hh, h in enumerate(heads):
            sink = jnp.where(rowg == hh, sink_ref[h], sink)
        m = jnp.maximum(jnp.max(s, axis=-1, keepdims=True), sink)
        p = jnp.exp(s - m)
        den = jnp.sum(p, axis=-1, keepdims=True) + jnp.exp(sink - m)
        o = _mm(p, vw) / den
        for hh, h in enumerate(heads):
            cs = slice(h * ATT_HEAD_DIM, (h + 1) * ATT_HEAD_DIM)
            o_ref[:, cs] = o[hh * BLOCK:(hh + 1) * BLOCK] * jax.nn.sigmoid(gate_ref[:, cs])


def _attention(cols, sink, cos2, sin2):
    T = cols.shape[0]
    nb = T // BLOCK
    prev = lambda i: (jnp.maximum(i - 1, 0), 0)
    cur = lambda i: (i, 0)
    nxt = lambda i: (jnp.minimum(i + 1, nb - 1), 0)
    kcb, vcb = CB_AK * LANES // KV_COLS, CB_AV * LANES // KV_COLS
    col = lambda f, cb: (lambda i: (f(i)[0], cb))
    tab = pl.BlockSpec
    return pl.pallas_call(
        functools.partial(_attn_kernel, seq=T),
        grid=(nb,),
        in_specs=[
            pl.BlockSpec(memory_space=pltpu.SMEM),
            pl.BlockSpec((BLOCK, D_MODEL), col(cur, CB_Q // 16)),
            pl.BlockSpec((BLOCK, KV_COLS), col(prev, kcb)),
            pl.BlockSpec((BLOCK, KV_COLS), col(cur, kcb)),
            pl.BlockSpec((BLOCK, KV_COLS), col(nxt, kcb)),
            pl.BlockSpec((BLOCK, KV_COLS), col(prev, vcb)),
            pl.BlockSpec((BLOCK, KV_COLS), col(cur, vcb)),
            pl.BlockSpec((BLOCK, KV_COLS), col(nxt, vcb)),
            pl.BlockSpec((BLOCK, D_MODEL), col(cur, CB_GA // 16)),
            tab((BLOCK, LANES), cur), tab((BLOCK, LANES), cur),
            tab((BLOCK, LANES), prev), tab((BLOCK, LANES), prev),
            tab((BLOCK, LANES), nxt), tab((BLOCK, LANES), nxt),
        ],
        out_specs=pl.BlockSpec((BLOCK, D_MODEL), cur),
        out_shape=jax.ShapeDtypeStruct((T, D_MODEL), F32),
        compiler_params=_cparams(("parallel",)),
        name="attention",
    )(sink, cols, cols, cols, cols, cols, cols, cols, cols, cos2, sin2, cos2, sin2, cos2, sin2)


def _head_sum(x, ones_bd):
    return _mm_exact_rhs(x, ones_bd, 2)


def _prep_kernel(r_ref, rp_ref, rn_ref, k_ref, kp_ref, kn_ref, v_ref, vp_ref, vn_ref,
                 lo_ref, lop_ref, lon_ref, mur_ref, muk_ref, muv_ref, mul_ref,
                 w0f_ref, w0b_ref, a0f_ref, a0b_ref, kk_ref, ka_ref, rk_ref,
                 wuf_ref, wub_ref, auf_ref, aub_ref, gup_ref, bd_ref,
                 r_o, v_o, kk_o, lwf_o, bf_o, kf_o, lwb_o, bb_o, kb_o, g_o, bonus_o):
    i = pl.program_id(0)
    first = i == 0
    last = i == pl.num_programs(0) - 1
    tm = r_ref.shape[0]

    def shift(c_ref, p_ref, n_ref, mu):
        c = c_ref[...]
        row = lax.broadcasted_iota(jnp.int32, c.shape, 0)
        prow = jnp.where(first, 0.0, p_ref[SUBLANES - 1:SUBLANES, :])
        nrow = jnp.where(last, 0.0, n_ref[0:1, :])
        prev = jnp.where(row == 0, prow, pltpu.roll(c, 1, 0))
        nxt = jnp.where(row == tm - 1, nrow, pltpu.roll(c, tm - 1, 0))
        return c + mu * (0.5 * (prev + nxt) - c)

    r = shift(r_ref, rp_ref, rn_ref, mur_ref[...])
    k = shift(k_ref, kp_ref, kn_ref, muk_ref[...])
    v = shift(v_ref, vp_ref, vn_ref, muv_ref[...])
    lo = shift(lo_ref, lop_ref, lon_ref, mul_ref[...])
    ones_bd = bd_ref[...]

    kk = k * kk_ref[...]
    kk = kk / jnp.maximum(jnp.sqrt(_head_sum(kk * kk, ones_bd)), 1e-12)
    k_a = ka_ref[...]

    def direction(wd, ad, w0, wu, a0, au):
        w_log = -jax.nn.softplus(-(w0 + _mm(jnp.tanh(wd), wu))) - 0.5
        lw = -jnp.exp(w_log)
        a = jax.nn.sigmoid(a0 + _mm(ad, au))
        k_mod = k * (1.0 + (a - 1.0) * k_a)
        return lw, kk * a, k_mod

    P = LORA_PAD
    lwf, bf, kf = direction(lo[:, 0:P], lo[:, 2 * P:3 * P], w0f_ref[...], wuf_ref[...], a0f_ref[...], auf_ref[...])
    lwb, bb, kb = direction(lo[:, P:2 * P], lo[:, 3 * P:4 * P], w0b_ref[...], wub_ref[...], a0b_ref[...], aub_ref[...])
    g = _mm(jax.nn.sigmoid(lo[:, 4 * P:4 * P + GATE_LORA]), gup_ref[...])
    bonus = _head_sum(r * (0.5 * (kf + kb)) * rk_ref[...], ones_bd) * v

    r_o[...] = r
    v_o[...] = v
    kk_o[...] = kk
    lwf_o[...] = lwf
    bf_o[...] = bf
    kf_o[...] = kf
    lwb_o[...] = lwb
    bb_o[...] = bb
    kb_o[...] = kb
    g_o[...] = g
    bonus_o[...] = bonus


def _rwkv_prep(cols, mu_rkv, mu_lora, vecs, w_up_f, w_up_b, a_up_f, a_up_b, g_up, ones_bd, tm=512):
    T = cols.shape[0]
    nt = T // tm
    hb = tm // SUBLANES
    nhb = T // SUBLANES
    main = lambda cb: pl.BlockSpec((tm, LANES), lambda i, j: (i, cb + j))
    hprev = lambda cb: pl.BlockSpec((SUBLANES, LANES), lambda i, j: (jnp.maximum(i * hb - 1, 0), cb + j))
    hnext = lambda cb: pl.BlockSpec((SUBLANES, LANES), lambda i, j: (jnp.minimum((i + 1) * hb, nhb - 1), cb + j))
    lcb = CB_LORA * LANES // LORA_COLS
    vec = lambda off: pl.BlockSpec((1, LANES), lambda i, j: (0, off + j))
    up = lambda rows: pl.BlockSpec((rows, LANES), lambda i, j: (0, j))
    in_specs = []
    for cb in (CB_R, CB_K, CB_V):
        in_specs += [main(cb), hprev(cb), hnext(cb)]
    in_specs += [
        pl.BlockSpec((tm, LORA_COLS), lambda i, j: (i, lcb)),
        pl.BlockSpec((SUBLANES, LORA_COLS), lambda i, j: (jnp.maximum(i * hb - 1, 0), lcb)),
        pl.BlockSpec((SUBLANES, LORA_COLS), lambda i, j: (jnp.minimum((i + 1) * hb, nhb - 1), lcb)),
        vec(0), vec(N_PAIRS), vec(2 * N_PAIRS),
        pl.BlockSpec((1, LORA_COLS), lambda i, j: (0, 0)),
    ]
    in_specs += [vec(0)] * 7
    in_specs += [up(LORA_PAD)] * 4 + [up(GATE_LORA)]
    in_specs += [pl.BlockSpec((LANES, LANES), lambda i, j: (0, 0))]
    out = jax.ShapeDtypeStruct((T, RWKV_DIM), F32)
    return pl.pallas_call(
        _prep_kernel,
        grid=(nt, N_PAIRS),
        in_specs=in_specs,
        out_specs=[pl.BlockSpec((tm, LANES), lambda i, j: (i, j))] * 11,
        out_shape=[out] * 11,
        compiler_params=_cparams(("parallel", "arbitrary")),
        name="rwkv_prep",
    )(cols, cols, cols, cols, cols, cols, cols, cols, cols, cols, cols, cols,
      mu_rkv, mu_rkv, mu_rkv, mu_lora, *vecs, w_up_f, w_up_b, a_up_f, a_up_b, g_up, ones_bd)


def _block_diag(x, head_masks):
    return jnp.concatenate([jnp.where(m, x, 0.0) for m in head_masks], axis=0).astype(BF16)


def _scan_consts(L, W):
    assert L == RWKV_HEAD_DIM
    nh = W // RWKV_HEAD_DIM
    row1 = lax.broadcasted_iota(jnp.int32, (L, L), 0)
    col1 = lax.broadcasted_iota(jnp.int32, (L, L), 1)
    row = lax.broadcasted_iota(jnp.int32, (L, W), 0)
    col = lax.broadcasted_iota(jnp.int32, (L, W), 1) & (L - 1)
    row2 = lax.broadcasted_iota(jnp.int32, (L, 2 * W), 0)
    col2 = lax.broadcasted_iota(jnp.int32, (L, 2 * W), 1) & (L - 1)
    lane = lax.broadcasted_iota(jnp.int32, (1, W), 1)
    srow = lax.broadcasted_iota(jnp.int32, (W, W), 0)
    scol = lax.broadcasted_iota(jnp.int32, (W, W), 1)
    same = lambda sh: (row >> sh) == (col >> sh)
    levels = []
    sh = 1
    while (1 << sh) < L:
        levels.append(same(sh + 1) & jnp.logical_not(same(sh)))
        sh += 1
    return dict(
        ones={False: jnp.where(col1 <= row1, 1.0, 0.0), True: jnp.where(col1 >= row1, 1.0, 0.0)},
        strict={False: col < row, True: col > row},
        incl={False: col2 <= row2, True: col2 >= row2},
        heads=[(lane >> 6) == h for h in range(nh)],
        same_head=(srow >> 6) == (scol >> 6),
        eye=jnp.where(row == col, 1.0, 0.0),
        base=same(1),
        levels=levels,
    )


def _chunk_local(insts, c):
    L, W = insts[0][0].shape
    hm = c["heads"]
    cums = [_mm_exact_lhs(c["ones"][inst[6]], inst[3], 3) for inst in insts]
    pre = []
    for (r, v, kk, lw, b, k, rev), cum in zip(insts, cums):
        tot = cum[0:1] if rev else cum[L - 1:L]
        mid = L // 2 if rev else L // 2 - 1
        rho = cum[mid:mid + 1]
        e1 = jnp.exp(cum - rho)
        e2 = jnp.exp(rho - cum)
        er = jnp.exp(rho)
        et = jnp.exp(tot - rho)
        pre.append(dict(a_t=-kk * e1 * jnp.exp(-lw), r_t=r * e1, b_t=b * e2, k_t=k * e2, er=er, et=et, v=v, rev=rev))
    for p in pre:
        lhs = jnp.concatenate([p["a_t"], p["r_t"]], axis=0)
        rhs = jnp.concatenate([jnp.where(m, p["b_t"], 0.0) for m in hm]
                              + [jnp.where(m, p["k_t"], 0.0) for m in hm], axis=0)
        gram = _mm(lhs, rhs, NT)
        p["a_b"] = jnp.where(c["strict"][p["rev"]], gram[:L, :W], 0.0)
        p["a_k"] = jnp.where(c["strict"][p["rev"]], gram[:L, W:], 0.0)
        p["r_bk"] = jnp.where(c["incl"][p["rev"]], gram[L:], 0.0)
    ds = [c["eye"] + jnp.where(c["base"], p["a_b"], 0.0) for p in pre]
    for level in c["levels"]:
        offs = [jnp.where(level, p["a_b"], 0.0) for p in pre]
        dbd = [_block_diag(d, hm) for d in ds]
        ts = [_mm(o, db) for o, db in zip(offs, dbd)]
        tbd = [_block_diag(t, hm) for t in ts]
        ds = [d + _mm(d, tb) for d, tb in zip(ds, tbd)]
    for p, d in zip(pre, ds):
        p["t_inv"] = d
        p["v_bd"] = _block_diag(p["v"], hm)
    for p in pre:
        p["akv"] = _mm(p["a_k"], p["v_bd"])
        p["ar_abs"] = jnp.concatenate([p["a_t"] * p["er"], p["r_t"] * p["er"]], axis=0).astype(BF16)
        p["w"] = jnp.concatenate([p["b_t"] * p["et"], p["k_t"] * p["et"]], axis=0).astype(BF16)
        p["gamma"] = p["er"] * p["et"]
    return pre


def _chunk_state(pre, states, c):
    hm = c["heads"]
    L = pre[0]["v"].shape[0]
    xs = [_mm(p["ar_abs"], s, NT) for p, s in zip(pre, states)]
    x = [q[:L] + p["akv"] for q, p in zip(xs, pre)]
    u = [_mm(p["t_inv"], _block_diag(xx, hm)) for p, xx in zip(pre, x)]
    ys = [q[L:] + _mm(p["r_bk"], jnp.concatenate([_block_diag(uu, hm), p["v_bd"]], axis=0))
          for q, p, uu in zip(xs, pre, u)]
    new = [s * p["gamma"] + jnp.where(c["same_head"], _mm(jnp.concatenate([uu, p["v"]], axis=0), p["w"], TN), 0.0)
           for s, p, uu in zip(states, pre, u)]
    return ys, new


def _scan_kernel(rf_ref, vf_ref, kkf_ref, lwf_ref, bf_ref, kf_ref,
                 rb_ref, vb_ref, kkb_ref, lwb_ref, bb_ref, kb_ref,
                 yf_ref, yb_ref, s_ref, *, chunk):
    @pl.when(pl.program_id(1) == 0)
    def _():
        s_ref[...] = jnp.zeros_like(s_ref)

    n_sub = rf_ref.shape[0] // chunk
    n_grp, width = s_ref.shape[1], s_ref.shape[2]
    c = _scan_consts(chunk, width)
    fwd_refs = (rf_ref, vf_ref, kkf_ref, lwf_ref, bf_ref, kf_ref)
    bwd_refs = (rb_ref, vb_ref, kkb_ref, lwb_ref, bb_ref, kb_ref)
    cols = [slice(g * width, (g + 1) * width) for g in range(n_grp)]
    rows_f = [slice(t * chunk, (t + 1) * chunk) for t in range(n_sub)]
    rows_b = rows_f[::-1]
    insts = []
    for t in range(n_sub):
        for cs in cols:
            insts.append(tuple(ref[rows_f[t], cs] for ref in fwd_refs) + (False,))
            insts.append(tuple(ref[rows_b[t], cs] for ref in bwd_refs) + (True,))
    local = _chunk_local(insts, c)
    per_step = 2 * n_grp
    states = [s_ref[d, g] for g in range(n_grp) for d in (0, 1)]
    for t in range(n_sub):
        ys, states = _chunk_state(local[t * per_step:(t + 1) * per_step], states, c)
        for g, cs in enumerate(cols):
            yf_ref[rows_f[t], cs] = ys[2 * g]
            yb_ref[rows_b[t], cs] = ys[2 * g + 1]
    for g in range(n_grp):
        s_ref[0, g] = states[2 * g]
        s_ref[1, g] = states[2 * g + 1]


def _rwkv_scan(r, v, kk, lwf, bf, kf, lwb, bb, kb, heads_per_group=4, groups_per_step=4, chunks_per_step=2):
    T = r.shape[0]
    rows = CHUNK * chunks_per_step
    nblk = T // rows
    gw = heads_per_group * RWKV_HEAD_DIM
    width = groups_per_step * gw
    fwd = pl.BlockSpec((rows, width), lambda p, c: (c, p))
    bwd = pl.BlockSpec((rows, width), lambda p, c: (nblk - 1 - c, p))
    out = jax.ShapeDtypeStruct((T, RWKV_DIM), F32)
    return pl.pallas_call(
        functools.partial(_scan_kernel, chunk=CHUNK),
        grid=(RWKV_DIM // width, nblk),
        in_specs=[fwd] * 6 + [bwd] * 6,
        out_specs=[fwd, bwd],
        out_shape=[out, out],
        scratch_shapes=[pltpu.VMEM((2, groups_per_step, gw, gw), F32)],
        compiler_params=_cparams(("parallel", "arbitrary")),
        name="rwkv_scan",
    )(r, v, kk, lwf, bf, kf, r, v, kk, lwb, bb, kb)


def _post_kernel(yf_ref, yb_ref, bonus_ref, g_ref, att_ref, gr_ref, gain_ref, bias_ref, bd_ref, o_ref):
    ones_bd = bd_ref[...]
    y = yf_ref[...] + yb_ref[...]
    inv_n = 1.0 / RWKV_HEAD_DIM
    mean = _head_sum(y, ones_bd) * inv_n
    d = y - mean
    var = _head_sum(d * d, ones_bd) * inv_n
    yn = d * lax.rsqrt(var + GN_EPS) * gain_ref[...] + bias_ref[...]
    o_rwkv = (yn + bonus_ref[...]) * g_ref[...]
    o_ref[...] = (att_ref[...] + jax.nn.sigmoid(gr_ref[...]) * o_rwkv).astype(o_ref.dtype)


def _rwkv_post(yf, yb, bonus, g, att, cols, gain, bias, ones_bd, tm=1024):
    T = yf.shape[0]
    blk = pl.BlockSpec((tm, LANES), lambda i, j: (i, j))
    vec = pl.BlockSpec((1, LANES), lambda i, j: (0, j))
    return pl.pallas_call(
        _post_kernel,
        grid=(T // tm, N_PAIRS),
        in_specs=[blk, blk, blk, blk, blk,
                  pl.BlockSpec((tm, LANES), lambda i, j: (i, CB_GR + j)),
                  vec, vec, pl.BlockSpec((LANES, LANES), lambda i, j: (0, 0))],
        out_specs=blk,
        out_shape=jax.ShapeDtypeStruct((T, D_MODEL), BF16),
        compiler_params=_cparams(("parallel", "parallel")),
        name="rwkv_post",
    )(yf, yb, bonus, g, att, cols, gain, bias, ones_bd)


def _rms(x, gain):
    return x * lax.rsqrt(jnp.mean(x * x, axis=-1, keepdims=True) + NORM_EPS) * gain


def _outproj_kernel(m_ref, w_ref, x_ref, gpost_ref, gpre_ref, h_ref, hn_ref):
    mix = jnp.dot(m_ref[...], w_ref[...], preferred_element_type=F32)
    h = x_ref[...] + _rms(mix, gpost_ref[...])
    h_ref[...] = h
    hn_ref[...] = _rms(h, gpre_ref[...]).astype(BF16)


def _outproj(merged, w_out, x, g_post, g_pre, tm=256):
    T = x.shape[0]
    row = pl.BlockSpec((tm, D_MODEL), lambda i: (i, 0))
    vec = pl.BlockSpec((1, D_MODEL), lambda i: (0, 0))
    return pl.pallas_call(
        _outproj_kernel,
        grid=(T // tm,),
        in_specs=[row, pl.BlockSpec((D_MODEL, D_MODEL), lambda i: (0, 0)), row, vec, vec],
        out_specs=[row, row],
        out_shape=[jax.ShapeDtypeStruct((T, D_MODEL), F32), jax.ShapeDtypeStruct((T, D_MODEL), BF16)],
        compiler_params=_cparams(("parallel",)),
        name="outproj",
    )(merged, w_out, x, g_post, g_pre)


def _ffn_kernel(hn_ref, wu_ref, wd_ref, h_ref, g_ref, o_ref, acc_ref):
    j = pl.program_id(1)

    @pl.when(j == 0)
    def _():
        acc_ref[...] = jnp.zeros_like(acc_ref)

    up = jnp.dot(hn_ref[...], wu_ref[...], preferred_element_type=F32)
    act = jnp.square(jnp.maximum(up, 0.0)).astype(BF16)
    acc_ref[...] += jnp.dot(act, wd_ref[...], preferred_element_type=F32)

    @pl.when(j == pl.num_programs(1) - 1)
    def _():
        o_ref[...] = h_ref[...] + _rms(acc_ref[...], g_ref[...])


def _ffn(hn, w_up, w_down, h, gain, tm=512, tf=512):
    T = h.shape[0]
    row = pl.BlockSpec((tm, D_MODEL), lambda i, j: (i, 0))
    return pl.pallas_call(
        _ffn_kernel,
        grid=(T // tm, D_FF // tf),
        in_specs=[row,
                  pl.BlockSpec((D_MODEL, tf), lambda i, j: (0, j)),
                  pl.BlockSpec((tf, D_MODEL), lambda i, j: (j, 0)),
                  row,
                  pl.BlockSpec((1, D_MODEL), lambda i, j: (0, 0))],
        out_specs=row,
        out_shape=jax.ShapeDtypeStruct((T, D_MODEL), F32),
        scratch_shapes=[pltpu.VMEM((tm, D_MODEL), F32)],
        compiler_params=_cparams(("parallel", "arbitrary")),
        name="ffn",
    )(hn, w_up, w_down, h, gain)


def _pad_cols(t, n):
    return jnp.pad(t, ((0, 0), (0, n - t.shape[1])))


def _pad_rows(t, n):
    return jnp.pad(t, ((0, n - t.shape[0]), (0, 0)))


def _split_cols(t, sizes):
    idx = [int(i) for i in np.cumsum(sizes)[:-1]]
    return jnp.split(t, idx, axis=-1)


def _permute_in_cols(t):
    shift_sizes = [RWKV_DIM] * 3 + [DECAY_LORA] * 2 + [ICLR_LORA] * 2 + [GATE_LORA]
    q, ak, av, rw, ga, gr = _split_cols(t, [D_MODEL, KV_COLS, KV_COLS, sum(shift_sizes), D_MODEL, D_MODEL])
    r, k, v, wdf, wdb, adf, adb, gd = _split_cols(rw, shift_sizes)
    lora = jnp.concatenate([_pad_cols(p, LORA_PAD) for p in (wdf, wdb, adf, adb)] + [gd], axis=1)
    return jnp.concatenate([q, ga, gr, r, k, v, ak, av, _pad_cols(lora, LORA_COLS)], axis=1)


def _rope_tables(T):
    pos = jnp.arange(T, dtype=F32)
    inv_freq = ROPE_THETA ** (-jnp.arange(0, ATT_HEAD_DIM, 2, dtype=F32) / ATT_HEAD_DIM)
    ang = pos[:, None] * inv_freq[None, :]
    cos, sin = jnp.cos(ang), jnp.sin(ang)
    return jnp.concatenate([cos, cos], axis=1), jnp.concatenate([-sin, sin], axis=1)


def kernel(x, norm_pre_mix, w_in, mu_shift, attn_sink, w0_fwd, w_up_fwd, w0_bwd, w_up_bwd, a0_fwd, a_up_fwd, a0_bwd, a_up_bwd, g_up, k_k, k_a, r_k, ln_x_gain, ln_x_bias, w_out, norm_post_mix, norm_pre_ffn, w_ffn_up, w_ffn_down, norm_post_ffn):
    B, T, _ = x.shape
    depth = w_in.shape[0]
    cos2, sin2 = _rope_tables(T)
    lane = np.arange(LANES)
    ones_bd = jnp.asarray((lane[:, None] // RWKV_HEAD_DIM) == (lane[None, :] // RWKV_HEAD_DIM), BF16)
    row = lambda t: t.reshape(1, -1)
    outs = []
    for bi in range(B):
        h = x[bi]
        for l in range(depth):
            w_perm = _permute_in_cols(w_in[l]).astype(BF16)
            mu = mu_shift[l].reshape(1, -1)
            shift_sizes = [RWKV_DIM] * 3 + [DECAY_LORA] * 2 + [ICLR_LORA] * 2 + [GATE_LORA]
            mr, mk, mv, m1, m2, m3, m4, mg = _split_cols(mu, shift_sizes)
            mu_rkv = jnp.concatenate([mr, mk, mv], axis=1)
            mu_lora = _pad_cols(jnp.concatenate([_pad_cols(p, LORA_PAD) for p in (m1, m2, m3, m4)] + [mg], axis=1),
                                LORA_COLS)
            cols = _inproj(h, row(norm_pre_mix[l]), w_perm)
            att = _attention(cols, attn_sink[l], cos2, sin2)
            vecs = [row(t[l]) for t in (w0_fwd, w0_bwd, a0_fwd, a0_bwd, k_k, k_a, r_k)]
            r, v, kk, lwf, bf, kf, lwb, bb, kb, g, bonus = _rwkv_prep(
                cols, mu_rkv, mu_lora, vecs,
                *[_pad_rows(t[l], LORA_PAD).astype(BF16) for t in (w_up_fwd, w_up_bwd, a_up_fwd, a_up_bwd)],
                g_up[l].astype(BF16), ones_bd)
            yf, yb = _rwkv_scan(r, v, kk, lwf, bf, kf, lwb, bb, kb)
            merged = _rwkv_post(yf, yb, bonus, g, att, cols, row(ln_x_gain[l]), row(ln_x_bias[l]), ones_bd)
            h, hn = _outproj(merged, w_out[l].astype(BF16), h, row(norm_post_mix[l]), row(norm_pre_ffn[l]))
            h = _ffn(hn, w_ffn_up[l].astype(BF16), w_ffn_down[l].astype(BF16), h, row(norm_post_ffn[l]))
        outs.append(h)
    return jnp.stack(outs, axis=0)
```

```python
import functools

import jax
import jax.numpy as jnp
import numpy as np
from jax import lax
from jax.experimental import pallas as pl
from jax.experimental.pallas import tpu as pltpu

F32 = jnp.float32
BF16 = jnp.bfloat16
LANES = 128
SUBLANES = 8

D_MODEL = 2048
ATT_HEAD_DIM = 128
ATT_HEADS = 16
ATT_KV_HEADS = 4
ATT_GROUP = 4
WINDOW = 128
BLOCK = 128
ROPE_THETA = 10000.0
RWKV_HEAD_DIM = 64
RWKV_DIM = 2048
N_PAIRS = RWKV_DIM // LANES
DECAY_LORA = 96
ICLR_LORA = 96
GATE_LORA = 256
LORA_PAD = 128
LORA_COLS = 1024
D_FF = 4 * D_MODEL
NORM_EPS = 1e-6
GN_EPS = 64e-5
MASK_VALUE = -1e30
KV_COLS = ATT_KV_HEADS * ATT_HEAD_DIM

CB_Q, CB_GA, CB_GR, CB_R, CB_K, CB_V, CB_AK, CB_AV, CB_LORA = 0, 16, 32, 48, 64, 80, 96, 100, 104
IN_COLS_PAD = 112 * LANES

CHUNK = 128
VMEM_LIMIT = 56 * 1024 * 1024

NN = (((1,), (0,)), ((), ()))
NT = (((1,), (1,)), ((), ()))
TN = (((0,), (0,)), ((), ()))


def _mm(a, b, dims=NN):
    return lax.dot_general(a.astype(BF16), b.astype(BF16), dims, preferred_element_type=F32)


def _split_bf16(x, parts):
    out = []
    for _ in range(parts - 1):
        hi = x.astype(BF16)
        out.append(hi)
        x = x - hi.astype(F32)
    out.append(x.astype(BF16))
    return out


def _mm_exact_lhs(a, b, parts):
    a = a.astype(BF16)
    acc = None
    for term in _split_bf16(b, parts):
        p = lax.dot_general(a, term, NN, preferred_element_type=F32)
        acc = p if acc is None else acc + p
    return acc


def _cparams(sem):
    return pltpu.CompilerParams(dimension_semantics=sem, vmem_limit_bytes=VMEM_LIMIT)


def _inproj_kernel(x_ref, g_ref, w_ref, o_ref, xn_ref):
    @pl.when(pl.program_id(1) == 0)
    def _():
        x = x_ref[...]
        ms = jnp.mean(x * x, axis=-1, keepdims=True)
        xn_ref[...] = (x * lax.rsqrt(ms + NORM_EPS) * g_ref[...]).astype(BF16)

    o_ref[...] = jnp.dot(xn_ref[...], w_ref[...], preferred_element_type=F32)


def _inproj(x, gain, w, tm=512, tn=2048):
    T = x.shape[0]
    n = w.shape[1]
    return pl.pallas_call(
        _inproj_kernel,
        grid=(T // tm, n // tn),
        in_specs=[
            pl.BlockSpec((tm, D_MODEL), lambda i, j: (i, 0)),
            pl.BlockSpec((1, D_MODEL), lambda i, j: (0, 0)),
            pl.BlockSpec((D_MODEL, tn), lambda i, j: (0, j)),
        ],
        out_specs=pl.BlockSpec((tm, tn), lambda i, j: (i, j)),
        out_shape=jax.ShapeDtypeStruct((T, n), F32),
        scratch_shapes=[pltpu.VMEM((tm, D_MODEL), BF16)],
        compiler_params=_cparams(("parallel", "arbitrary")),
        name="inproj",
    )(x, gain, w)


def _rope(x, c, s):
    return x * c + pltpu.roll(x, ATT_HEAD_DIM // 2, 1) * s


def _attn_kernel(sink_ref, q_ref, kp_ref, kc_ref, kn_ref, vp_ref, vc_ref, vn_ref, gate_ref,
                 cc_ref, sc_ref, cp_ref, sp_ref, cn_ref, sn_ref, o_ref, *, seq):
    i = pl.program_id(0)
    cc, sc = cc_ref[...], sc_ref[...]
    cp, sp = cp_ref[...], sp_ref[...]
    cn, sn = cn_ref[...], sn_ref[...]
    qi = lax.broadcasted_iota(jnp.int32, (BLOCK, 3 * BLOCK), 0)
    sj = lax.broadcasted_iota(jnp.int32, (BLOCK, 3 * BLOCK), 1)
    kpos = (i - 1) * BLOCK + sj
    valid = (jnp.abs(sj - BLOCK - qi) <= WINDOW) & (kpos >= 0) & (kpos < seq)
    valid4 = jnp.concatenate([valid] * ATT_GROUP, axis=0)
    rowg = lax.broadcasted_iota(jnp.int32, (ATT_GROUP * BLOCK, 1), 0) // BLOCK
    scale = ATT_HEAD_DIM ** -0.5
    for g in range(ATT_KV_HEADS):
        ks = slice(g * ATT_HEAD_DIM, (g + 1) * ATT_HEAD_DIM)
        kw = jnp.concatenate([_rope(kp_ref[:, ks], cp, sp), _rope(kc_ref[:, ks], cc, sc),
                              _rope(kn_ref[:, ks], cn, sn)], axis=0)
        vw = jnp.concatenate([vp_ref[:, ks], vc_ref[:, ks], vn_ref[:, ks]], axis=0)
        heads = [g * ATT_GROUP + hh for hh in range(ATT_GROUP)]
        q4 = jnp.concatenate(
            [_rope(q_ref[:, h * ATT_HEAD_DIM:(h + 1) * ATT_HEAD_DIM], cc, sc) for h in heads], axis=0) * scale
        s = _mm(q4, kw, NT)
        s = jnp.where(valid4, s, MASK_VALUE)
        sink = jnp.zeros((ATT_GROUP * BLOCK, 1), F32)
        for hh, h in enumerate(heads):
            sink = jnp.where(rowg == hh, sink_ref[h], sink)
        m = jnp.maximum(jnp.max(s, axis=-1, keepdims=True), sink)
        p = jnp.exp(s - m)
        den = jnp.sum(p, axis=-1, keepdims=True) + jnp.exp(sink - m)
        o = _mm(p, vw) / den
        for hh, h in enumerate(heads):
            cs = slice(h * ATT_HEAD_DIM, (h + 1) * ATT_HEAD_DIM)
            o_ref[:, cs] = (o[hh * BLOCK:(hh + 1) * BLOCK] * jax.nn.sigmoid(gate_ref[:, cs])).astype(o_ref.dtype)


def _attention(cols, sink, cos2, sin2):
    T = cols.shape[0]
    nb = T // BLOCK
    prev = lambda i: (jnp.maximum(i - 1, 0), 0)
    cur = lambda i: (i, 0)
    nxt = lambda i: (jnp.minimum(i + 1, nb - 1), 0)
    kcb, vcb = CB_AK * LANES // KV_COLS, CB_AV * LANES // KV_COLS
    col = lambda f, cb: (lambda i: (f(i)[0], cb))
    tab = pl.BlockSpec
    return pl.pallas_call(
        functools.partial(_attn_kernel, seq=T),
        grid=(nb,),
        in_specs=[
            pl.BlockSpec(memory_space=pltpu.SMEM),
            pl.BlockSpec((BLOCK, D_MODEL), col(cur, CB_Q // 16)),
            pl.BlockSpec((BLOCK, KV_COLS), col(prev, kcb)),
            pl.BlockSpec((BLOCK, KV_COLS), col(cur, kcb)),
            pl.BlockSpec((BLOCK, KV_COLS), col(nxt, kcb)),
            pl.BlockSpec((BLOCK, KV_COLS), col(prev, vcb)),
            pl.BlockSpec((BLOCK, KV_COLS), col(cur, vcb)),
            pl.BlockSpec((BLOCK, KV_COLS), col(nxt, vcb)),
            pl.BlockSpec((BLOCK, D_MODEL), col(cur, CB_GA // 16)),
            tab((BLOCK, LANES), cur), tab((BLOCK, LANES), cur),
            tab((BLOCK, LANES), prev), tab((BLOCK, LANES), prev),
            tab((BLOCK, LANES), nxt), tab((BLOCK, LANES), nxt),
        ],
        out_specs=pl.BlockSpec((BLOCK, D_MODEL), cur),
        out_shape=jax.ShapeDtypeStruct((T, D_MODEL), BF16),
        compiler_params=_cparams(("parallel",)),
        name="attention",
    )(sink, cols, cols, cols, cols, cols, cols, cols, cols, cos2, sin2, cos2, sin2, cos2, sin2)


def _head_sum(x, ones_bd):
    return _mm(x, ones_bd)


def _prep_kernel(r_ref, rp_ref, rn_ref, k_ref, kp_ref, kn_ref, v_ref, vp_ref, vn_ref,
                 lo_ref, lop_ref, lon_ref, mur_ref, muk_ref, muv_ref, mul_ref,
                 w0f_ref, w0b_ref, a0f_ref, a0b_ref, kk_ref, ka_ref, rk_ref,
                 wuf_ref, wub_ref, auf_ref, aub_ref, gup_ref, bd_ref,
                 r_o, v_o, kk_o, lwf_o, bf_o, kf_o, lwb_o, bb_o, kb_o, g_o, bonus_o, act_ref):
    i = pl.program_id(0)
    first = i == 0
    last = i == pl.num_programs(0) - 1
    tm = r_ref.shape[0]

    def shift(c_ref, p_ref, n_ref, mu):
        c = c_ref[...]
        row = lax.broadcasted_iota(jnp.int32, c.shape, 0)
        prow = jnp.where(first, 0.0, p_ref[SUBLANES - 1:SUBLANES, :])
        nrow = jnp.where(last, 0.0, n_ref[0:1, :])
        prev = jnp.where(row == 0, prow, pltpu.roll(c, 1, 0))
        nxt = jnp.where(row == tm - 1, nrow, pltpu.roll(c, tm - 1, 0))
        return c + mu * (0.5 * (prev + nxt) - c)

    r = shift(r_ref, rp_ref, rn_ref, mur_ref[...])
    k = shift(k_ref, kp_ref, kn_ref, muk_ref[...])
    v = shift(v_ref, vp_ref, vn_ref, muv_ref[...])
    P = LORA_PAD

    @pl.when(pl.program_id(1) == 0)
    def _():
        lo = shift(lo_ref, lop_ref, lon_ref, mul_ref[...])
        act_ref[:, 0:2 * P] = jnp.tanh(lo[:, 0:2 * P]).astype(BF16)
        act_ref[:, 2 * P:4 * P] = lo[:, 2 * P:4 * P].astype(BF16)
        act_ref[:, 4 * P:] = jax.nn.sigmoid(lo[:, 4 * P:4 * P + GATE_LORA]).astype(BF16)

    ones_bd = bd_ref[...]

    kk = k * kk_ref[...]
    kk = kk / jnp.maximum(jnp.sqrt(_head_sum(kk * kk, ones_bd)), 1e-12)
    k_a = ka_ref[...]

    def direction(wd, ad, w0, wu, a0, au):
        w_log = -jax.nn.softplus(-(w0 + _mm(wd, wu))) - 0.5
        lw = -jnp.exp(w_log)
        a = jax.nn.sigmoid(a0 + _mm(ad, au))
        k_mod = k * (1.0 + (a - 1.0) * k_a)
        return lw, kk * a, k_mod

    lwf, bf, kf = direction(act_ref[:, 0:P], act_ref[:, 2 * P:3 * P],
                            w0f_ref[...], wuf_ref[...], a0f_ref[...], auf_ref[...])
    lwb, bb, kb = direction(act_ref[:, P:2 * P], act_ref[:, 3 * P:4 * P],
                            w0b_ref[...], wub_ref[...], a0b_ref[...], aub_ref[...])
    g = _mm(act_ref[:, 4 * P:], gup_ref[...])
    bonus = _head_sum(r * (0.5 * (kf + kb)) * rk_ref[...], ones_bd) * v

    for ref, val in ((r_o, r), (v_o, v), (kk_o, kk), (lwf_o, lwf), (bf_o, bf), (kf_o, kf),
                     (lwb_o, lwb), (bb_o, bb), (kb_o, kb), (g_o, g), (bonus_o, bonus)):
        ref[...] = val.astype(ref.dtype)


def _rwkv_prep(cols, mu_rkv, mu_lora, vecs, w_up_f, w_up_b, a_up_f, a_up_b, g_up, ones_bd, tm=512):
    T = cols.shape[0]
    nt = T // tm
    hb = tm // SUBLANES
    nhb = T // SUBLANES
    main = lambda cb: pl.BlockSpec((tm, LANES), lambda i, j: (i, cb + j))
    hprev = lambda cb: pl.BlockSpec((SUBLANES, LANES), lambda i, j: (jnp.maximum(i * hb - 1, 0), cb + j))
    hnext = lambda cb: pl.BlockSpec((SUBLANES, LANES), lambda i, j: (jnp.minimum((i + 1) * hb, nhb - 1), cb + j))
    lcb = CB_LORA * LANES // LORA_COLS
    vec = lambda off: pl.BlockSpec((1, LANES), lambda i, j: (0, off + j))
    up = lambda rows: pl.BlockSpec((rows, LANES), lambda i, j: (0, j))
    in_specs = []
    for cb in (CB_R, CB_K, CB_V):
        in_specs += [main(cb), hprev(cb), hnext(cb)]
    in_specs += [
        pl.BlockSpec((tm, LORA_COLS), lambda i, j: (i, lcb)),
        pl.BlockSpec((SUBLANES, LORA_COLS), lambda i, j: (jnp.maximum(i * hb - 1, 0), lcb)),
        pl.BlockSpec((SUBLANES, LORA_COLS), lambda i, j: (jnp.minimum((i + 1) * hb, nhb - 1), lcb)),
        vec(0), vec(N_PAIRS), vec(2 * N_PAIRS),
        pl.BlockSpec((1, LORA_COLS), lambda i, j: (0, 0)),
    ]
    in_specs += [vec(0)] * 7
    in_specs += [up(LORA_PAD)] * 4 + [up(GATE_LORA)]
    in_specs += [pl.BlockSpec((LANES, LANES), lambda i, j: (0, 0))]
    dtypes = [BF16, BF16, BF16, F32, BF16, BF16, F32, BF16, BF16, BF16, BF16]
    return pl.pallas_call(
        _prep_kernel,
        grid=(nt, N_PAIRS),
        in_specs=in_specs,
        out_specs=[pl.BlockSpec((tm, LANES), lambda i, j: (i, j))] * 11,
        out_shape=[jax.ShapeDtypeStruct((T, RWKV_DIM), dt) for dt in dtypes],
        scratch_shapes=[pltpu.VMEM((tm, 4 * LORA_PAD + GATE_LORA), BF16)],
        compiler_params=_cparams(("parallel", "arbitrary")),
        name="rwkv_prep",
    )(cols, cols, cols, cols, cols, cols, cols, cols, cols, cols, cols, cols,
      mu_rkv, mu_rkv, mu_rkv, mu_lora, *vecs, w_up_f, w_up_b, a_up_f, a_up_b, g_up, ones_bd)


def _tri_inverse_all(mats, upper):
    L = mats[0].shape[0]
    row = lax.broadcasted_iota(jnp.int32, (L, L), 0)
    col = lax.broadcasted_iota(jnp.int32, (L, L), 1)
    hrow = lax.broadcasted_iota(jnp.int32, (L // 2, L), 0)
    hcol = lax.broadcasted_iota(jnp.int32, (L // 2, L), 1)
    same = lambda sh: (row >> sh) == (col >> sh)
    eye = jnp.where(row == col, 1.0, 0.0)
    ds = [eye + jnp.where(same(1), a, 0.0) for a in mats]
    sh = 1
    while (1 << sh) < L:
        m = 1 << sh
        dbs = [d.astype(BF16) for d in ds]
        if m < SUBLANES:
            level = same(sh + 1) & jnp.logical_not(same(sh))
            ts = [_mm(jnp.where(level, a, 0.0), db) for a, db in zip(mats, dbs)]
            ds = [d + _mm(db, t) for d, db, t in zip(ds, dbs, ts)]
        else:
            n_half = L // (2 * m)
            part = [slice(q * m, (q + 1) * m) for q in range(n_half)]
            blocks = lambda x: [x[q * m:(q + 1) * m] for q in range(2 * n_half)]
            zero = jnp.zeros((m, L), F32)
            act = lambda x, up: jnp.concatenate(blocks(x)[(0 if up else 1)::2], axis=0)
            level = {False: (hcol >> sh) == 2 * (hrow >> sh), True: (hcol >> sh) == 2 * (hrow >> sh) + 1}

            def spread(x, up, rest=None):
                out = []
                for q in range(n_half):
                    other = zero if rest is None else rest[2 * q + (1 if up else 0)]
                    out += [x[part[q]], other] if up else [other, x[part[q]]]
                return jnp.concatenate(out, axis=0)

            ts = [_mm(jnp.where(level[up], act(a, up), 0.0), db) for a, db, up in zip(mats, dbs, upper)]
            upd = [_mm(act(d, up), spread(t, up)) for d, t, up in zip(ds, ts, upper)]
            ds = [spread(act(d, up) + u, up, rest=blocks(d)) for d, u, up in zip(ds, upd, upper)]
        sh += 1
    return ds


def _chunk_all(insts):
    L = insts[0][0].shape[0]
    n_inst = len(insts)
    row = lax.broadcasted_iota(jnp.int32, (L, L), 0)
    col = lax.broadcasted_iota(jnp.int32, (L, L), 1)
    incl = {False: col <= row, True: col >= row}
    strict = {False: col < row, True: col > row}
    ones = {rev: jnp.where(incl[rev], 1.0, 0.0) for rev in (False, True)}
    incl2 = {rev: jnp.concatenate([incl[rev]] * 2, axis=1) for rev in (False, True)}
    strict2 = {rev: jnp.concatenate([strict[rev]] * 2, axis=1) for rev in (False, True)}
    lane = lax.broadcasted_iota(jnp.int32, (1, LANES), 1)
    h0 = lane < RWKV_HEAD_DIM
    srow = lax.broadcasted_iota(jnp.int32, (LANES, LANES), 0)
    scol = lax.broadcasted_iota(jnp.int32, (LANES, LANES), 1)
    same_head = (srow >= RWKV_HEAD_DIM) == (scol >= RWKV_HEAD_DIM)

    cums = [_mm_exact_lhs(ones[inst[7]], inst[3], 2) for inst in insts]
    pre = []
    for (r, v, kk, lw, b, k, s, rev), cum in zip(insts, cums):
        tot = cum[0:1] if rev else cum[L - 1:L]
        mid = L // 2 if rev else L // 2 - 1
        rho = cum[mid:mid + 1]
        e1 = jnp.exp(cum - rho)
        e2 = jnp.exp(rho - cum)
        er = jnp.exp(rho)
        et = jnp.exp(tot - rho)
        a_t = -kk * e1 * jnp.exp(-lw)
        r_t = r * e1
        b_t = b * e2
        k_t = k * e2
        pre.append((a_t, r_t, b_t, k_t, er, et))
    grams = []
    for a_t, r_t, b_t, k_t, _, _ in pre:
        zero = jnp.zeros_like(a_t)
        lhs = jnp.concatenate([jnp.where(h0, a_t, zero), jnp.where(h0, zero, a_t),
                               jnp.where(h0, r_t, zero), jnp.where(h0, zero, r_t)], axis=0)
        rhs = jnp.concatenate([b_t, k_t], axis=0)
        grams.append(_mm(lhs, rhs, NT))
    a_bk, r_bk = [], []
    for inst, gram in zip(insts, grams):
        rev = inst[7]
        a_bk.append([jnp.where(strict2[rev], gram[hh * L:(hh + 1) * L], 0.0) for hh in range(2)])
        r_bk.append([jnp.where(incl2[rev], gram[(2 + hh) * L:(3 + hh) * L], 0.0) for hh in range(2)])
    t_inv = _tri_inverse_all([a_bk[n][hh][:, :L] for n in range(n_inst) for hh in range(2)],
                             [insts[n][7] for n in range(n_inst) for hh in range(2)])
    xs = [_mm(jnp.concatenate([a_t * er, r_t * er], axis=0), inst[6], NT)
          for inst, (a_t, r_t, _, _, er, _) in zip(insts, pre)]
    pick = lambda t2: jnp.where(h0, t2[:L], t2[L:])
    akv = [_mm(jnp.concatenate([a_bk[n][0][:, L:], a_bk[n][1][:, L:]], axis=0), inst[1])
           for n, inst in enumerate(insts)]
    x = [xs[n][:L] + pick(akv[n]) for n in range(n_inst)]
    u = [pick(_mm(jnp.concatenate([t_inv[2 * n], t_inv[2 * n + 1]], axis=0), x[n])) for n in range(n_inst)]
    z = [jnp.concatenate([u[n], inst[1]], axis=0) for n, inst in enumerate(insts)]
    out = []
    for n, inst in enumerate(insts):
        _, _, b_t, k_t, er, et = pre[n]
        y = xs[n][L:] + pick(_mm(jnp.concatenate([r_bk[n][0], r_bk[n][1]], axis=0), z[n]))
        w = jnp.concatenate([b_t * et, k_t * et], axis=0)
        s_new = inst[6] * (er * et) + jnp.where(same_head, _mm(z[n], w, TN), 0.0)
        out.append((y, s_new))
    return out


def _scan_kernel(rf_ref, vf_ref, kkf_ref, lwf_ref, bf_ref, kf_ref,
                 rb_ref, vb_ref, kkb_ref, lwb_ref, bb_ref, kb_ref,
                 yf_ref, yb_ref, s_ref):
    @pl.when(pl.program_id(1) == 0)
    def _():
        s_ref[...] = jnp.zeros_like(s_ref)

    n_pairs = s_ref.shape[1]
    insts = []
    for p in range(n_pairs):
        cs = slice(p * LANES, (p + 1) * LANES)
        insts.append(tuple(ref[:, cs].astype(F32) for ref in (rf_ref, vf_ref, kkf_ref, lwf_ref, bf_ref, kf_ref))
                     + (s_ref[0, p], False))
        insts.append(tuple(ref[:, cs].astype(F32) for ref in (rb_ref, vb_ref, kkb_ref, lwb_ref, bb_ref, kb_ref))
                     + (s_ref[1, p], True))
    res = _chunk_all(insts)
    for p in range(n_pairs):
        cs = slice(p * LANES, (p + 1) * LANES)
        yf_ref[:, cs], s_ref[0, p] = res[2 * p]
        yb_ref[:, cs], s_ref[1, p] = res[2 * p + 1]


def _rwkv_scan(r, v, kk, lwf, bf, kf, lwb, bb, kb, pairs_per_step=8):
    T = r.shape[0]
    nc = T // CHUNK
    width = pairs_per_step * LANES
    fwd = pl.BlockSpec((CHUNK, width), lambda p, c: (c, p))
    bwd = pl.BlockSpec((CHUNK, width), lambda p, c: (nc - 1 - c, p))
    out = jax.ShapeDtypeStruct((T, RWKV_DIM), F32)
    return pl.pallas_call(
        _scan_kernel,
        grid=(N_PAIRS // pairs_per_step, nc),
        in_specs=[fwd] * 6 + [bwd] * 6,
        out_specs=[fwd, bwd],
        out_shape=[out, out],
        scratch_shapes=[pltpu.VMEM((2, pairs_per_step, LANES, LANES), F32)],
        compiler_params=_cparams(("parallel", "arbitrary")),
        name="rwkv_scan",
    )(r, v, kk, lwf, bf, kf, r, v, kk, lwb, bb, kb)


def _post_kernel(yf_ref, yb_ref, bonus_ref, g_ref, att_ref, gr_ref, gain_ref, bias_ref, bd_ref, o_ref):
    ones_bd = bd_ref[...]
    y = yf_ref[...] + yb_ref[...]
    inv_n = 1.0 / RWKV_HEAD_DIM
    mean = _head_sum(y, ones_bd) * inv_n
    d = y - mean
    var = _head_sum(d * d, ones_bd) * inv_n
    yn = d * lax.rsqrt(var + GN_EPS) * gain_ref[...] + bias_ref[...]
    o_rwkv = (yn + bonus_ref[...].astype(F32)) * g_ref[...].astype(F32)
    o_ref[...] = (att_ref[...].astype(F32) + jax.nn.sigmoid(gr_ref[...]) * o_rwkv).astype(o_ref.dtype)


def _rwkv_post(yf, yb, bonus, g, att, cols, gain, bias, ones_bd, tm=1024):
    T = yf.shape[0]
    blk = pl.BlockSpec((tm, LANES), lambda i, j: (i, j))
    vec = pl.BlockSpec((1, LANES), lambda i, j: (0, j))
    return pl.pallas_call(
        _post_kernel,
        grid=(T // tm, N_PAIRS),
        in_specs=[blk, blk, blk, blk, blk,
                  pl.BlockSpec((tm, LANES), lambda i, j: (i, CB_GR + j)),
                  vec, vec, pl.BlockSpec((LANES, LANES), lambda i, j: (0, 0))],
        out_specs=blk,
        out_shape=jax.ShapeDtypeStruct((T, D_MODEL), BF16),
        compiler_params=_cparams(("parallel", "parallel")),
        name="rwkv_post",
    )(yf, yb, bonus, g, att, cols, gain, bias, ones_bd)


def _rms(x, gain):
    return x * lax.rsqrt(jnp.mean(x * x, axis=-1, keepdims=True) + NORM_EPS) * gain


def _outproj_kernel(m_ref, w_ref, x_ref, gpost_ref, gpre_ref, h_ref, hn_ref):
    mix = jnp.dot(m_ref[...], w_ref[...], preferred_element_type=F32)
    h = x_ref[...] + _rms(mix, gpost_ref[...])
    h_ref[...] = h
    hn_ref[...] = _rms(h, gpre_ref[...]).astype(BF16)


def _outproj(merged, w_out, x, g_post, g_pre, tm=256):
    T = x.shape[0]
    row = pl.BlockSpec((tm, D_MODEL), lambda i: (i, 0))
    vec = pl.BlockSpec((1, D_MODEL), lambda i: (0, 0))
    return pl.pallas_call(
        _outproj_kernel,
        grid=(T // tm,),
        in_specs=[row, pl.BlockSpec((D_MODEL, D_MODEL), lambda i: (0, 0)), row, vec, vec],
        out_specs=[row, row],
        out_shape=[jax.ShapeDtypeStruct((T, D_MODEL), F32), jax.ShapeDtypeStruct((T, D_MODEL), BF16)],
        compiler_params=_cparams(("parallel",)),
        name="outproj",
    )(merged, w_out, x, g_post, g_pre)


def _ffn_kernel(hn_ref, wu_ref, wd_ref, h_ref, g_ref, o_ref, acc_ref):
    j = pl.program_id(1)

    @pl.when(j == 0)
    def _():
        acc_ref[...] = jnp.zeros_like(acc_ref)

    up = jnp.dot(hn_ref[...], wu_ref[...], preferred_element_type=F32)
    act = jnp.square(jnp.maximum(up, 0.0)).astype(BF16)
    acc_ref[...] += jnp.dot(act, wd_ref[...], preferred_element_type=F32)

    @pl.when(j == pl.num_programs(1) - 1)
    def _():
        o_ref[...] = h_ref[...] + _rms(acc_ref[...], g_ref[...])


def _ffn(hn, w_up, w_down, h, gain, tm=512, tf=512):
    T = h.shape[0]
    row = pl.BlockSpec((tm, D_MODEL), lambda i, j: (i, 0))
    return pl.pallas_call(
        _ffn_kernel,
        grid=(T // tm, D_FF // tf),
        in_specs=[row,
                  pl.BlockSpec((D_MODEL, tf), lambda i, j: (0, j)),
                  pl.BlockSpec((tf, D_MODEL), lambda i, j: (j, 0)),
                  row,
                  pl.BlockSpec((1, D_MODEL), lambda i, j: (0, 0))],
        out_specs=row,
        out_shape=jax.ShapeDtypeStruct((T, D_MODEL), F32),
        scratch_shapes=[pltpu.VMEM((tm, D_MODEL), F32)],
        compiler_params=_cparams(("parallel", "arbitrary")),
        name="ffn",
    )(hn, w_up, w_down, h, gain)


def _pad_cols(t, n):
    return jnp.pad(t, ((0, 0), (0, n - t.shape[1])))


def _pad_rows(t, n):
    return jnp.pad(t, ((0, n - t.shape[0]), (0, 0)))


def _split_cols(t, sizes):
    idx = [int(i) for i in np.cumsum(sizes)[:-1]]
    return jnp.split(t, idx, axis=-1)


def _permute_in_cols(t):
    shift_sizes = [RWKV_DIM] * 3 + [DECAY_LORA] * 2 + [ICLR_LORA] * 2 + [GATE_LORA]
    q, ak, av, rw, ga, gr = _split_cols(t, [D_MODEL, KV_COLS, KV_COLS, sum(shift_sizes), D_MODEL, D_MODEL])
    r, k, v, wdf, wdb, adf, adb, gd = _split_cols(rw, shift_sizes)
    lora = jnp.concatenate([_pad_cols(p, LORA_PAD) for p in (wdf, wdb, adf, adb)] + [gd], axis=1)
    return jnp.concatenate([q, ga, gr, r, k, v, ak, av, _pad_cols(lora, LORA_COLS)], axis=1)


def _rope_tables(T):
    pos = jnp.arange(T, dtype=F32)
    inv_freq = ROPE_THETA ** (-jnp.arange(0, ATT_HEAD_DIM, 2, dtype=F32) / ATT_HEAD_DIM)
    ang = pos[:, None] * inv_freq[None, :]
    cos, sin = jnp.cos(ang), jnp.sin(ang)
    return jnp.concatenate([cos, cos], axis=1), jnp.concatenate([-sin, sin], axis=1)


def kernel(x, norm_pre_mix, w_in, mu_shift, attn_sink, w0_fwd, w_up_fwd, w0_bwd, w_up_bwd, a0_fwd, a_up_fwd, a0_bwd, a_up_bwd, g_up, k_k, k_a, r_k, ln_x_gain, ln_x_bias, w_out, norm_post_mix, norm_pre_ffn, w_ffn_up, w_ffn_down, norm_post_ffn):
    B, T, _ = x.shape
    depth = w_in.shape[0]
    cos2, sin2 = _rope_tables(T)
    lane = np.arange(LANES)
    ones_bd = jnp.asarray((lane[:, None] // RWKV_HEAD_DIM) == (lane[None, :] // RWKV_HEAD_DIM), BF16)
    row = lambda t: t.reshape(1, -1)
    outs = []
    for bi in range(B):
        h = x[bi]
        for l in range(depth):
            w_perm = _permute_in_cols(w_in[l]).astype(BF16)
            mu = mu_shift[l].reshape(1, -1)
            shift_sizes = [RWKV_DIM] * 3 + [DECAY_LORA] * 2 + [ICLR_LORA] * 2 + [GATE_LORA]
            mr, mk, mv, m1, m2, m3, m4, mg = _split_cols(mu, shift_sizes)
            mu_rkv = jnp.concatenate([mr, mk, mv], axis=1)
            mu_lora = _pad_cols(jnp.concatenate([_pad_cols(p, LORA_PAD) for p in (m1, m2, m3, m4)] + [mg], axis=1),
                                LORA_COLS)
            cols = _inproj(h, row(norm_pre_mix[l]), w_perm)
            att = _attention(cols, attn_sink[l], cos2, sin2)
            vecs = [row(t[l]) for t in (w0_fwd, w0_bwd, a0_fwd, a0_bwd, k_k, k_a, r_k)]
            r, v, kk, lwf, bf, kf, lwb, bb, kb, g, bonus = _rwkv_prep(
                cols, mu_rkv, mu_lora, vecs,
                *[_pad_rows(t[l], LORA_PAD).astype(BF16) for t in (w_up_fwd, w_up_bwd, a_up_fwd, a_up_bwd)],
                g_up[l].astype(BF16), ones_bd)
            yf, yb = _rwkv_scan(r, v, kk, lwf, bf, kf, lwb, bb, kb)
            merged = _rwkv_post(yf, yb, bonus, g, att, cols, row(ln_x_gain[l]), row(ln_x_bias[l]), ones_bd)
            h, hn = _outproj(merged, w_out[l].astype(BF16), h, row(norm_post_mix[l]), row(norm_pre_ffn[l]))
            h = _ffn(hn, w_ffn_up[l].astype(BF16), w_ffn_down[l].astype(BF16), h, row(norm_post_ffn[l]))
        outs.append(h)
    return jnp.stack(outs, axis=0)
```

```python
import functools

import jax
import jax.numpy as jnp
import numpy as np
from jax import lax
from jax.experimental import pallas as pl
from jax.experimental.pallas import tpu as pltpu

F32 = jnp.float32
BF16 = jnp.bfloat16
LANES = 128
SUBLANES = 8
PACKED_SUBLANES = 16

D_MODEL = 2048
ATT_HEAD_DIM = 128
ATT_HEADS = 16
ATT_KV_HEADS = 4
ATT_GROUP = 4
WINDOW = 128
BLOCK = 128
ROPE_THETA = 10000.0
RWKV_HEAD_DIM = 64
RWKV_DIM = 2048
N_PAIRS = RWKV_DIM // LANES
DECAY_LORA = 96
ICLR_LORA = 96
GATE_LORA = 256
LORA_PAD = 128
LORA_COLS = 1024
D_FF = 4 * D_MODEL
NORM_EPS = 1e-6
GN_EPS = 64e-5
MASK_VALUE = -1e30
KV_COLS = ATT_KV_HEADS * ATT_HEAD_DIM

CB_Q, CB_GA, CB_GR, CB_R, CB_K, CB_V, CB_AK, CB_AV, CB_LORA = 0, 16, 32, 48, 64, 80, 96, 100, 104
IN_COLS_PAD = 112 * LANES

CHUNK = 128
VMEM_LIMIT = 56 * 1024 * 1024

NN = (((1,), (0,)), ((), ()))
NT = (((1,), (1,)), ((), ()))
TN = (((0,), (0,)), ((), ()))


def _mm(a, b, dims=NN):
    return lax.dot_general(a.astype(BF16), b.astype(BF16), dims, preferred_element_type=F32)


def _split_bf16(x, parts):
    out = []
    for _ in range(parts - 1):
        hi = x.astype(BF16)
        out.append(hi)
        x = x - hi.astype(F32)
    out.append(x.astype(BF16))
    return out


def _mm_exact_lhs(a, b, parts):
    a = a.astype(BF16)
    acc = None
    for term in _split_bf16(b, parts):
        p = lax.dot_general(a, term, NN, preferred_element_type=F32)
        acc = p if acc is None else acc + p
    return acc


def _cparams(sem):
    return pltpu.CompilerParams(dimension_semantics=sem, vmem_limit_bytes=VMEM_LIMIT)


def _inproj_kernel(x_ref, g_ref, w_ref, o_ref, xn_ref):
    @pl.when(pl.program_id(1) == 0)
    def _():
        x = x_ref[...]
        ms = jnp.mean(x * x, axis=-1, keepdims=True)
        xn_ref[...] = (x * lax.rsqrt(ms + NORM_EPS) * g_ref[...]).astype(BF16)

    o_ref[...] = jnp.dot(xn_ref[...], w_ref[...], preferred_element_type=F32).astype(o_ref.dtype)


def _inproj(x, gain, w, tm=1024, tn=1024):
    T = x.shape[0]
    n = w.shape[1]
    return pl.pallas_call(
        _inproj_kernel,
        grid=(T // tm, n // tn),
        in_specs=[
            pl.BlockSpec((tm, D_MODEL), lambda i, j: (i, 0)),
            pl.BlockSpec((1, D_MODEL), lambda i, j: (0, 0)),
            pl.BlockSpec((D_MODEL, tn), lambda i, j: (0, j)),
        ],
        out_specs=pl.BlockSpec((tm, tn), lambda i, j: (i, j)),
        out_shape=jax.ShapeDtypeStruct((T, n), BF16),
        scratch_shapes=[pltpu.VMEM((tm, D_MODEL), BF16)],
        compiler_params=_cparams(("parallel", "arbitrary")),
        name="inproj",
    )(x, gain, w)


def _rope(x, c, s):
    return x * c + pltpu.roll(x, ATT_HEAD_DIM // 2, 1) * s


def _attn_kernel(sink_ref, q_ref, kp_ref, kc_ref, kn_ref, vp_ref, vc_ref, vn_ref, gate_ref,
                 cc_ref, sc_ref, cp_ref, sp_ref, cn_ref, sn_ref, o_ref, *, seq):
    i = pl.program_id(0)
    cc, sc = cc_ref[...], sc_ref[...]
    cp, sp = cp_ref[...], sp_ref[...]
    cn, sn = cn_ref[...], sn_ref[...]
    qi = lax.broadcasted_iota(jnp.int32, (BLOCK, 3 * BLOCK), 0)
    sj = lax.broadcasted_iota(jnp.int32, (BLOCK, 3 * BLOCK), 1)
    kpos = (i - 1) * BLOCK + sj
    valid = (jnp.abs(sj - BLOCK - qi) <= WINDOW) & (kpos >= 0) & (kpos < seq)
    valid4 = jnp.concatenate([valid] * ATT_GROUP, axis=0)
    rowg = lax.broadcasted_iota(jnp.int32, (ATT_GROUP * BLOCK, 1), 0) // BLOCK
    scale = ATT_HEAD_DIM ** -0.5
    for g in range(ATT_KV_HEADS):
        ks = slice(g * ATT_HEAD_DIM, (g + 1) * ATT_HEAD_DIM)
        kw = jnp.concatenate([_rope(kp_ref[:, ks].astype(F32), cp, sp), _rope(kc_ref[:, ks].astype(F32), cc, sc),
                              _rope(kn_ref[:, ks].astype(F32), cn, sn)], axis=0)
        vw = jnp.concatenate([vp_ref[:, ks], vc_ref[:, ks], vn_ref[:, ks]], axis=0)
        heads = [g * ATT_GROUP + hh for hh in range(ATT_GROUP)]
        q4 = jnp.concatenate(
            [_rope(q_ref[:, h * ATT_HEAD_DIM:(h + 1) * ATT_HEAD_DIM].astype(F32), cc, sc) for h in heads],
            axis=0) * scale
        s = _mm(q4, kw, NT)
        s = jnp.where(valid4, s, MASK_VALUE)
        sink = jnp.zeros((ATT_GROUP * BLOCK, 1), F32)
        for hh, h in enumerate(heads):
            sink = jnp.where(rowg == hh, sink_ref[h], sink)
        m = jnp.maximum(jnp.max(s, axis=-1, keepdims=True), sink)
        p = jnp.exp(s - m)
        den = jnp.sum(p, axis=-1, keepdims=True) + jnp.exp(sink - m)
        o = _mm(p, vw) / den
        for hh, h in enumerate(heads):
            cs = slice(h * ATT_HEAD_DIM, (h + 1) * ATT_HEAD_DIM)
            o_ref[:, cs] = (o[hh * BLOCK:(hh + 1) * BLOCK] * jax.nn.sigmoid(gate_ref[:, cs].astype(F32))).astype(o_ref.dtype)


def _attention(cols, sink, cos2, sin2):
    T = cols.shape[0]
    nb = T // BLOCK
    prev = lambda i: (jnp.maximum(i - 1, 0), 0)
    cur = lambda i: (i, 0)
    nxt = lambda i: (jnp.minimum(i + 1, nb - 1), 0)
    kcb, vcb = CB_AK * LANES // KV_COLS, CB_AV * LANES // KV_COLS
    col = lambda f, cb: (lambda i: (f(i)[0], cb))
    tab = pl.BlockSpec
    return pl.pallas_call(
        functools.partial(_attn_kernel, seq=T),
        grid=(nb,),
        in_specs=[
            pl.BlockSpec(memory_space=pltpu.SMEM),
            pl.BlockSpec((BLOCK, D_MODEL), col(cur, CB_Q // 16)),
            pl.BlockSpec((BLOCK, KV_COLS), col(prev, kcb)),
            pl.BlockSpec((BLOCK, KV_COLS), col(cur, kcb)),
            pl.BlockSpec((BLOCK, KV_COLS), col(nxt, kcb)),
            pl.BlockSpec((BLOCK, KV_COLS), col(prev, vcb)),
            pl.BlockSpec((BLOCK, KV_COLS), col(cur, vcb)),
            pl.BlockSpec((BLOCK, KV_COLS), col(nxt, vcb)),
            pl.BlockSpec((BLOCK, D_MODEL), col(cur, CB_GA // 16)),
            tab((BLOCK, LANES), cur), tab((BLOCK, LANES), cur),
            tab((BLOCK, LANES), prev), tab((BLOCK, LANES), prev),
            tab((BLOCK, LANES), nxt), tab((BLOCK, LANES), nxt),
        ],
        out_specs=pl.BlockSpec((BLOCK, D_MODEL), cur),
        out_shape=jax.ShapeDtypeStruct((T, D_MODEL), BF16),
        compiler_params=_cparams(("parallel",)),
        name="attention",
    )(sink, cols, cols, cols, cols, cols, cols, cols, cols, cos2, sin2, cos2, sin2, cos2, sin2)


def _head_sum(x, ones_bd):
    return _mm(x, ones_bd)


def _prep_kernel(r_ref, rp_ref, rn_ref, k_ref, kp_ref, kn_ref, v_ref, vp_ref, vn_ref,
                 lo_ref, lop_ref, lon_ref, mur_ref, muk_ref, muv_ref, mul_ref,
                 w0f_ref, w0b_ref, a0f_ref, a0b_ref, kk_ref, ka_ref, rk_ref,
                 wuf_ref, wub_ref, auf_ref, aub_ref, gup_ref, bd_ref,
                 r_o, v_o, kk_o, lwf_o, bf_o, kf_o, lwb_o, bb_o, kb_o, g_o, bonus_o, act_ref):
    i = pl.program_id(0)
    first = i == 0
    last = i == pl.num_programs(0) - 1
    tm = r_ref.shape[0]

    def shift(c_ref, p_ref, n_ref, mu):
        c = c_ref[...].astype(F32)
        row = lax.broadcasted_iota(jnp.int32, c.shape, 0)
        prow = jnp.where(first, 0.0, p_ref[PACKED_SUBLANES - 1:PACKED_SUBLANES, :].astype(F32))
        nrow = jnp.where(last, 0.0, n_ref[0:1, :].astype(F32))
        prev = jnp.where(row == 0, prow, pltpu.roll(c, 1, 0))
        nxt = jnp.where(row == tm - 1, nrow, pltpu.roll(c, tm - 1, 0))
        return c + mu * (0.5 * (prev + nxt) - c)

    r = shift(r_ref, rp_ref, rn_ref, mur_ref[...])
    k = shift(k_ref, kp_ref, kn_ref, muk_ref[...])
    v = shift(v_ref, vp_ref, vn_ref, muv_ref[...])
    P = LORA_PAD

    @pl.when(pl.program_id(1) == 0)
    def _():
        lo = shift(lo_ref, lop_ref, lon_ref, mul_ref[...])
        act_ref[:, 0:2 * P] = jnp.tanh(lo[:, 0:2 * P]).astype(BF16)
        act_ref[:, 2 * P:4 * P] = lo[:, 2 * P:4 * P].astype(BF16)
        act_ref[:, 4 * P:] = jax.nn.sigmoid(lo[:, 4 * P:4 * P + GATE_LORA]).astype(BF16)

    ones_bd = bd_ref[...]

    kk = k * kk_ref[...]
    kk = kk / jnp.maximum(jnp.sqrt(_head_sum(kk * kk, ones_bd)), 1e-12)
    k_a = ka_ref[...]

    def direction(wd, ad, w0, wu, a0, au):
        w_log = -jax.nn.softplus(-(w0 + _mm(wd, wu))) - 0.5
        lw = -jnp.exp(w_log)
        a = jax.nn.sigmoid(a0 + _mm(ad, au))
        k_mod = k * (1.0 + (a - 1.0) * k_a)
        return lw, kk * a, k_mod

    lwf, bf, kf = direction(act_ref[:, 0:P], act_ref[:, 2 * P:3 * P],
                            w0f_ref[...], wuf_ref[...], a0f_ref[...], auf_ref[...])
    lwb, bb, kb = direction(act_ref[:, P:2 * P], act_ref[:, 3 * P:4 * P],
                            w0b_ref[...], wub_ref[...], a0b_ref[...], aub_ref[...])
    g = _mm(act_ref[:, 4 * P:], gup_ref[...])
    bonus = _head_sum(r * (0.5 * (kf + kb)) * rk_ref[...], ones_bd) * v

    for ref, val in ((r_o, r), (v_o, v), (kk_o, kk), (lwf_o, lwf), (bf_o, bf), (kf_o, kf),
                     (lwb_o, lwb), (bb_o, bb), (kb_o, kb), (g_o, g), (bonus_o, bonus)):
        ref[...] = val.astype(ref.dtype)


def _rwkv_prep(cols, mu_rkv, mu_lora, vecs, w_up_f, w_up_b, a_up_f, a_up_b, g_up, ones_bd, tm=512):
    T = cols.shape[0]
    nt = T // tm
    halo = PACKED_SUBLANES
    hb = tm // halo
    nhb = T // halo
    main = lambda cb: pl.BlockSpec((tm, LANES), lambda i, j: (i, cb + j))
    hprev = lambda cb: pl.BlockSpec((halo, LANES), lambda i, j: (jnp.maximum(i * hb - 1, 0), cb + j))
    hnext = lambda cb: pl.BlockSpec((halo, LANES), lambda i, j: (jnp.minimum((i + 1) * hb, nhb - 1), cb + j))
    lcb = CB_LORA * LANES // LORA_COLS
    vec = lambda off: pl.BlockSpec((1, LANES), lambda i, j: (0, off + j))
    up = lambda rows: pl.BlockSpec((rows, LANES), lambda i, j: (0, j))
    in_specs = []
    for cb in (CB_R, CB_K, CB_V):
        in_specs += [main(cb), hprev(cb), hnext(cb)]
    in_specs += [
        pl.BlockSpec((tm, LORA_COLS), lambda i, j: (i, lcb)),
        pl.BlockSpec((halo, LORA_COLS), lambda i, j: (jnp.maximum(i * hb - 1, 0), lcb)),
        pl.BlockSpec((halo, LORA_COLS), lambda i, j: (jnp.minimum((i + 1) * hb, nhb - 1), lcb)),
        vec(0), vec(N_PAIRS), vec(2 * N_PAIRS),
        pl.BlockSpec((1, LORA_COLS), lambda i, j: (0, 0)),
    ]
    in_specs += [vec(0)] * 7
    in_specs += [up(LORA_PAD)] * 4 + [up(GATE_LORA)]
    in_specs += [pl.BlockSpec((LANES, LANES), lambda i, j: (0, 0))]
    dtypes = [BF16, BF16, BF16, F32, BF16, BF16, F32, BF16, BF16, BF16, BF16]
    return pl.pallas_call(
        _prep_kernel,
        grid=(nt, N_PAIRS),
        in_specs=in_specs,
        out_specs=[pl.BlockSpec((tm, LANES), lambda i, j: (i, j))] * 11,
        out_shape=[jax.ShapeDtypeStruct((T, RWKV_DIM), dt) for dt in dtypes],
        scratch_shapes=[pltpu.VMEM((tm, 4 * LORA_PAD + GATE_LORA), BF16)],
        compiler_params=_cparams(("parallel", "arbitrary")),
        name="rwkv_prep",
    )(cols, cols, cols, cols, cols, cols, cols, cols, cols, cols, cols, cols,
      mu_rkv, mu_rkv, mu_rkv, mu_lora, *vecs, w_up_f, w_up_b, a_up_f, a_up_b, g_up, ones_bd)


def _tri_inverse_all(mats, upper):
    L = mats[0].shape[0]
    row = lax.broadcasted_iota(jnp.int32, (L, L), 0)
    col = lax.broadcasted_iota(jnp.int32, (L, L), 1)
    hrow = lax.broadcasted_iota(jnp.int32, (L // 2, L), 0)
    hcol = lax.broadcasted_iota(jnp.int32, (L // 2, L), 1)
    same = lambda sh: (row >> sh) == (col >> sh)
    eye = jnp.where(row == col, 1.0, 0.0)
    ds = [eye + jnp.where(same(1), a, 0.0) for a in mats]
    sh = 1
    while (1 << sh) < L:
        m = 1 << sh
        dbs = [d.astype(BF16) for d in ds]
        if m < SUBLANES:
            level = same(sh + 1) & jnp.logical_not(same(sh))
            ts = [_mm(jnp.where(level, a, 0.0), db) for a, db in zip(mats, dbs)]
            ds = [d + _mm(db, t) for d, db, t in zip(ds, dbs, ts)]
        else:
            n_half = L // (2 * m)
            part = [slice(q * m, (q + 1) * m) for q in range(n_half)]
            blocks = lambda x: [x[q * m:(q + 1) * m] for q in range(2 * n_half)]
            zero = jnp.zeros((m, L), F32)
            act = lambda x, up: jnp.concatenate(blocks(x)[(0 if up else 1)::2], axis=0)
            level = {False: (hcol >> sh) == 2 * (hrow >> sh), True: (hcol >> sh) == 2 * (hrow >> sh) + 1}

            def spread(x, up, rest=None):
                out = []
                for q in range(n_half):
                    other = zero if rest is None else rest[2 * q + (1 if up else 0)]
                    out += [x[part[q]], other] if up else [other, x[part[q]]]
                return jnp.concatenate(out, axis=0)

            ts = [_mm(jnp.where(level[up], act(a, up), 0.0), db) for a, db, up in zip(mats, dbs, upper)]
            upd = [_mm(act(d, up), spread(t, up)) for d, t, up in zip(ds, ts, upper)]
            ds = [spread(act(d, up) + u, up, rest=blocks(d)) for d, u, up in zip(ds, upd, upper)]
        sh += 1
    return ds


def _chunk_all(insts):
    L = insts[0][0].shape[0]
    n_inst = len(insts)
    row = lax.broadcasted_iota(jnp.int32, (L, L), 0)
    col = lax.broadcasted_iota(jnp.int32, (L, L), 1)
    incl = {False: col <= row, True: col >= row}
    strict = {False: col < row, True: col > row}
    ones = {rev: jnp.where(incl[rev], 1.0, 0.0) for rev in (False, True)}
    incl2 = {rev: jnp.concatenate([incl[rev]] * 2, axis=1) for rev in (False, True)}
    strict2 = {rev: jnp.concatenate([strict[rev]] * 2, axis=1) for rev in (False, True)}
    lane = lax.broadcasted_iota(jnp.int32, (1, LANES), 1)
    h0 = lane < RWKV_HEAD_DIM
    srow = lax.broadcasted_iota(jnp.int32, (LANES, LANES), 0)
    scol = lax.broadcasted_iota(jnp.int32, (LANES, LANES), 1)
    same_head = (srow >= RWKV_HEAD_DIM) == (scol >= RWKV_HEAD_DIM)

    cums = [_mm_exact_lhs(ones[inst[7]], inst[3], 2) for inst in insts]
    pre = []
    for (r, v, kk, lw, b, k, s, rev), cum in zip(insts, cums):
        tot = cum[0:1] if rev else cum[L - 1:L]
        mid = L // 2 if rev else L // 2 - 1
        rho = cum[mid:mid + 1]
        e1 = jnp.exp(cum - rho)
        e2 = jnp.exp(rho - cum)
        er = jnp.exp(rho)
        et = jnp.exp(tot - rho)
        a_t = -kk * e1 * jnp.exp(-lw)
        r_t = r * e1
        b_t = b * e2
        k_t = k * e2
        pre.append((a_t, r_t, b_t, k_t, er, et))
    grams = []
    for a_t, r_t, b_t, k_t, _, _ in pre:
        zero = jnp.zeros_like(a_t)
        lhs = jnp.concatenate([jnp.where(h0, a_t, zero), jnp.where(h0, zero, a_t),
                               jnp.where(h0, r_t, zero), jnp.where(h0, zero, r_t)], axis=0)
        rhs = jnp.concatenate([b_t, k_t], axis=0)
        grams.append(_mm(lhs, rhs, NT))
    a_bk, r_bk = [], []
    for inst, gram in zip(insts, grams):
        rev = inst[7]
        a_bk.append([jnp.where(strict2[rev], gram[hh * L:(hh + 1) * L], 0.0) for hh in range(2)])
        r_bk.append([jnp.where(incl2[rev], gram[(2 + hh) * L:(3 + hh) * L], 0.0) for hh in range(2)])
    t_inv = _tri_inverse_all([a_bk[n][hh][:, :L] for n in range(n_inst) for hh in range(2)],
                             [insts[n][7] for n in range(n_inst) for hh in range(2)])
    xs = [_mm(jnp.concatenate([a_t * er, r_t * er], axis=0), inst[6], NT)
          for inst, (a_t, r_t, _, _, er, _) in zip(insts, pre)]
    pick = lambda t2: jnp.where(h0, t2[:L], t2[L:])
    akv = [_mm(jnp.concatenate([a_bk[n][0][:, L:], a_bk[n][1][:, L:]], axis=0), inst[1])
           for n, inst in enumerate(insts)]
    x = [xs[n][:L] + pick(akv[n]) for n in range(n_inst)]
    u = [pick(_mm(jnp.concatenate([t_inv[2 * n], t_inv[2 * n + 1]], axis=0), x[n])) for n in range(n_inst)]
    z = [jnp.concatenate([u[n], inst[1]], axis=0) for n, inst in enumerate(insts)]
    out = []
    for n, inst in enumerate(insts):
        _, _, b_t, k_t, er, et = pre[n]
        y = xs[n][L:] + pick(_mm(jnp.concatenate([r_bk[n][0], r_bk[n][1]], axis=0), z[n]))
        w = jnp.concatenate([b_t * et, k_t * et], axis=0)
        s_new = inst[6] * (er * et) + jnp.where(same_head, _mm(z[n], w, TN), 0.0)
        out.append((y, s_new))
    return out


def _scan_kernel(rf_ref, vf_ref, kkf_ref, lwf_ref, bf_ref, kf_ref,
                 rb_ref, vb_ref, kkb_ref, lwb_ref, bb_ref, kb_ref,
                 yf_ref, yb_ref, s_ref):
    @pl.when(pl.program_id(1) == 0)
    def _():
        s_ref[...] = jnp.zeros_like(s_ref)

    n_pairs = s_ref.shape[1]
    insts = []
    for p in range(n_pairs):
        cs = slice(p * LANES, (p + 1) * LANES)
        insts.append(tuple(ref[:, cs].astype(F32) for ref in (rf_ref, vf_ref, kkf_ref, lwf_ref, bf_ref, kf_ref))
                     + (s_ref[0, p], False))
        insts.append(tuple(ref[:, cs].astype(F32) for ref in (rb_ref, vb_ref, kkb_ref, lwb_ref, bb_ref, kb_ref))
                     + (s_ref[1, p], True))
    res = _chunk_all(insts)
    for p in range(n_pairs):
        cs = slice(p * LANES, (p + 1) * LANES)
        yf_ref[:, cs], s_ref[0, p] = res[2 * p]
        yb_ref[:, cs], s_ref[1, p] = res[2 * p + 1]


def _rwkv_scan(r, v, kk, lwf, bf, kf, lwb, bb, kb, pairs_per_step=8):
    T = r.shape[0]
    nc = T // CHUNK
    width = pairs_per_step * LANES
    fwd = pl.BlockSpec((CHUNK, width), lambda p, c: (c, p))
    bwd = pl.BlockSpec((CHUNK, width), lambda p, c: (nc - 1 - c, p))
    out = jax.ShapeDtypeStruct((T, RWKV_DIM), F32)
    return pl.pallas_call(
        _scan_kernel,
        grid=(N_PAIRS // pairs_per_step, nc),
        in_specs=[fwd] * 6 + [bwd] * 6,
        out_specs=[fwd, bwd],
        out_shape=[out, out],
        scratch_shapes=[pltpu.VMEM((2, pairs_per_step, LANES, LANES), F32)],
        compiler_params=_cparams(("parallel", "arbitrary")),
        name="rwkv_scan",
    )(r, v, kk, lwf, bf, kf, r, v, kk, lwb, bb, kb)


def _rms(x, gain):
    return x * lax.rsqrt(jnp.mean(x * x, axis=-1, keepdims=True) + NORM_EPS) * gain


def _outproj_kernel(yf_ref, yb_ref, bonus_ref, g_ref, att_ref, gr_ref, gain_ref, bias_ref, bd_ref,
                    w_ref, x_ref, gpost_ref, gpre_ref, h_ref, hn_ref, m_ref):
    ones_bd = bd_ref[...]
    inv_n = 1.0 / RWKV_HEAD_DIM
    for j in range(N_PAIRS):
        cs = slice(j * LANES, (j + 1) * LANES)
        y = yf_ref[:, cs] + yb_ref[:, cs]
        mean = _head_sum(y, ones_bd) * inv_n
        d = y - mean
        var = _head_sum(d * d, ones_bd) * inv_n
        yn = d * lax.rsqrt(var + GN_EPS) * gain_ref[:, cs] + bias_ref[:, cs]
        o_rwkv = (yn + bonus_ref[:, cs].astype(F32)) * g_ref[:, cs].astype(F32)
        merged = att_ref[:, cs].astype(F32) + jax.nn.sigmoid(gr_ref[:, cs].astype(F32)) * o_rwkv
        m_ref[:, cs] = merged.astype(BF16)
    mix = jnp.dot(m_ref[...], w_ref[...], preferred_element_type=F32)
    h = x_ref[...] + _rms(mix, gpost_ref[...])
    h_ref[...] = h
    hn_ref[...] = _rms(h, gpre_ref[...]).astype(BF16)


def _outproj(yf, yb, bonus, g, att, cols, gain, bias, ones_bd, w_out, x, g_post, g_pre, tm=256):
    T = x.shape[0]
    row = pl.BlockSpec((tm, D_MODEL), lambda i: (i, 0))
    vec = pl.BlockSpec((1, D_MODEL), lambda i: (0, 0))
    return pl.pallas_call(
        _outproj_kernel,
        grid=(T // tm,),
        in_specs=[row, row, row, row, row,
                  pl.BlockSpec((tm, D_MODEL), lambda i: (i, CB_GR // 16)),
                  vec, vec, pl.BlockSpec((LANES, LANES), lambda i: (0, 0)),
                  pl.BlockSpec((D_MODEL, D_MODEL), lambda i: (0, 0), pipeline_mode=pl.Buffered(1)),
                  row, vec, vec],
        out_specs=[row, row],
        out_shape=[jax.ShapeDtypeStruct((T, D_MODEL), F32), jax.ShapeDtypeStruct((T, D_MODEL), BF16)],
        scratch_shapes=[pltpu.VMEM((tm, D_MODEL), BF16)],
        compiler_params=_cparams(("parallel",)),
        name="outproj",
    )(yf, yb, bonus, g, att, cols, gain, bias, ones_bd, w_out, x, g_post, g_pre)


def _ffn_kernel(hn_ref, wu_ref, wd_ref, h_ref, g_ref, o_ref, acc_ref):
    j = pl.program_id(1)

    @pl.when(j == 0)
    def _():
        acc_ref[...] = jnp.zeros_like(acc_ref)

    up = jnp.dot(hn_ref[...], wu_ref[...], preferred_element_type=F32)
    act = jnp.square(jnp.maximum(up, 0.0)).astype(BF16)
    acc_ref[...] += jnp.dot(act, wd_ref[...], preferred_element_type=F32)

    @pl.when(j == pl.num_programs(1) - 1)
    def _():
        o_ref[...] = h_ref[...] + _rms(acc_ref[...], g_ref[...])


def _ffn(hn, w_up, w_down, h, gain, tm=1024, tf=512):
    T = h.shape[0]
    row = pl.BlockSpec((tm, D_MODEL), lambda i, j: (i, 0))
    row_once = pl.BlockSpec((tm, D_MODEL), lambda i, j: (i, 0), pipeline_mode=pl.Buffered(1))
    return pl.pallas_call(
        _ffn_kernel,
        grid=(T // tm, D_FF // tf),
        in_specs=[row,
                  pl.BlockSpec((D_MODEL, tf), lambda i, j: (0, j)),
                  pl.BlockSpec((tf, D_MODEL), lambda i, j: (j, 0)),
                  row_once,
                  pl.BlockSpec((1, D_MODEL), lambda i, j: (0, 0))],
        out_specs=row_once,
        out_shape=jax.ShapeDtypeStruct((T, D_MODEL), F32),
        scratch_shapes=[pltpu.VMEM((tm, D_MODEL), F32)],
        compiler_params=_cparams(("parallel", "arbitrary")),
        name="ffn",
    )(hn, w_up, w_down, h, gain)


def _pad_cols(t, n):
    return jnp.pad(t, ((0, 0), (0, n - t.shape[1])))


def _pad_rows(t, n):
    return jnp.pad(t, ((0, n - t.shape[0]), (0, 0)))


def _split_cols(t, sizes):
    idx = [int(i) for i in np.cumsum(sizes)[:-1]]
    return jnp.split(t, idx, axis=-1)


def _permute_in_cols(t):
    shift_sizes = [RWKV_DIM] * 3 + [DECAY_LORA] * 2 + [ICLR_LORA] * 2 + [GATE_LORA]
    q, ak, av, rw, ga, gr = _split_cols(t, [D_MODEL, KV_COLS, KV_COLS, sum(shift_sizes), D_MODEL, D_MODEL])
    r, k, v, wdf, wdb, adf, adb, gd = _split_cols(rw, shift_sizes)
    lora = jnp.concatenate([_pad_cols(p, LORA_PAD) for p in (wdf, wdb, adf, adb)] + [gd], axis=1)
    return jnp.concatenate([q, ga, gr, r, k, v, ak, av, _pad_cols(lora, LORA_COLS)], axis=1)


def _rope_tables(T):
    pos = jnp.arange(T, dtype=F32)
    inv_freq = ROPE_THETA ** (-jnp.arange(0, ATT_HEAD_DIM, 2, dtype=F32) / ATT_HEAD_DIM)
    ang = pos[:, None] * inv_freq[None, :]
    cos, sin = jnp.cos(ang), jnp.sin(ang)
    return jnp.concatenate([cos, cos], axis=1), jnp.concatenate([-sin, sin], axis=1)


def kernel(x, norm_pre_mix, w_in, mu_shift, attn_sink, w0_fwd, w_up_fwd, w0_bwd, w_up_bwd, a0_fwd, a_up_fwd, a0_bwd, a_up_bwd, g_up, k_k, k_a, r_k, ln_x_gain, ln_x_bias, w_out, norm_post_mix, norm_pre_ffn, w_ffn_up, w_ffn_down, norm_post_ffn):
    B, T, _ = x.shape
    depth = w_in.shape[0]
    cos2, sin2 = _rope_tables(T)
    lane = np.arange(LANES)
    ones_bd = jnp.asarray((lane[:, None] // RWKV_HEAD_DIM) == (lane[None, :] // RWKV_HEAD_DIM), BF16)
    row = lambda t: t.reshape(1, -1)
    outs = []
    for bi in range(B):
        h = x[bi]
        for l in range(depth):
            w_perm = _permute_in_cols(w_in[l]).astype(BF16)
            mu = mu_shift[l].reshape(1, -1)
            shift_sizes = [RWKV_DIM] * 3 + [DECAY_LORA] * 2 + [ICLR_LORA] * 2 + [GATE_LORA]
            mr, mk, mv, m1, m2, m3, m4, mg = _split_cols(mu, shift_sizes)
            mu_rkv = jnp.concatenate([mr, mk, mv], axis=1)
            mu_lora = _pad_cols(jnp.concatenate([_pad_cols(p, LORA_PAD) for p in (m1, m2, m3, m4)] + [mg], axis=1),
                                LORA_COLS)
            cols = _inproj(h, row(norm_pre_mix[l]), w_perm)
            att = _attention(cols, attn_sink[l], cos2, sin2)
            vecs = [row(t[l]) for t in (w0_fwd, w0_bwd, a0_fwd, a0_bwd, k_k, k_a, r_k)]
            r, v, kk, lwf, bf, kf, lwb, bb, kb, g, bonus = _rwkv_prep(
                cols, mu_rkv, mu_lora, vecs,
                *[_pad_rows(t[l], LORA_PAD).astype(BF16) for t in (w_up_fwd, w_up_bwd, a_up_fwd, a_up_bwd)],
                g_up[l].astype(BF16), ones_bd)
            yf, yb = _rwkv_scan(r, v, kk, lwf, bf, kf, lwb, bb, kb)
            h, hn = _outproj(yf, yb, bonus, g, att, cols, row(ln_x_gain[l]), row(ln_x_bias[l]), ones_bd,
                             w_out[l].astype(BF16), h, row(norm_post_mix[l]), row(norm_pre_ffn[l]))
            h = _ffn(hn, w_ffn_up[l].astype(BF16), w_ffn_down[l].astype(BF16), h, row(norm_post_ffn[l]))
        outs.append(h)
    return jnp.stack(outs, axis=0)
```

```python
import functools

import jax
import jax.numpy as jnp
import numpy as np
from jax import lax
from jax.experimental import pallas as pl
from jax.experimental.pallas import tpu as pltpu

F32 = jnp.float32
BF16 = jnp.bfloat16
LANES = 128
SUBLANES = 8
PACKED_SUBLANES = 16

D_MODEL = 2048
ATT_HEAD_DIM = 128
ATT_HEADS = 16
ATT_KV_HEADS = 4
ATT_GROUP = 4
WINDOW = 128
BLOCK = 128
ROPE_THETA = 10000.0
RWKV_HEAD_DIM = 64
RWKV_DIM = 2048
N_PAIRS = RWKV_DIM // LANES
DECAY_LORA = 96
ICLR_LORA = 96
GATE_LORA = 256
LORA_PAD = 128
LORA_COLS = 1024
D_FF = 4 * D_MODEL
NORM_EPS = 1e-6
GN_EPS = 64e-5
MASK_VALUE = -1e30
DECAY_SCALE = float(np.exp(-0.5))
KV_COLS = ATT_KV_HEADS * ATT_HEAD_DIM

CB_Q, CB_GA, CB_GR, CB_R, CB_K, CB_V, CB_AK, CB_AV, CB_LORA = 0, 16, 32, 48, 64, 80, 96, 100, 104
IN_COLS_PAD = 112 * LANES

CHUNK = 128
VMEM_LIMIT = 56 * 1024 * 1024

NN = (((1,), (0,)), ((), ()))
NT = (((1,), (1,)), ((), ()))
TN = (((0,), (0,)), ((), ()))


def _mm(a, b, dims=NN):
    return lax.dot_general(a.astype(BF16), b.astype(BF16), dims, preferred_element_type=F32)


def _split_bf16(x, parts):
    out = []
    for _ in range(parts - 1):
        hi = x.astype(BF16)
        out.append(hi)
        x = x - hi.astype(F32)
    out.append(x.astype(BF16))
    return out


def _mm_exact_lhs(a, b, parts):
    a = a.astype(BF16)
    acc = None
    for term in _split_bf16(b, parts):
        p = lax.dot_general(a, term, NN, preferred_element_type=F32)
        acc = p if acc is None else acc + p
    return acc


def _cparams(sem):
    return pltpu.CompilerParams(dimension_semantics=sem, vmem_limit_bytes=VMEM_LIMIT)


def _inproj_kernel(x_ref, g_ref, w_ref, o_ref, xn_ref):
    @pl.when(pl.program_id(1) == 0)
    def _():
        x = x_ref[...]
        ms = jnp.mean(x * x, axis=-1, keepdims=True)
        xn_ref[...] = (x * lax.rsqrt(ms + NORM_EPS) * g_ref[...]).astype(BF16)

    o_ref[...] = jnp.dot(xn_ref[...], w_ref[...], preferred_element_type=F32).astype(o_ref.dtype)


def _inproj(x, gain, w, tm=1024, tn=1024):
    T = x.shape[0]
    n = w.shape[1]
    return pl.pallas_call(
        _inproj_kernel,
        grid=(T // tm, n // tn),
        in_specs=[
            pl.BlockSpec((tm, D_MODEL), lambda i, j: (i, 0)),
            pl.BlockSpec((1, D_MODEL), lambda i, j: (0, 0)),
            pl.BlockSpec((D_MODEL, tn), lambda i, j: (0, j)),
        ],
        out_specs=pl.BlockSpec((tm, tn), lambda i, j: (i, j)),
        out_shape=jax.ShapeDtypeStruct((T, n), BF16),
        scratch_shapes=[pltpu.VMEM((tm, D_MODEL), BF16)],
        compiler_params=_cparams(("parallel", "arbitrary")),
        name="inproj",
    )(x, gain, w)


def _rope(x, c, s):
    return x * c + pltpu.roll(x, ATT_HEAD_DIM // 2, 1) * s


def _attn_kernel(sink_ref, q_ref, kp_ref, kc_ref, kn_ref, vp_ref, vc_ref, vn_ref, gate_ref,
                 cc_ref, sc_ref, cp_ref, sp_ref, cn_ref, sn_ref, o_ref, *, seq):
    i = pl.program_id(0)
    cc, sc = cc_ref[...], sc_ref[...]
    cp, sp = cp_ref[...], sp_ref[...]
    cn, sn = cn_ref[...], sn_ref[...]
    qi = lax.broadcasted_iota(jnp.int32, (BLOCK, 3 * BLOCK), 0)
    sj = lax.broadcasted_iota(jnp.int32, (BLOCK, 3 * BLOCK), 1)
    kpos = (i - 1) * BLOCK + sj
    valid = (jnp.abs(sj - BLOCK - qi) <= WINDOW) & (kpos >= 0) & (kpos < seq)
    valid4 = jnp.concatenate([valid] * ATT_GROUP, axis=0)
    rowg = lax.broadcasted_iota(jnp.int32, (ATT_GROUP * BLOCK, 1), 0) // BLOCK
    scale = ATT_HEAD_DIM ** -0.5
    for g in range(ATT_KV_HEADS):
        ks = slice(g * ATT_HEAD_DIM, (g + 1) * ATT_HEAD_DIM)
        kw = jnp.concatenate([_rope(kp_ref[:, ks].astype(F32), cp, sp), _rope(kc_ref[:, ks].astype(F32), cc, sc),
                              _rope(kn_ref[:, ks].astype(F32), cn, sn)], axis=0)
        vw = jnp.concatenate([vp_ref[:, ks], vc_ref[:, ks], vn_ref[:, ks]], axis=0)
        heads = [g * ATT_GROUP + hh for hh in range(ATT_GROUP)]
        q4 = jnp.concatenate(
            [_rope(q_ref[:, h * ATT_HEAD_DIM:(h + 1) * ATT_HEAD_DIM].astype(F32), cc, sc) for h in heads],
            axis=0) * scale
        s = _mm(q4, kw, NT)
        s = jnp.where(valid4, s, MASK_VALUE)
        sink = jnp.zeros((ATT_GROUP * BLOCK, 1), F32)
        for hh, h in enumerate(heads):
            sink = jnp.where(rowg == hh, sink_ref[h], sink)
        m = jnp.maximum(jnp.max(s, axis=-1, keepdims=True), sink)
        p = jnp.exp(s - m)
        den = jnp.sum(p, axis=-1, keepdims=True) + jnp.exp(sink - m)
        o = _mm(p, vw) / den
        for hh, h in enumerate(heads):
            cs = slice(h * ATT_HEAD_DIM, (h + 1) * ATT_HEAD_DIM)
            o_ref[:, cs] = (o[hh * BLOCK:(hh + 1) * BLOCK] * jax.nn.sigmoid(gate_ref[:, cs].astype(F32))).astype(o_ref.dtype)


def _attention(cols, sink, cos2, sin2):
    T = cols.shape[0]
    nb = T // BLOCK
    prev = lambda i: (jnp.maximum(i - 1, 0), 0)
    cur = lambda i: (i, 0)
    nxt = lambda i: (jnp.minimum(i + 1, nb - 1), 0)
    kcb, vcb = CB_AK * LANES // KV_COLS, CB_AV * LANES // KV_COLS
    col = lambda f, cb: (lambda i: (f(i)[0], cb))
    tab = pl.BlockSpec
    return pl.pallas_call(
        functools.partial(_attn_kernel, seq=T),
        grid=(nb,),
        in_specs=[
            pl.BlockSpec(memory_space=pltpu.SMEM),
            pl.BlockSpec((BLOCK, D_MODEL), col(cur, CB_Q // 16)),
            pl.BlockSpec((BLOCK, KV_COLS), col(prev, kcb)),
            pl.BlockSpec((BLOCK, KV_COLS), col(cur, kcb)),
            pl.BlockSpec((BLOCK, KV_COLS), col(nxt, kcb)),
            pl.BlockSpec((BLOCK, KV_COLS), col(prev, vcb)),
            pl.BlockSpec((BLOCK, KV_COLS), col(cur, vcb)),
            pl.BlockSpec((BLOCK, KV_COLS), col(nxt, vcb)),
            pl.BlockSpec((BLOCK, D_MODEL), col(cur, CB_GA // 16)),
            tab((BLOCK, LANES), cur), tab((BLOCK, LANES), cur),
            tab((BLOCK, LANES), prev), tab((BLOCK, LANES), prev),
            tab((BLOCK, LANES), nxt), tab((BLOCK, LANES), nxt),
        ],
        out_specs=pl.BlockSpec((BLOCK, D_MODEL), cur),
        out_shape=jax.ShapeDtypeStruct((T, D_MODEL), BF16),
        compiler_params=_cparams(("parallel",)),
        name="attention",
    )(sink, cols, cols, cols, cols, cols, cols, cols, cols, cos2, sin2, cos2, sin2, cos2, sin2)


def _head_sum(x, ones_bd):
    return _mm(x, ones_bd)


def _prep_kernel(r_ref, rp_ref, rn_ref, k_ref, kp_ref, kn_ref, v_ref, vp_ref, vn_ref,
                 lo_ref, lop_ref, lon_ref, mur_ref, muk_ref, muv_ref, mul_ref,
                 w0f_ref, w0b_ref, a0f_ref, a0b_ref, kk_ref, ka_ref, rk_ref,
                 wuf_ref, wub_ref, auf_ref, aub_ref, gup_ref, bd_ref,
                 r_o, v_o, kk_o, lwf_o, bf_o, kf_o, lwb_o, bb_o, kb_o, g_o, bonus_o, act_ref):
    i = pl.program_id(0)
    first = i == 0
    last = i == pl.num_programs(0) - 1
    tm = r_ref.shape[0]

    def shift(c_ref, p_ref, n_ref, mu):
        c = c_ref[...].astype(F32)
        row = lax.broadcasted_iota(jnp.int32, c.shape, 0)
        prow = jnp.where(first, 0.0, p_ref[PACKED_SUBLANES - 1:PACKED_SUBLANES, :].astype(F32))
        nrow = jnp.where(last, 0.0, n_ref[0:1, :].astype(F32))
        prev = jnp.where(row == 0, prow, pltpu.roll(c, 1, 0))
        nxt = jnp.where(row == tm - 1, nrow, pltpu.roll(c, tm - 1, 0))
        return c * (1.0 - mu) + (prev + nxt) * (0.5 * mu)

    r = shift(r_ref, rp_ref, rn_ref, mur_ref[...])
    k = shift(k_ref, kp_ref, kn_ref, muk_ref[...])
    v = shift(v_ref, vp_ref, vn_ref, muv_ref[...])
    P = LORA_PAD

    @pl.when(pl.program_id(1) == 0)
    def _():
        lo = shift(lo_ref, lop_ref, lon_ref, mul_ref[...])
        act_ref[:, 0:2 * P] = jnp.tanh(lo[:, 0:2 * P]).astype(BF16)
        act_ref[:, 2 * P:4 * P] = lo[:, 2 * P:4 * P].astype(BF16)
        act_ref[:, 4 * P:] = jax.nn.sigmoid(lo[:, 4 * P:4 * P + GATE_LORA]).astype(BF16)

    ones_bd = bd_ref[...]

    kk = k * kk_ref[...]
    kk = kk * lax.rsqrt(jnp.maximum(_head_sum(kk * kk, ones_bd), 1e-24))
    k_a = ka_ref[...]

    def direction(wd, ad, w0, wu, a0, au):
        lw = -DECAY_SCALE * jax.nn.sigmoid(w0 + _mm(wd, wu))
        a = jax.nn.sigmoid(a0 + _mm(ad, au))
        k_mod = k * (1.0 + (a - 1.0) * k_a)
        return lw, kk * a, k_mod

    lwf, bf, kf = direction(act_ref[:, 0:P], act_ref[:, 2 * P:3 * P],
                            w0f_ref[...], wuf_ref[...], a0f_ref[...], auf_ref[...])
    lwb, bb, kb = direction(act_ref[:, P:2 * P], act_ref[:, 3 * P:4 * P],
                            w0b_ref[...], wub_ref[...], a0b_ref[...], aub_ref[...])
    g = _mm(act_ref[:, 4 * P:], gup_ref[...])
    bonus = _head_sum(r * (0.5 * (kf + kb)) * rk_ref[...], ones_bd) * v

    for ref, val in ((r_o, r), (v_o, v), (kk_o, kk), (lwf_o, lwf), (bf_o, bf), (kf_o, kf),
                     (lwb_o, lwb), (bb_o, bb), (kb_o, kb), (g_o, g), (bonus_o, bonus)):
        ref[...] = val.astype(ref.dtype)


def _rwkv_prep(cols, mu_rkv, mu_lora, vecs, w_up_f, w_up_b, a_up_f, a_up_b, g_up, ones_bd, tm=1024):
    T = cols.shape[0]
    nt = T // tm
    halo = PACKED_SUBLANES
    hb = tm // halo
    nhb = T // halo
    main = lambda cb: pl.BlockSpec((tm, LANES), lambda i, j: (i, cb + j))
    hprev = lambda cb: pl.BlockSpec((halo, LANES), lambda i, j: (jnp.maximum(i * hb - 1, 0), cb + j))
    hnext = lambda cb: pl.BlockSpec((halo, LANES), lambda i, j: (jnp.minimum((i + 1) * hb, nhb - 1), cb + j))
    lcb = CB_LORA * LANES // LORA_COLS
    vec = lambda off: pl.BlockSpec((1, LANES), lambda i, j: (0, off + j))
    up = lambda rows: pl.BlockSpec((rows, LANES), lambda i, j: (0, j))
    in_specs = []
    for cb in (CB_R, CB_K, CB_V):
        in_specs += [main(cb), hprev(cb), hnext(cb)]
    in_specs += [
        pl.BlockSpec((tm, LORA_COLS), lambda i, j: (i, lcb)),
        pl.BlockSpec((halo, LORA_COLS), lambda i, j: (jnp.maximum(i * hb - 1, 0), lcb)),
        pl.BlockSpec((halo, LORA_COLS), lambda i, j: (jnp.minimum((i + 1) * hb, nhb - 1), lcb)),
        vec(0), vec(N_PAIRS), vec(2 * N_PAIRS),
        pl.BlockSpec((1, LORA_COLS), lambda i, j: (0, 0)),
    ]
    in_specs += [vec(0)] * 7
    in_specs += [up(LORA_PAD)] * 4 + [up(GATE_LORA)]
    in_specs += [pl.BlockSpec((LANES, LANES), lambda i, j: (0, 0))]
    dtypes = [BF16, BF16, BF16, F32, BF16, BF16, F32, BF16, BF16, BF16, BF16]
    return pl.pallas_call(
        _prep_kernel,
        grid=(nt, N_PAIRS),
        in_specs=in_specs,
        out_specs=[pl.BlockSpec((tm, LANES), lambda i, j: (i, j))] * 11,
        out_shape=[jax.ShapeDtypeStruct((T, RWKV_DIM), dt) for dt in dtypes],
        scratch_shapes=[pltpu.VMEM((tm, 4 * LORA_PAD + GATE_LORA), BF16)],
        compiler_params=_cparams(("parallel", "arbitrary")),
        name="rwkv_prep",
    )(cols, cols, cols, cols, cols, cols, cols, cols, cols, cols, cols, cols,
      mu_rkv, mu_rkv, mu_rkv, mu_lora, *vecs, w_up_f, w_up_b, a_up_f, a_up_b, g_up, ones_bd)


def _tri_inverse_all(mats, upper):
    L = mats[0].shape[0]
    row = lax.broadcasted_iota(jnp.int32, (L, L), 0)
    col = lax.broadcasted_iota(jnp.int32, (L, L), 1)
    hrow = lax.broadcasted_iota(jnp.int32, (L // 2, L), 0)
    hcol = lax.broadcasted_iota(jnp.int32, (L // 2, L), 1)
    same = lambda sh: (row >> sh) == (col >> sh)
    eye = jnp.where(row == col, 1.0, 0.0)
    ds = [eye + jnp.where(same(1), a, 0.0) for a in mats]
    sh = 1
    while (1 << sh) < L:
        m = 1 << sh
        dbs = [d.astype(BF16) for d in ds]
        if m < SUBLANES:
            level = same(sh + 1) & jnp.logical_not(same(sh))
            ts = [_mm(jnp.where(level, a, 0.0), db) for a, db in zip(mats, dbs)]
            ds = [d + _mm(db, t) for d, db, t in zip(ds, dbs, ts)]
        else:
            n_half = L // (2 * m)
            part = [slice(q * m, (q + 1) * m) for q in range(n_half)]
            blocks = lambda x: [x[q * m:(q + 1) * m] for q in range(2 * n_half)]
            zero = jnp.zeros((m, L), F32)
            act = lambda x, up: jnp.concatenate(blocks(x)[(0 if up else 1)::2], axis=0)
            level = {False: (hcol >> sh) == 2 * (hrow >> sh), True: (hcol >> sh) == 2 * (hrow >> sh) + 1}

            def spread(x, up, rest=None):
                out = []
                for q in range(n_half):
                    other = zero if rest is None else rest[2 * q + (1 if up else 0)]
                    out += [x[part[q]], other] if up else [other, x[part[q]]]
                return jnp.concatenate(out, axis=0)

            ts = [_mm(jnp.where(level[up], act(a, up), 0.0), db) for a, db, up in zip(mats, dbs, upper)]
            upd = [_mm(act(d, up), spread(t, up)) for d, t, up in zip(ds, ts, upper)]
            ds = [spread(act(d, up) + u, up, rest=blocks(d)) for d, u, up in zip(ds, upd, upper)]
        sh += 1
    return ds


def _chunk_all(insts):
    L = insts[0][0].shape[0]
    n_inst = len(insts)
    row = lax.broadcasted_iota(jnp.int32, (L, L), 0)
    col = lax.broadcasted_iota(jnp.int32, (L, L), 1)
    incl = {False: col <= row, True: col >= row}
    strict = {False: col < row, True: col > row}
    ones = {rev: jnp.where(incl[rev], 1.0, 0.0) for rev in (False, True)}
    incl2 = {rev: jnp.concatenate([incl[rev]] * 2, axis=1) for rev in (False, True)}
    strict2 = {rev: jnp.concatenate([strict[rev]] * 2, axis=1) for rev in (False, True)}
    lane = lax.broadcasted_iota(jnp.int32, (1, LANES), 1)
    h0 = lane < RWKV_HEAD_DIM
    srow = lax.broadcasted_iota(jnp.int32, (LANES, LANES), 0)
    scol = lax.broadcasted_iota(jnp.int32, (LANES, LANES), 1)
    same_head = (srow >= RWKV_HEAD_DIM) == (scol >= RWKV_HEAD_DIM)

    cums = [_mm_exact_lhs(ones[inst[7]], inst[3], 2) for inst in insts]
    pre = []
    for (r, v, kk, lw, b, k, s, rev), cum in zip(insts, cums):
        tot = cum[0:1] if rev else cum[L - 1:L]
        mid = L // 2 if rev else L // 2 - 1
        rho = cum[mid:mid + 1]
        e1 = jnp.exp(cum - rho)
        e2 = jnp.exp(rho - cum)
        er = jnp.exp(rho)
        et = jnp.exp(tot - rho)
        a_t = -kk * e1 * jnp.exp(-lw)
        r_t = r * e1
        b_t = b * e2
        k_t = k * e2
        pre.append((a_t, r_t, b_t, k_t, er, et))
    grams = []
    for a_t, r_t, b_t, k_t, _, _ in pre:
        zero = jnp.zeros_like(a_t)
        lhs = jnp.concatenate([jnp.where(h0, a_t, zero), jnp.where(h0, zero, a_t),
                               jnp.where(h0, r_t, zero), jnp.where(h0, zero, r_t)], axis=0)
        rhs = jnp.concatenate([b_t, k_t], axis=0)
        grams.append(_mm(lhs, rhs, NT))
    a_bk, r_bk = [], []
    for inst, gram in zip(insts, grams):
        rev = inst[7]
        a_bk.append([jnp.where(strict2[rev], gram[hh * L:(hh + 1) * L], 0.0) for hh in range(2)])
        r_bk.append([jnp.where(incl2[rev], gram[(2 + hh) * L:(3 + hh) * L], 0.0) for hh in range(2)])
    t_inv = _tri_inverse_all([a_bk[n][hh][:, :L] for n in range(n_inst) for hh in range(2)],
                             [insts[n][7] for n in range(n_inst) for hh in range(2)])
    xs = [_mm(jnp.concatenate([a_t * er, r_t * er], axis=0), inst[6], NT)
          for inst, (a_t, r_t, _, _, er, _) in zip(insts, pre)]
    pick = lambda t2: jnp.where(h0, t2[:L], t2[L:])
    akv = [_mm(jnp.concatenate([a_bk[n][0][:, L:], a_bk[n][1][:, L:]], axis=0), inst[1])
           for n, inst in enumerate(insts)]
    x = [xs[n][:L] + pick(akv[n]) for n in range(n_inst)]
    u = [pick(_mm(jnp.concatenate([t_inv[2 * n], t_inv[2 * n + 1]], axis=0), x[n])) for n in range(n_inst)]
    z = [jnp.concatenate([u[n], inst[1]], axis=0) for n, inst in enumerate(insts)]
    out = []
    for n, inst in enumerate(insts):
        _, _, b_t, k_t, er, et = pre[n]
        y = xs[n][L:] + pick(_mm(jnp.concatenate([r_bk[n][0], r_bk[n][1]], axis=0), z[n]))
        w = jnp.concatenate([b_t * et, k_t * et], axis=0)
        s_new = inst[6] * (er * et) + jnp.where(same_head, _mm(z[n], w, TN), 0.0)
        out.append((y, s_new))
    return out


def _scan_kernel(rf_ref, vf_ref, kkf_ref, lwf_ref, bf_ref, kf_ref,
                 rb_ref, vb_ref, kkb_ref, lwb_ref, bb_ref, kb_ref,
                 yf_ref, yb_ref, s_ref):
    @pl.when(pl.program_id(1) == 0)
    def _():
        s_ref[...] = jnp.zeros_like(s_ref)

    n_pairs = s_ref.shape[1]
    insts = []
    for p in range(n_pairs):
        cs = slice(p * LANES, (p + 1) * LANES)
        insts.append(tuple(ref[:, cs].astype(F32) for ref in (rf_ref, vf_ref, kkf_ref, lwf_ref, bf_ref, kf_ref))
                     + (s_ref[0, p], False))
        insts.append(tuple(ref[:, cs].astype(F32) for ref in (rb_ref, vb_ref, kkb_ref, lwb_ref, bb_ref, kb_ref))
                     + (s_ref[1, p], True))
    res = _chunk_all(insts)
    for p in range(n_pairs):
        cs = slice(p * LANES, (p + 1) * LANES)
        yf_ref[:, cs], s_ref[0, p] = res[2 * p]
        yb_ref[:, cs], s_ref[1, p] = res[2 * p + 1]


def _rwkv_scan(r, v, kk, lwf, bf, kf, lwb, bb, kb, pairs_per_step=8):
    T = r.shape[0]
    nc = T // CHUNK
    width = pairs_per_step * LANES
    fwd = pl.BlockSpec((CHUNK, width), lambda p, c: (c, p))
    bwd = pl.BlockSpec((CHUNK, width), lambda p, c: (nc - 1 - c, p))
    out = jax.ShapeDtypeStruct((T, RWKV_DIM), F32)
    return pl.pallas_call(
        _scan_kernel,
        grid=(N_PAIRS // pairs_per_step, nc),
        in_specs=[fwd] * 6 + [bwd] * 6,
        out_specs=[fwd, bwd],
        out_shape=[out, out],
        scratch_shapes=[pltpu.VMEM((2, pairs_per_step, LANES, LANES), F32)],
        compiler_params=_cparams(("parallel", "arbitrary")),
        name="rwkv_scan",
    )(r, v, kk, lwf, bf, kf, r, v, kk, lwb, bb, kb)


def _rms(x, gain):
    return x * lax.rsqrt(jnp.mean(x * x, axis=-1, keepdims=True) + NORM_EPS) * gain


def _outproj_kernel(yf_ref, yb_ref, bonus_ref, g_ref, att_ref, gr_ref, gain_ref, bias_ref, bd_ref,
                    w_ref, x_ref, gpost_ref, gpre_ref, h_ref, hn_ref, m_ref):
    ones_bd = bd_ref[...]
    inv_n = 1.0 / RWKV_HEAD_DIM
    for j in range(N_PAIRS):
        cs = slice(j * LANES, (j + 1) * LANES)
        y = yf_ref[:, cs] + yb_ref[:, cs]
        mean = _head_sum(y, ones_bd) * inv_n
        d = y - mean
        var = _head_sum(d * d, ones_bd) * inv_n
        yn = d * lax.rsqrt(var + GN_EPS) * gain_ref[:, cs] + bias_ref[:, cs]
        o_rwkv = (yn + bonus_ref[:, cs].astype(F32)) * g_ref[:, cs].astype(F32)
        merged = att_ref[:, cs].astype(F32) + jax.nn.sigmoid(gr_ref[:, cs].astype(F32)) * o_rwkv
        m_ref[:, cs] = merged.astype(BF16)
    mix = jnp.dot(m_ref[...], w_ref[...], preferred_element_type=F32)
    h = x_ref[...] + _rms(mix, gpost_ref[...])
    h_ref[...] = h
    hn_ref[...] = _rms(h, gpre_ref[...]).astype(BF16)


def _outproj(yf, yb, bonus, g, att, cols, gain, bias, ones_bd, w_out, x, g_post, g_pre, tm=256):
    T = x.shape[0]
    row = pl.BlockSpec((tm, D_MODEL), lambda i: (i, 0))
    vec = pl.BlockSpec((1, D_MODEL), lambda i: (0, 0))
    return pl.pallas_call(
        _outproj_kernel,
        grid=(T // tm,),
        in_specs=[row, row, row, row, row,
                  pl.BlockSpec((tm, D_MODEL), lambda i: (i, CB_GR // 16)),
                  vec, vec, pl.BlockSpec((LANES, LANES), lambda i: (0, 0)),
                  pl.BlockSpec((D_MODEL, D_MODEL), lambda i: (0, 0), pipeline_mode=pl.Buffered(1)),
                  row, vec, vec],
        out_specs=[row, row],
        out_shape=[jax.ShapeDtypeStruct((T, D_MODEL), F32), jax.ShapeDtypeStruct((T, D_MODEL), BF16)],
        scratch_shapes=[pltpu.VMEM((tm, D_MODEL), BF16)],
        compiler_params=_cparams(("parallel",)),
        name="outproj",
    )(yf, yb, bonus, g, att, cols, gain, bias, ones_bd, w_out, x, g_post, g_pre)


def _ffn_kernel(hn_ref, wu_ref, wd_ref, h_ref, g_ref, o_ref, acc_ref):
    j = pl.program_id(1)

    @pl.when(j == 0)
    def _():
        acc_ref[...] = jnp.zeros_like(acc_ref)

    up = jnp.dot(hn_ref[...], wu_ref[...], preferred_element_type=F32)
    act = jnp.square(jnp.maximum(up, 0.0)).astype(BF16)
    acc_ref[...] += jnp.dot(act, wd_ref[...], preferred_element_type=F32)

    @pl.when(j == pl.num_programs(1) - 1)
    def _():
        o_ref[...] = h_ref[...] + _rms(acc_ref[...], g_ref[...])


def _ffn(hn, w_up, w_down, h, gain, tm=1024, tf=512):
    T = h.shape[0]
    row = pl.BlockSpec((tm, D_MODEL), lambda i, j: (i, 0))
    row_once = pl.BlockSpec((tm, D_MODEL), lambda i, j: (i, 0), pipeline_mode=pl.Buffered(1))
    return pl.pallas_call(
        _ffn_kernel,
        grid=(T // tm, D_FF // tf),
        in_specs=[row,
                  pl.BlockSpec((D_MODEL, tf), lambda i, j: (0, j)),
                  pl.BlockSpec((tf, D_MODEL), lambda i, j: (j, 0)),
                  row_once,
                  pl.BlockSpec((1, D_MODEL), lambda i, j: (0, 0))],
        out_specs=row_once,
        out_shape=jax.ShapeDtypeStruct((T, D_MODEL), F32),
        scratch_shapes=[pltpu.VMEM((tm, D_MODEL), F32)],
        compiler_params=_cparams(("parallel", "arbitrary")),
        name="ffn",
    )(hn, w_up, w_down, h, gain)


def _pad_cols(t, n):
    return jnp.pad(t, ((0, 0), (0, n - t.shape[1])))


def _pad_rows(t, n):
    return jnp.pad(t, ((0, n - t.shape[0]), (0, 0)))


def _split_cols(t, sizes):
    idx = [int(i) for i in np.cumsum(sizes)[:-1]]
    return jnp.split(t, idx, axis=-1)


def _permute_in_cols(t):
    shift_sizes = [RWKV_DIM] * 3 + [DECAY_LORA] * 2 + [ICLR_LORA] * 2 + [GATE_LORA]
    q, ak, av, rw, ga, gr = _split_cols(t, [D_MODEL, KV_COLS, KV_COLS, sum(shift_sizes), D_MODEL, D_MODEL])
    r, k, v, wdf, wdb, adf, adb, gd = _split_cols(rw, shift_sizes)
    lora = jnp.concatenate([_pad_cols(p, LORA_PAD) for p in (wdf, wdb, adf, adb)] + [gd], axis=1)
    return jnp.concatenate([q, ga, gr, r, k, v, ak, av, _pad_cols(lora, LORA_COLS)], axis=1)


def _rope_tables(T):
    pos = jnp.arange(T, dtype=F32)
    inv_freq = ROPE_THETA ** (-jnp.arange(0, ATT_HEAD_DIM, 2, dtype=F32) / ATT_HEAD_DIM)
    ang = pos[:, None] * inv_freq[None, :]
    cos, sin = jnp.cos(ang), jnp.sin(ang)
    return jnp.concatenate([cos, cos], axis=1), jnp.concatenate([-sin, sin], axis=1)


def kernel(x, norm_pre_mix, w_in, mu_shift, attn_sink, w0_fwd, w_up_fwd, w0_bwd, w_up_bwd, a0_fwd, a_up_fwd, a0_bwd, a_up_bwd, g_up, k_k, k_a, r_k, ln_x_gain, ln_x_bias, w_out, norm_post_mix, norm_pre_ffn, w_ffn_up, w_ffn_down, norm_post_ffn):
    B, T, _ = x.shape
    depth = w_in.shape[0]
    cos2, sin2 = _rope_tables(T)
    lane = np.arange(LANES)
    ones_bd = jnp.asarray((lane[:, None] // RWKV_HEAD_DIM) == (lane[None, :] // RWKV_HEAD_DIM), BF16)
    row = lambda t: t.reshape(1, -1)
    outs = []
    for bi in range(B):
        h = x[bi]
        for l in range(depth):
            w_perm = _permute_in_cols(w_in[l].astype(BF16))
            mu = mu_shift[l].reshape(1, -1)
            shift_sizes = [RWKV_DIM] * 3 + [DECAY_LORA] * 2 + [ICLR_LORA] * 2 + [GATE_LORA]
            mr, mk, mv, m1, m2, m3, m4, mg = _split_cols(mu, shift_sizes)
            mu_rkv = jnp.concatenate([mr, mk, mv], axis=1)
            mu_lora = _pad_cols(jnp.concatenate([_pad_cols(p, LORA_PAD) for p in (m1, m2, m3, m4)] + [mg], axis=1),
                                LORA_COLS)
            cols = _inproj(h, row(norm_pre_mix[l]), w_perm)
            att = _attention(cols, attn_sink[l], cos2, sin2)
            vecs = [row(t[l]) for t in (w0_fwd, w0_bwd, a0_fwd, a0_bwd, k_k, k_a, r_k)]
            r, v, kk, lwf, bf, kf, lwb, bb, kb, g, bonus = _rwkv_prep(
                cols, mu_rkv, mu_lora, vecs,
                *[_pad_rows(t[l], LORA_PAD).astype(BF16) for t in (w_up_fwd, w_up_bwd, a_up_fwd, a_up_bwd)],
                g_up[l].astype(BF16), ones_bd)
            yf, yb = _rwkv_scan(r, v, kk, lwf, bf, kf, lwb, bb, kb)
            h, hn = _outproj(yf, yb, bonus, g, att, cols, row(ln_x_gain[l]), row(ln_x_bias[l]), ones_bd,
                             w_out[l].astype(BF16), h, row(norm_post_mix[l]), row(norm_pre_ffn[l]))
            h = _ffn(hn, w_ffn_up[l].astype(BF16), w_ffn_down[l].astype(BF16), h, row(norm_post_ffn[l]))
        outs.append(h)
    return jnp.stack(outs, axis=0)
```

```python
import functools

import jax
import jax.numpy as jnp
import numpy as np
from jax import lax
from jax.experimental import pallas as pl
from jax.experimental.pallas import tpu as pltpu

F32 = jnp.float32
BF16 = jnp.bfloat16
LANES = 128
SUBLANES = 8
PACKED_SUBLANES = 16

D_MODEL = 2048
ATT_HEAD_DIM = 128
ATT_HEADS = 16
ATT_KV_HEADS = 4
ATT_GROUP = 4
WINDOW = 128
BLOCK = 128
ROPE_THETA = 10000.0
RWKV_HEAD_DIM = 64
RWKV_DIM = 2048
N_PAIRS = RWKV_DIM // LANES
DECAY_LORA = 96
ICLR_LORA = 96
GATE_LORA = 256
LORA_PAD = 128
LORA_COLS = 1024
D_FF = 4 * D_MODEL
NORM_EPS = 1e-6
GN_EPS = 64e-5
MASK_VALUE = -1e30
DECAY_SCALE = float(np.exp(-0.5))
KV_COLS = ATT_KV_HEADS * ATT_HEAD_DIM

CB_Q, CB_GA, CB_GR, CB_AK, CB_AV, CB_LORA = 0, 16, 32, 48, 52, 56
RKV_PAIRS_PER_TILE = 2

CHUNK = 128
VMEM_LIMIT = 56 * 1024 * 1024

NN = (((1,), (0,)), ((), ()))
NT = (((1,), (1,)), ((), ()))
TN = (((0,), (0,)), ((), ()))


def _mm(a, b, dims=NN):
    return lax.dot_general(a.astype(BF16), b.astype(BF16), dims, preferred_element_type=F32)


def _split_bf16(x, parts):
    out = []
    for _ in range(parts - 1):
        hi = x.astype(BF16)
        out.append(hi)
        x = x - hi.astype(F32)
    out.append(x.astype(BF16))
    return out


def _mm_exact_lhs(a, b, parts):
    a = a.astype(BF16)
    acc = None
    for term in _split_bf16(b, parts):
        p = lax.dot_general(a, term, NN, preferred_element_type=F32)
        acc = p if acc is None else acc + p
    return acc


def _cparams(sem):
    return pltpu.CompilerParams(dimension_semantics=sem, vmem_limit_bytes=VMEM_LIMIT)


def _inproj_kernel(x_ref, g_ref, w_ref, o_ref, xn_ref):
    @pl.when(pl.program_id(1) == 0)
    def _():
        x = x_ref[...]
        ms = jnp.mean(x * x, axis=-1, keepdims=True)
        xn_ref[...] = (x * lax.rsqrt(ms + NORM_EPS) * g_ref[...]).astype(BF16)

    o_ref[...] = jnp.dot(xn_ref[...], w_ref[...], preferred_element_type=F32).astype(o_ref.dtype)


def _inproj(x, gain, w, tm=1024, tn=1024):
    T = x.shape[0]
    n = w.shape[1]
    return pl.pallas_call(
        _inproj_kernel,
        grid=(T // tm, n // tn),
        in_specs=[
            pl.BlockSpec((tm, D_MODEL), lambda i, j: (i, 0)),
            pl.BlockSpec((1, D_MODEL), lambda i, j: (0, 0)),
            pl.BlockSpec((D_MODEL, tn), lambda i, j: (0, j)),
        ],
        out_specs=pl.BlockSpec((tm, tn), lambda i, j: (i, j)),
        out_shape=jax.ShapeDtypeStruct((T, n), BF16),
        scratch_shapes=[pltpu.VMEM((tm, D_MODEL), BF16)],
        compiler_params=_cparams(("parallel", "arbitrary")),
        name="inproj",
    )(x, gain, w)


def _rope(x, c, s):
    return x * c + pltpu.roll(x, ATT_HEAD_DIM // 2, 1) * s


def _attn_kernel(sink_ref, q_ref, kp_ref, kc_ref, kn_ref, vp_ref, vc_ref, vn_ref, gate_ref,
                 cc_ref, sc_ref, cp_ref, sp_ref, cn_ref, sn_ref, o_ref, *, seq):
    i = pl.program_id(0)
    cc, sc = cc_ref[...], sc_ref[...]
    cp, sp = cp_ref[...], sp_ref[...]
    cn, sn = cn_ref[...], sn_ref[...]
    qi = lax.broadcasted_iota(jnp.int32, (BLOCK, 3 * BLOCK), 0)
    sj = lax.broadcasted_iota(jnp.int32, (BLOCK, 3 * BLOCK), 1)
    kpos = (i - 1) * BLOCK + sj
    valid = (jnp.abs(sj - BLOCK - qi) <= WINDOW) & (kpos >= 0) & (kpos < seq)
    valid4 = jnp.concatenate([valid] * ATT_GROUP, axis=0)
    rowg = lax.broadcasted_iota(jnp.int32, (ATT_GROUP * BLOCK, 1), 0) // BLOCK
    scale = ATT_HEAD_DIM ** -0.5
    for g in range(ATT_KV_HEADS):
        ks = slice(g * ATT_HEAD_DIM, (g + 1) * ATT_HEAD_DIM)
        kw = jnp.concatenate([_rope(kp_ref[:, ks].astype(F32), cp, sp), _rope(kc_ref[:, ks].astype(F32), cc, sc),
                              _rope(kn_ref[:, ks].astype(F32), cn, sn)], axis=0)
        vw = jnp.concatenate([vp_ref[:, ks], vc_ref[:, ks], vn_ref[:, ks]], axis=0)
        heads = [g * ATT_GROUP + hh for hh in range(ATT_GROUP)]
        q4 = jnp.concatenate(
            [_rope(q_ref[:, h * ATT_HEAD_DIM:(h + 1) * ATT_HEAD_DIM].astype(F32), cc, sc) for h in heads],
            axis=0) * scale
        s = _mm(q4, kw, NT)
        s = jnp.where(valid4, s, MASK_VALUE)
        sink = jnp.zeros((ATT_GROUP * BLOCK, 1), F32)
        for hh, h in enumerate(heads):
            sink = jnp.where(rowg == hh, sink_ref[h], sink)
        m = jnp.maximum(jnp.max(s, axis=-1, keepdims=True), sink)
        p = jnp.exp(s - m)
        den = jnp.sum(p, axis=-1, keepdims=True) + jnp.exp(sink - m)
        o = _mm(p, vw) / den
        for hh, h in enumerate(heads):
            cs = slice(h * ATT_HEAD_DIM, (h + 1) * ATT_HEAD_DIM)
            o_ref[:, cs] = (o[hh * BLOCK:(hh + 1) * BLOCK] * jax.nn.sigmoid(gate_ref[:, cs].astype(F32))).astype(o_ref.dtype)


def _attention(cols, sink, cos2, sin2):
    T = cols.shape[0]
    nb = T // BLOCK
    prev = lambda i: (jnp.maximum(i - 1, 0), 0)
    cur = lambda i: (i, 0)
    nxt = lambda i: (jnp.minimum(i + 1, nb - 1), 0)
    kcb, vcb = CB_AK * LANES // KV_COLS, CB_AV * LANES // KV_COLS
    col = lambda f, cb: (lambda i: (f(i)[0], cb))
    tab = pl.BlockSpec
    return pl.pallas_call(
        functools.partial(_attn_kernel, seq=T),
        grid=(nb,),
        in_specs=[
            pl.BlockSpec(memory_space=pltpu.SMEM),
            pl.BlockSpec((BLOCK, D_MODEL), col(cur, CB_Q // 16)),
            pl.BlockSpec((BLOCK, KV_COLS), col(prev, kcb)),
            pl.BlockSpec((BLOCK, KV_COLS), col(cur, kcb)),
            pl.BlockSpec((BLOCK, KV_COLS), col(nxt, kcb)),
            pl.BlockSpec((BLOCK, KV_COLS), col(prev, vcb)),
            pl.BlockSpec((BLOCK, KV_COLS), col(cur, vcb)),
            pl.BlockSpec((BLOCK, KV_COLS), col(nxt, vcb)),
            pl.BlockSpec((BLOCK, D_MODEL), col(cur, CB_GA // 16)),
            tab((BLOCK, LANES), cur), tab((BLOCK, LANES), cur),
            tab((BLOCK, LANES), prev), tab((BLOCK, LANES), prev),
            tab((BLOCK, LANES), nxt), tab((BLOCK, LANES), nxt),
        ],
        out_specs=pl.BlockSpec((BLOCK, D_MODEL), cur),
        out_shape=jax.ShapeDtypeStruct((T, D_MODEL), BF16),
        compiler_params=_cparams(("parallel",)),
        name="attention",
    )(sink, cols, cols, cols, cols, cols, cols, cols, cols, cos2, sin2, cos2, sin2, cos2, sin2)


def _head_sum(x, ones_bd):
    return _mm(x, ones_bd)


def _rkv_kernel(x_ref, xp_ref, xq_ref, gain_ref, w_ref, lo_ref, lop_ref, lon_ref, mu_ref, mul_ref,
                w0f_ref, w0b_ref, a0f_ref, a0b_ref, kk_ref, ka_ref, rk_ref,
                wuf_ref, wub_ref, auf_ref, aub_ref, gup_ref, bd_ref,
                r_o, v_o, kk_o, lwf_o, bf_o, kf_o, lwb_o, bb_o, kb_o, g_o, bonus_o,
                xs_ref, xh_ref, act_ref):
    i = pl.program_id(0)
    first = i == 0
    last = i == pl.num_programs(0) - 1
    tm = x_ref.shape[0]
    P = LORA_PAD

    def normed(x):
        return (x * lax.rsqrt(jnp.mean(x * x, axis=-1, keepdims=True) + NORM_EPS) * gain_ref[...]).astype(BF16)

    def shift(c, prow, nrow, mu):
        row = lax.broadcasted_iota(jnp.int32, c.shape, 0)
        prev = jnp.where(row == 0, prow, pltpu.roll(c, 1, 0))
        nxt = jnp.where(row == tm - 1, nrow, pltpu.roll(c, tm - 1, 0))
        return c * (1.0 - mu) + (prev + nxt) * (0.5 * mu)

    @pl.when(pl.program_id(1) == 0)
    def _():
        xs_ref[...] = normed(x_ref[...])
        xh_ref[...] = normed(jnp.concatenate([xp_ref[...], xq_ref[...]], axis=0))
        lo = shift(lo_ref[...].astype(F32),
                   jnp.where(first, 0.0, lop_ref[PACKED_SUBLANES - 1:PACKED_SUBLANES, :].astype(F32)),
                   jnp.where(last, 0.0, lon_ref[0:1, :].astype(F32)), mul_ref[...])
        act_ref[:, 0:2 * P] = jnp.tanh(lo[:, 0:2 * P]).astype(BF16)
        act_ref[:, 2 * P:4 * P] = lo[:, 2 * P:4 * P].astype(BF16)
        act_ref[:, 4 * P:] = jax.nn.sigmoid(lo[:, 4 * P:4 * P + GATE_LORA]).astype(BF16)

    ones_bd = bd_ref[...]
    c_all = jnp.dot(xs_ref[...], w_ref[...], preferred_element_type=F32)
    ch = jnp.dot(xh_ref[...], w_ref[...], preferred_element_type=F32)
    prow_all = jnp.where(first, 0.0, ch[SUBLANES - 1:SUBLANES])
    nrow_all = jnp.where(last, 0.0, ch[SUBLANES:SUBLANES + 1])
    mu_all = mu_ref[...]
    for p in range(RKV_PAIRS_PER_TILE):
        ps = slice(p * LANES, (p + 1) * LANES)
        cols = [slice((3 * p + q) * LANES, (3 * p + q + 1) * LANES) for q in range(3)]
        r, k, v = [shift(c_all[:, cs], prow_all[:, cs], nrow_all[:, cs], mu_all[:, cs]) for cs in cols]

        kk = k * kk_ref[:, ps]
        kk = kk * lax.rsqrt(jnp.maximum(_head_sum(kk * kk, ones_bd), 1e-24))
        k_a = ka_ref[:, ps]

        def direction(wd, ad, w0, wu, a0, au):
            lw = -DECAY_SCALE * jax.nn.sigmoid(w0 + _mm(wd, wu))
            a = jax.nn.sigmoid(a0 + _mm(ad, au))
            k_mod = k * (1.0 + (a - 1.0) * k_a)
            return lw, kk * a, k_mod

        lwf, bf, kf = direction(act_ref[:, 0:P], act_ref[:, 2 * P:3 * P],
                                w0f_ref[:, ps], wuf_ref[:, ps], a0f_ref[:, ps], auf_ref[:, ps])
        lwb, bb, kb = direction(act_ref[:, P:2 * P], act_ref[:, 3 * P:4 * P],
                                w0b_ref[:, ps], wub_ref[:, ps], a0b_ref[:, ps], aub_ref[:, ps])
        g = _mm(act_ref[:, 4 * P:], gup_ref[:, ps])
        bonus = _head_sum(r * (0.5 * (kf + kb)) * rk_ref[:, ps], ones_bd) * v

        for ref, val in ((r_o, r), (v_o, v), (kk_o, kk), (lwf_o, lwf), (bf_o, bf), (kf_o, kf),
                         (lwb_o, lwb), (bb_o, bb), (kb_o, kb), (g_o, g), (bonus_o, bonus)):
            ref[:, ps] = val.astype(ref.dtype)


def _rwkv_project(x, gain, w_rkv, cols, mu_rkv, mu_lora, vecs, w_up_f, w_up_b, a_up_f, a_up_b, g_up, ones_bd, tm=512):
    T = x.shape[0]
    width = RKV_PAIRS_PER_TILE * LANES
    tile = 3 * width
    xb, nxb = tm // SUBLANES, T // SUBLANES
    lb, nlb = tm // PACKED_SUBLANES, T // PACKED_SUBLANES
    lcb = CB_LORA * LANES // LORA_COLS
    vec = pl.BlockSpec((1, width), lambda i, j: (0, j))
    up = lambda rows: pl.BlockSpec((rows, width), lambda i, j: (0, j))
    in_specs = [
        pl.BlockSpec((tm, D_MODEL), lambda i, j: (i, 0)),
        pl.BlockSpec((SUBLANES, D_MODEL), lambda i, j: (jnp.maximum(i * xb - 1, 0), 0)),
        pl.BlockSpec((SUBLANES, D_MODEL), lambda i, j: (jnp.minimum((i + 1) * xb, nxb - 1), 0)),
        pl.BlockSpec((1, D_MODEL), lambda i, j: (0, 0)),
        pl.BlockSpec((D_MODEL, tile), lambda i, j: (0, j)),
        pl.BlockSpec((tm, LORA_COLS), lambda i, j: (i, lcb)),
        pl.BlockSpec((PACKED_SUBLANES, LORA_COLS), lambda i, j: (jnp.maximum(i * lb - 1, 0), lcb)),
        pl.BlockSpec((PACKED_SUBLANES, LORA_COLS), lambda i, j: (jnp.minimum((i + 1) * lb, nlb - 1), lcb)),
        pl.BlockSpec((1, tile), lambda i, j: (0, j)),
        pl.BlockSpec((1, LORA_COLS), lambda i, j: (0, 0)),
    ]
    in_specs += [vec] * 7
    in_specs += [up(LORA_PAD)] * 4 + [up(GATE_LORA)]
    in_specs += [pl.BlockSpec((LANES, LANES), lambda i, j: (0, 0))]
    dtypes = [BF16, BF16, BF16, F32, BF16, BF16, F32, BF16, BF16, BF16, BF16]
    return pl.pallas_call(
        _rkv_kernel,
        grid=(T // tm, RWKV_DIM // width),
        in_specs=in_specs,
        out_specs=[pl.BlockSpec((tm, width), lambda i, j: (i, j))] * 11,
        out_shape=[jax.ShapeDtypeStruct((T, RWKV_DIM), dt) for dt in dtypes],
        scratch_shapes=[pltpu.VMEM((tm, D_MODEL), BF16), pltpu.VMEM((2 * SUBLANES, D_MODEL), BF16),
                        pltpu.VMEM((tm, 4 * LORA_PAD + GATE_LORA), BF16)],
        compiler_params=_cparams(("parallel", "arbitrary")),
        name="rwkv_project",
    )(x, x, x, gain, w_rkv, cols, cols, cols, mu_rkv, mu_lora, *vecs, w_up_f, w_up_b, a_up_f, a_up_b, g_up, ones_bd)


def _tri_inverse_all(mats, upper):
    L = mats[0].shape[0]
    row = lax.broadcasted_iota(jnp.int32, (L, L), 0)
    col = lax.broadcasted_iota(jnp.int32, (L, L), 1)
    hrow = lax.broadcasted_iota(jnp.int32, (L // 2, L), 0)
    hcol = lax.broadcasted_iota(jnp.int32, (L // 2, L), 1)
    same = lambda sh: (row >> sh) == (col >> sh)
    eye = jnp.where(row == col, 1.0, 0.0)
    ds = [eye + jnp.where(same(1), a, 0.0) for a in mats]
    sh = 1
    while (1 << sh) < L:
        m = 1 << sh
        dbs = [d.astype(BF16) for d in ds]
        if m < SUBLANES:
            level = same(sh + 1) & jnp.logical_not(same(sh))
            ts = [_mm(jnp.where(level, a, 0.0), db) for a, db in zip(mats, dbs)]
            ds = [d + _mm(db, t) for d, db, t in zip(ds, dbs, ts)]
        else:
            n_half = L // (2 * m)
            part = [slice(q * m, (q + 1) * m) for q in range(n_half)]
            blocks = lambda x: [x[q * m:(q + 1) * m] for q in range(2 * n_half)]
            zero = jnp.zeros((m, L), F32)
            act = lambda x, up: jnp.concatenate(blocks(x)[(0 if up else 1)::2], axis=0)
            level = {False: (hcol >> sh) == 2 * (hrow >> sh), True: (hcol >> sh) == 2 * (hrow >> sh) + 1}

            def spread(x, up, rest=None):
                out = []
                for q in range(n_half):
                    other = zero if rest is None else rest[2 * q + (1 if up else 0)]
                    out += [x[part[q]], other] if up else [other, x[part[q]]]
                return jnp.concatenate(out, axis=0)

            ts = [_mm(jnp.where(level[up], act(a, up), 0.0), db) for a, db, up in zip(mats, dbs, upper)]
            upd = [_mm(act(d, up), spread(t, up)) for d, t, up in zip(ds, ts, upper)]
            ds = [spread(act(d, up) + u, up, rest=blocks(d)) for d, u, up in zip(ds, upd, upper)]
        sh += 1
    return ds


def _chunk_all(insts):
    L = insts[0][0].shape[0]
    n_inst = len(insts)
    row = lax.broadcasted_iota(jnp.int32, (L, L), 0)
    col = lax.broadcasted_iota(jnp.int32, (L, L), 1)
    incl = {False: col <= row, True: col >= row}
    strict = {False: col < row, True: col > row}
    ones = {rev: jnp.where(incl[rev], 1.0, 0.0) for rev in (False, True)}
    incl2 = {rev: jnp.concatenate([incl[rev]] * 2, axis=1) for rev in (False, True)}
    strict2 = {rev: jnp.concatenate([strict[rev]] * 2, axis=1) for rev in (False, True)}
    lane = lax.broadcasted_iota(jnp.int32, (1, LANES), 1)
    h0 = lane < RWKV_HEAD_DIM
    srow = lax.broadcasted_iota(jnp.int32, (LANES, LANES), 0)
    scol = lax.broadcasted_iota(jnp.int32, (LANES, LANES), 1)
    same_head = (srow >= RWKV_HEAD_DIM) == (scol >= RWKV_HEAD_DIM)

    cums = [_mm_exact_lhs(ones[inst[7]], inst[3], 2) for inst in insts]
    pre = []
    for (r, v, kk, lw, b, k, s, rev), cum in zip(insts, cums):
        tot = cum[0:1] if rev else cum[L - 1:L]
        mid = L // 2 if rev else L // 2 - 1
        rho = cum[mid:mid + 1]
        e1 = jnp.exp(cum - rho)
        e2 = jnp.exp(rho - cum)
        er = jnp.exp(rho)
        et = jnp.exp(tot - rho)
        a_t = -kk * e1 * jnp.exp(-lw)
        r_t = r * e1
        b_t = b * e2
        k_t = k * e2
        zero = jnp.zeros_like(a_t)
        pre.append(dict(
            lhs=jnp.concatenate([jnp.where(h0, a_t, zero), jnp.where(h0, zero, a_t),
                                 jnp.where(h0, r_t, zero), jnp.where(h0, zero, r_t)], axis=0).astype(BF16),
            rhs=jnp.concatenate([b_t, k_t], axis=0).astype(BF16),
            ar_abs=jnp.concatenate([a_t * er, r_t * er], axis=0).astype(BF16),
            w=jnp.concatenate([b_t * et, k_t * et], axis=0).astype(BF16),
            gamma=er * et, v=v.astype(BF16)))
    grams = [_mm(p["lhs"], p["rhs"], NT) for p in pre]
    a_ab, a_k, r_bk = [], [], []
    for inst, gram in zip(insts, grams):
        rev = inst[7]
        a2 = [jnp.where(strict2[rev], gram[hh * L:(hh + 1) * L], 0.0) for hh in range(2)]
        r2 = [jnp.where(incl2[rev], gram[(2 + hh) * L:(3 + hh) * L], 0.0) for hh in range(2)]
        a_ab += [a2[0][:, :L], a2[1][:, :L]]
        a_k.append(jnp.concatenate([a2[0][:, L:], a2[1][:, L:]], axis=0).astype(BF16))
        r_bk.append(jnp.concatenate(r2, axis=0).astype(BF16))
    t_inv = _tri_inverse_all(a_ab, [insts[n][7] for n in range(n_inst) for hh in range(2)])
    pick = lambda t2: jnp.where(h0, t2[:L], t2[L:])
    xs = [_mm(p["ar_abs"], inst[6], NT) for inst, p in zip(insts, pre)]
    akv = [_mm(a_k[n], pre[n]["v"]) for n in range(n_inst)]
    x = [xs[n][:L] + pick(akv[n]) for n in range(n_inst)]
    u = [pick(_mm(jnp.concatenate([t_inv[2 * n], t_inv[2 * n + 1]], axis=0), x[n])) for n in range(n_inst)]
    z = [jnp.concatenate([u[n].astype(BF16), pre[n]["v"]], axis=0) for n in range(n_inst)]
    out = []
    for n, inst in enumerate(insts):
        y = xs[n][L:] + pick(_mm(r_bk[n], z[n]))
        s_new = inst[6] * pre[n]["gamma"] + jnp.where(same_head, _mm(z[n], pre[n]["w"], TN), 0.0)
        out.append((y, s_new))
    return out


def _scan_kernel(rf_ref, vf_ref, kkf_ref, lwf_ref, bf_ref, kf_ref,
                 rb_ref, vb_ref, kkb_ref, lwb_ref, bb_ref, kb_ref,
                 yf_ref, yb_ref, s_ref):
    @pl.when(pl.program_id(1) == 0)
    def _():
        s_ref[...] = jnp.zeros_like(s_ref)

    n_pairs = s_ref.shape[1]
    insts = []
    for p in range(n_pairs):
        cs = slice(p * LANES, (p + 1) * LANES)
        insts.append(tuple(ref[:, cs].astype(F32) for ref in (rf_ref, vf_ref, kkf_ref, lwf_ref, bf_ref, kf_ref))
                     + (s_ref[0, p], False))
        insts.append(tuple(ref[:, cs].astype(F32) for ref in (rb_ref, vb_ref, kkb_ref, lwb_ref, bb_ref, kb_ref))
                     + (s_ref[1, p], True))
    res = _chunk_all(insts)
    for p in range(n_pairs):
        cs = slice(p * LANES, (p + 1) * LANES)
        yf_ref[:, cs], s_ref[0, p] = res[2 * p]
        yb_ref[:, cs], s_ref[1, p] = res[2 * p + 1]


def _rwkv_scan(r, v, kk, lwf, bf, kf, lwb, bb, kb, pairs_per_step=8):
    T = r.shape[0]
    nc = T // CHUNK
    width = pairs_per_step * LANES
    fwd = pl.BlockSpec((CHUNK, width), lambda p, c: (c, p))
    bwd = pl.BlockSpec((CHUNK, width), lambda p, c: (nc - 1 - c, p))
    out = jax.ShapeDtypeStruct((T, RWKV_DIM), F32)
    return pl.pallas_call(
        _scan_kernel,
        grid=(N_PAIRS // pairs_per_step, nc),
        in_specs=[fwd] * 6 + [bwd] * 6,
        out_specs=[fwd, bwd],
        out_shape=[out, out],
        scratch_shapes=[pltpu.VMEM((2, pairs_per_step, LANES, LANES), F32)],
        compiler_params=_cparams(("parallel", "arbitrary")),
        name="rwkv_scan",
    )(r, v, kk, lwf, bf, kf, r, v, kk, lwb, bb, kb)


def _rms(x, gain):
    return x * lax.rsqrt(jnp.mean(x * x, axis=-1, keepdims=True) + NORM_EPS) * gain


def _outproj_kernel(yf_ref, yb_ref, bonus_ref, g_ref, att_ref, gr_ref, gain_ref, bias_ref, bd_ref,
                    w_ref, x_ref, gpost_ref, gpre_ref, h_ref, hn_ref, m_ref):
    ones_bd = bd_ref[...]
    inv_n = 1.0 / RWKV_HEAD_DIM
    for j in range(N_PAIRS):
        cs = slice(j * LANES, (j + 1) * LANES)
        y = yf_ref[:, cs] + yb_ref[:, cs]
        mean = _head_sum(y, ones_bd) * inv_n
        d = y - mean
        var = _head_sum(d * d, ones_bd) * inv_n
        yn = d * lax.rsqrt(var + GN_EPS) * gain_ref[:, cs] + bias_ref[:, cs]
        o_rwkv = (yn + bonus_ref[:, cs].astype(F32)) * g_ref[:, cs].astype(F32)
        merged = att_ref[:, cs].astype(F32) + jax.nn.sigmoid(gr_ref[:, cs].astype(F32)) * o_rwkv
        m_ref[:, cs] = merged.astype(BF16)
    mix = jnp.dot(m_ref[...], w_ref[...], preferred_element_type=F32)
    h = x_ref[...] + _rms(mix, gpost_ref[...])
    h_ref[...] = h
    hn_ref[...] = _rms(h, gpre_ref[...]).astype(BF16)


def _outproj(yf, yb, bonus, g, att, cols, gain, bias, ones_bd, w_out, x, g_post, g_pre, tm=256):
    T = x.shape[0]
    row = pl.BlockSpec((tm, D_MODEL), lambda i: (i, 0))
    vec = pl.BlockSpec((1, D_MODEL), lambda i: (0, 0))
    return pl.pallas_call(
        _outproj_kernel,
        grid=(T // tm,),
        in_specs=[row, row, row, row, row,
                  pl.BlockSpec((tm, D_MODEL), lambda i: (i, CB_GR // 16)),
                  vec, vec, pl.BlockSpec((LANES, LANES), lambda i: (0, 0)),
                  pl.BlockSpec((D_MODEL, D_MODEL), lambda i: (0, 0), pipeline_mode=pl.Buffered(1)),
                  row, vec, vec],
        out_specs=[row, row],
        out_shape=[jax.ShapeDtypeStruct((T, D_MODEL), F32), jax.ShapeDtypeStruct((T, D_MODEL), BF16)],
        scratch_shapes=[pltpu.VMEM((tm, D_MODEL), BF16)],
        compiler_params=_cparams(("parallel",)),
        name="outproj",
    )(yf, yb, bonus, g, att, cols, gain, bias, ones_bd, w_out, x, g_post, g_pre)


def _ffn_kernel(hn_ref, wu_ref, wd_ref, h_ref, g_ref, o_ref, acc_ref):
    j = pl.program_id(1)

    @pl.when(j == 0)
    def _():
        acc_ref[...] = jnp.zeros_like(acc_ref)

    up = jnp.dot(hn_ref[...], wu_ref[...], preferred_element_type=F32)
    act = jnp.square(jnp.maximum(up, 0.0)).astype(BF16)
    acc_ref[...] += jnp.dot(act, wd_ref[...], preferred_element_type=F32)

    @pl.when(j == pl.num_programs(1) - 1)
    def _():
        o_ref[...] = h_ref[...] + _rms(acc_ref[...], g_ref[...])


def _ffn(hn, w_up, w_down, h, gain, tm=1024, tf=512):
    T = h.shape[0]
    row = pl.BlockSpec((tm, D_MODEL), lambda i, j: (i, 0))
    row_once = pl.BlockSpec((tm, D_MODEL), lambda i, j: (i, 0), pipeline_mode=pl.Buffered(1))
    return pl.pallas_call(
        _ffn_kernel,
        grid=(T // tm, D_FF // tf),
        in_specs=[row,
                  pl.BlockSpec((D_MODEL, tf), lambda i, j: (0, j)),
                  pl.BlockSpec((tf, D_MODEL), lambda i, j: (j, 0)),
                  row_once,
                  pl.BlockSpec((1, D_MODEL), lambda i, j: (0, 0))],
        out_specs=row_once,
        out_shape=jax.ShapeDtypeStruct((T, D_MODEL), F32),
        scratch_shapes=[pltpu.VMEM((tm, D_MODEL), F32)],
        compiler_params=_cparams(("parallel", "arbitrary")),
        name="ffn",
    )(hn, w_up, w_down, h, gain)


def _pad_cols(t, n):
    return jnp.pad(t, ((0, 0), (0, n - t.shape[1])))


def _pad_rows(t, n):
    return jnp.pad(t, ((0, n - t.shape[0]), (0, 0)))


def _split_cols(t, sizes):
    idx = [int(i) for i in np.cumsum(sizes)[:-1]]
    return jnp.split(t, idx, axis=-1)


SHIFT_SIZES = [RWKV_DIM] * 3 + [DECAY_LORA] * 2 + [ICLR_LORA] * 2 + [GATE_LORA]


def _interleave_rkv(r, k, v):
    rows = r.shape[0]
    return jnp.stack([t.reshape(rows, N_PAIRS, LANES) for t in (r, k, v)], axis=2).reshape(rows, 3 * RWKV_DIM)


def _pad_lora(wdf, wdb, adf, adb, gd):
    return _pad_cols(jnp.concatenate([_pad_cols(p, LORA_PAD) for p in (wdf, wdb, adf, adb)] + [gd], axis=1), LORA_COLS)


def _split_in_cols(t):
    q, ak, av, rw, ga, gr = _split_cols(t, [D_MODEL, KV_COLS, KV_COLS, sum(SHIFT_SIZES), D_MODEL, D_MODEL])
    r, k, v, wdf, wdb, adf, adb, gd = _split_cols(rw, SHIFT_SIZES)
    return jnp.concatenate([q, ga, gr, ak, av, _pad_lora(wdf, wdb, adf, adb, gd)], axis=1), _interleave_rkv(r, k, v)


def _rope_tables(T):
    pos = jnp.arange(T, dtype=F32)
    inv_freq = ROPE_THETA ** (-jnp.arange(0, ATT_HEAD_DIM, 2, dtype=F32) / ATT_HEAD_DIM)
    ang = pos[:, None] * inv_freq[None, :]
    cos, sin = jnp.cos(ang), jnp.sin(ang)
    return jnp.concatenate([cos, cos], axis=1), jnp.concatenate([-sin, sin], axis=1)


def kernel(x, norm_pre_mix, w_in, mu_shift, attn_sink, w0_fwd, w_up_fwd, w0_bwd, w_up_bwd, a0_fwd, a_up_fwd, a0_bwd, a_up_bwd, g_up, k_k, k_a, r_k, ln_x_gain, ln_x_bias, w_out, norm_post_mix, norm_pre_ffn, w_ffn_up, w_ffn_down, norm_post_ffn):
    B, T, _ = x.shape
    depth = w_in.shape[0]
    cos2, sin2 = _rope_tables(T)
    lane = np.arange(LANES)
    ones_bd = jnp.asarray((lane[:, None] // RWKV_HEAD_DIM) == (lane[None, :] // RWKV_HEAD_DIM), BF16)
    row = lambda t: t.reshape(1, -1)
    outs = []
    for bi in range(B):
        h = x[bi]
        for l in range(depth):
            w_cols, w_rkv = _split_in_cols(w_in[l].astype(BF16))
            mr, mk, mv, m1, m2, m3, m4, mg = _split_cols(mu_shift[l].reshape(1, -1), SHIFT_SIZES)
            mu_rkv = _interleave_rkv(mr, mk, mv)
            mu_lora = _pad_lora(m1, m2, m3, m4, mg)
            cols = _inproj(h, row(norm_pre_mix[l]), w_cols)
            att = _attention(cols, attn_sink[l], cos2, sin2)
            vecs = [row(t[l]) for t in (w0_fwd, w0_bwd, a0_fwd, a0_bwd, k_k, k_a, r_k)]
            r, v, kk, lwf, bf, kf, lwb, bb, kb, g, bonus = _rwkv_project(
                h, row(norm_pre_mix[l]), w_rkv, cols, mu_rkv, mu_lora, vecs,
                *[_pad_rows(t[l], LORA_PAD).astype(BF16) for t in (w_up_fwd, w_up_bwd, a_up_fwd, a_up_bwd)],
                g_up[l].astype(BF16), ones_bd)
            yf, yb = _rwkv_scan(r, v, kk, lwf, bf, kf, lwb, bb, kb)
            h, hn = _outproj(yf, yb, bonus, g, att, cols, row(ln_x_gain[l]), row(ln_x_bias[l]), ones_bd,
                             w_out[l].astype(BF16), h, row(norm_post_mix[l]), row(norm_pre_ffn[l]))
            h = _ffn(hn, w_ffn_up[l].astype(BF16), w_ffn_down[l].astype(BF16), h, row(norm_post_ffn[l]))
        outs.append(h)
    return jnp.stack(outs, axis=0)
```

```python
import functools

import jax
import jax.numpy as jnp
import numpy as np
from jax import lax
from jax.experimental import pallas as pl
from jax.experimental.pallas import tpu as pltpu

F32 = jnp.float32
BF16 = jnp.bfloat16
LANES = 128
SUBLANES = 8
PACKED_SUBLANES = 16

D_MODEL = 2048
ATT_HEAD_DIM = 128
ATT_HEADS = 16
ATT_KV_HEADS = 4
ATT_GROUP = 4
WINDOW = 128
BLOCK = 128
ROPE_THETA = 10000.0
RWKV_HEAD_DIM = 64
RWKV_DIM = 2048
N_PAIRS = RWKV_DIM // LANES
DECAY_LORA = 96
ICLR_LORA = 96
GATE_LORA = 256
LORA_PAD = 128
LORA_COLS = 1024
D_FF = 4 * D_MODEL
NORM_EPS = 1e-6
GN_EPS = 64e-5
MASK_VALUE = -1e30
DECAY_SCALE = float(np.exp(-0.5))
KV_COLS = ATT_KV_HEADS * ATT_HEAD_DIM

CB_Q, CB_GA, CB_GR, CB_R, CB_K, CB_V, CB_AK, CB_AV, CB_LORA = 0, 16, 32, 48, 64, 80, 96, 100, 104

CHUNK = 128
VMEM_LIMIT = 56 * 1024 * 1024

NN = (((1,), (0,)), ((), ()))
NT = (((1,), (1,)), ((), ()))
TN = (((0,), (0,)), ((), ()))


def _mm(a, b, dims=NN):
    return lax.dot_general(a.astype(BF16), b.astype(BF16), dims, preferred_element_type=F32)


def _split_bf16(x, parts):
    out = []
    for _ in range(parts - 1):
        hi = x.astype(BF16)
        out.append(hi)
        x = x - hi.astype(F32)
    out.append(x.astype(BF16))
    return out


def _mm_exact_lhs(a, b, parts):
    a = a.astype(BF16)
    acc = None
    for term in _split_bf16(b, parts):
        p = lax.dot_general(a, term, NN, preferred_element_type=F32)
        acc = p if acc is None else acc + p
    return acc


def _cparams(sem):
    return pltpu.CompilerParams(dimension_semantics=sem, vmem_limit_bytes=VMEM_LIMIT)


def _inproj_kernel(x_ref, g_ref, w_ref, o_ref, xn_ref):
    @pl.when(pl.program_id(1) == 0)
    def _():
        x = x_ref[...]
        ms = jnp.mean(x * x, axis=-1, keepdims=True)
        xn_ref[...] = (x * lax.rsqrt(ms + NORM_EPS) * g_ref[...]).astype(BF16)

    o_ref[...] = jnp.dot(xn_ref[...], w_ref[...], preferred_element_type=F32).astype(o_ref.dtype)


def _inproj(x, gain, w, tm=1024, tn=2048):
    T = x.shape[0]
    n = w.shape[1]
    return pl.pallas_call(
        _inproj_kernel,
        grid=(T // tm, n // tn),
        in_specs=[
            pl.BlockSpec((tm, D_MODEL), lambda i, j: (i, 0)),
            pl.BlockSpec((1, D_MODEL), lambda i, j: (0, 0)),
            pl.BlockSpec((D_MODEL, tn), lambda i, j: (0, j)),
        ],
        out_specs=pl.BlockSpec((tm, tn), lambda i, j: (i, j)),
        out_shape=jax.ShapeDtypeStruct((T, n), BF16),
        scratch_shapes=[pltpu.VMEM((tm, D_MODEL), BF16)],
        compiler_params=_cparams(("parallel", "arbitrary")),
        name="inproj",
    )(x, gain, w)


def _rope(x, c, s):
    return x * c + pltpu.roll(x, ATT_HEAD_DIM // 2, 1) * s


def _attn_kernel(sink_ref, q_ref, kp_ref, kc_ref, kn_ref, vp_ref, vc_ref, vn_ref, gate_ref,
                 cc_ref, sc_ref, cp_ref, sp_ref, cn_ref, sn_ref, o_ref, *, seq):
    i = pl.program_id(0)
    cc, sc = cc_ref[...], sc_ref[...]
    cp, sp = cp_ref[...], sp_ref[...]
    cn, sn = cn_ref[...], sn_ref[...]
    qi = lax.broadcasted_iota(jnp.int32, (BLOCK, 3 * BLOCK), 0)
    sj = lax.broadcasted_iota(jnp.int32, (BLOCK, 3 * BLOCK), 1)
    kpos = (i - 1) * BLOCK + sj
    valid = (jnp.abs(sj - BLOCK - qi) <= WINDOW) & (kpos >= 0) & (kpos < seq)
    valid4 = jnp.concatenate([valid] * ATT_GROUP, axis=0)
    rowg = lax.broadcasted_iota(jnp.int32, (ATT_GROUP * BLOCK, 1), 0) // BLOCK
    scale = ATT_HEAD_DIM ** -0.5
    for g in range(ATT_KV_HEADS):
        ks = slice(g * ATT_HEAD_DIM, (g + 1) * ATT_HEAD_DIM)
        kw = jnp.concatenate([_rope(kp_ref[:, ks].astype(F32), cp, sp), _rope(kc_ref[:, ks].astype(F32), cc, sc),
                              _rope(kn_ref[:, ks].astype(F32), cn, sn)], axis=0)
        vw = jnp.concatenate([vp_ref[:, ks], vc_ref[:, ks], vn_ref[:, ks]], axis=0)
        heads = [g * ATT_GROUP + hh for hh in range(ATT_GROUP)]
        q4 = jnp.concatenate(
            [_rope(q_ref[:, h * ATT_HEAD_DIM:(h + 1) * ATT_HEAD_DIM].astype(F32), cc, sc) for h in heads],
            axis=0) * scale
        s = _mm(q4, kw, NT)
        s = jnp.where(valid4, s, MASK_VALUE)
        sink = jnp.zeros((ATT_GROUP * BLOCK, 1), F32)
        for hh, h in enumerate(heads):
            sink = jnp.where(rowg == hh, sink_ref[h], sink)
        m = jnp.maximum(jnp.max(s, axis=-1, keepdims=True), sink)
        p = jnp.exp(s - m)
        den = jnp.sum(p, axis=-1, keepdims=True) + jnp.exp(sink - m)
        o = _mm(p, vw) / den
        for hh, h in enumerate(heads):
            cs = slice(h * ATT_HEAD_DIM, (h + 1) * ATT_HEAD_DIM)
            o_ref[:, cs] = (o[hh * BLOCK:(hh + 1) * BLOCK] * jax.nn.sigmoid(gate_ref[:, cs].astype(F32))).astype(o_ref.dtype)


def _attention(cols, sink, cos2, sin2):
    T = cols.shape[0]
    nb = T // BLOCK
    prev = lambda i: (jnp.maximum(i - 1, 0), 0)
    cur = lambda i: (i, 0)
    nxt = lambda i: (jnp.minimum(i + 1, nb - 1), 0)
    kcb, vcb = CB_AK * LANES // KV_COLS, CB_AV * LANES // KV_COLS
    col = lambda f, cb: (lambda i: (f(i)[0], cb))
    tab = pl.BlockSpec
    return pl.pallas_call(
        functools.partial(_attn_kernel, seq=T),
        grid=(nb,),
        in_specs=[
            pl.BlockSpec(memory_space=pltpu.SMEM),
            pl.BlockSpec((BLOCK, D_MODEL), col(cur, CB_Q // 16)),
            pl.BlockSpec((BLOCK, KV_COLS), col(prev, kcb)),
            pl.BlockSpec((BLOCK, KV_COLS), col(cur, kcb)),
            pl.BlockSpec((BLOCK, KV_COLS), col(nxt, kcb)),
            pl.BlockSpec((BLOCK, KV_COLS), col(prev, vcb)),
            pl.BlockSpec((BLOCK, KV_COLS), col(cur, vcb)),
            pl.BlockSpec((BLOCK, KV_COLS), col(nxt, vcb)),
            pl.BlockSpec((BLOCK, D_MODEL), col(cur, CB_GA // 16)),
            tab((BLOCK, LANES), cur), tab((BLOCK, LANES), cur),
            tab((BLOCK, LANES), prev), tab((BLOCK, LANES), prev),
            tab((BLOCK, LANES), nxt), tab((BLOCK, LANES), nxt),
        ],
        out_specs=pl.BlockSpec((BLOCK, D_MODEL), cur),
        out_shape=jax.ShapeDtypeStruct((T, D_MODEL), BF16),
        compiler_params=_cparams(("parallel",)),
        name="attention",
    )(sink, cols, cols, cols, cols, cols, cols, cols, cols, cos2, sin2, cos2, sin2, cos2, sin2)


def _head_sum(x, ones_bd):
    return _mm(x, ones_bd)


def _prep_kernel(r_ref, rp_ref, rn_ref, k_ref, kp_ref, kn_ref, v_ref, vp_ref, vn_ref,
                 lo_ref, lop_ref, lon_ref, mur_ref, muk_ref, muv_ref, mul_ref,
                 w0f_ref, w0b_ref, a0f_ref, a0b_ref, kk_ref, ka_ref, rk_ref,
                 wuf_ref, wub_ref, auf_ref, aub_ref, gup_ref, bd_ref,
                 r_o, v_o, kk_o, lwf_o, bf_o, kf_o, lwb_o, bb_o, kb_o, g_o, bonus_o, act_ref):
    i = pl.program_id(0)
    first = i == 0
    last = i == pl.num_programs(0) - 1
    tm = r_ref.shape[0]

    def shift(c_ref, p_ref, n_ref, mu):
        c = c_ref[...].astype(F32)
        row = lax.broadcasted_iota(jnp.int32, c.shape, 0)
        prow = jnp.where(first, 0.0, p_ref[PACKED_SUBLANES - 1:PACKED_SUBLANES, :].astype(F32))
        nrow = jnp.where(last, 0.0, n_ref[0:1, :].astype(F32))
        prev = jnp.where(row == 0, prow, pltpu.roll(c, 1, 0))
        nxt = jnp.where(row == tm - 1, nrow, pltpu.roll(c, tm - 1, 0))
        return c * (1.0 - mu) + (prev + nxt) * (0.5 * mu)

    r = shift(r_ref, rp_ref, rn_ref, mur_ref[...])
    k = shift(k_ref, kp_ref, kn_ref, muk_ref[...])
    v = shift(v_ref, vp_ref, vn_ref, muv_ref[...])
    P = LORA_PAD

    @pl.when(pl.program_id(1) == 0)
    def _():
        lo = shift(lo_ref, lop_ref, lon_ref, mul_ref[...])
        act_ref[:, 0:2 * P] = jnp.tanh(lo[:, 0:2 * P]).astype(BF16)
        act_ref[:, 2 * P:4 * P] = lo[:, 2 * P:4 * P].astype(BF16)
        act_ref[:, 4 * P:] = jax.nn.sigmoid(lo[:, 4 * P:4 * P + GATE_LORA]).astype(BF16)

    ones_bd = bd_ref[...]

    kk = k * kk_ref[...]
    kk = kk * lax.rsqrt(jnp.maximum(_head_sum(kk * kk, ones_bd), 1e-24))
    k_a = ka_ref[...]

    def direction(wd, ad, w0, wu, a0, au):
        lw = -DECAY_SCALE * jax.nn.sigmoid(w0 + _mm(wd, wu))
        a = jax.nn.sigmoid(a0 + _mm(ad, au))
        k_mod = k * (1.0 + (a - 1.0) * k_a)
        return lw, kk * a, k_mod

    lwf, bf, kf = direction(act_ref[:, 0:P], act_ref[:, 2 * P:3 * P],
                            w0f_ref[...], wuf_ref[...], a0f_ref[...], auf_ref[...])
    lwb, bb, kb = direction(act_ref[:, P:2 * P], act_ref[:, 3 * P:4 * P],
                            w0b_ref[...], wub_ref[...], a0b_ref[...], aub_ref[...])
    g = _mm(act_ref[:, 4 * P:], gup_ref[...])
    bonus = _head_sum(r * (0.5 * (kf + kb)) * rk_ref[...], ones_bd) * v

    for ref, val in ((r_o, r), (v_o, v), (kk_o, kk), (lwf_o, lwf), (bf_o, bf), (kf_o, kf),
                     (lwb_o, lwb), (bb_o, bb), (kb_o, kb), (g_o, g), (bonus_o, bonus)):
        ref[...] = val.astype(ref.dtype)


def _rwkv_prep(cols, mu_rkv, mu_lora, vecs, w_up_f, w_up_b, a_up_f, a_up_b, g_up, ones_bd, tm=2048):
    T = cols.shape[0]
    nt = T // tm
    halo = PACKED_SUBLANES
    hb = tm // halo
    nhb = T // halo
    main = lambda cb: pl.BlockSpec((tm, LANES), lambda i, j: (i, cb + j))
    hprev = lambda cb: pl.BlockSpec((halo, LANES), lambda i, j: (jnp.maximum(i * hb - 1, 0), cb + j))
    hnext = lambda cb: pl.BlockSpec((halo, LANES), lambda i, j: (jnp.minimum((i + 1) * hb, nhb - 1), cb + j))
    lcb = CB_LORA * LANES // LORA_COLS
    vec = lambda off: pl.BlockSpec((1, LANES), lambda i, j: (0, off + j))
    up = lambda rows: pl.BlockSpec((rows, LANES), lambda i, j: (0, j))
    in_specs = []
    for cb in (CB_R, CB_K, CB_V):
        in_specs += [main(cb), hprev(cb), hnext(cb)]
    in_specs += [
        pl.BlockSpec((tm, LORA_COLS), lambda i, j: (i, lcb)),
        pl.BlockSpec((halo, LORA_COLS), lambda i, j: (jnp.maximum(i * hb - 1, 0), lcb)),
        pl.BlockSpec((halo, LORA_COLS), lambda i, j: (jnp.minimum((i + 1) * hb, nhb - 1), lcb)),
        vec(0), vec(N_PAIRS), vec(2 * N_PAIRS),
        pl.BlockSpec((1, LORA_COLS), lambda i, j: (0, 0)),
    ]
    in_specs += [vec(0)] * 7
    in_specs += [up(LORA_PAD)] * 4 + [up(GATE_LORA)]
    in_specs += [pl.BlockSpec((LANES, LANES), lambda i, j: (0, 0))]
    dtypes = [BF16, BF16, BF16, F32, BF16, BF16, F32, BF16, BF16, BF16, BF16]
    return pl.pallas_call(
        _prep_kernel,
        grid=(nt, N_PAIRS),
        in_specs=in_specs,
        out_specs=[pl.BlockSpec((tm, LANES), lambda i, j: (i, j))] * 11,
        out_shape=[jax.ShapeDtypeStruct((T, RWKV_DIM), dt) for dt in dtypes],
        scratch_shapes=[pltpu.VMEM((tm, 4 * LORA_PAD + GATE_LORA), BF16)],
        compiler_params=_cparams(("parallel", "arbitrary")),
        name="rwkv_prep",
    )(cols, cols, cols, cols, cols, cols, cols, cols, cols, cols, cols, cols,
      mu_rkv, mu_rkv, mu_rkv, mu_lora, *vecs, w_up_f, w_up_b, a_up_f, a_up_b, g_up, ones_bd)


def _tri_inverse_all(mats, upper):
    L = mats[0].shape[0]
    row = lax.broadcasted_iota(jnp.int32, (L, L), 0)
    col = lax.broadcasted_iota(jnp.int32, (L, L), 1)
    hrow = lax.broadcasted_iota(jnp.int32, (L // 2, L), 0)
    hcol = lax.broadcasted_iota(jnp.int32, (L // 2, L), 1)
    same = lambda sh: (row >> sh) == (col >> sh)
    eye = jnp.where(row == col, 1.0, 0.0)
    ds = [eye + jnp.where(same(1), a, 0.0) for a in mats]
    sh = 1
    while (1 << sh) < L:
        m = 1 << sh
        dbs = [d.astype(BF16) for d in ds]
        if m < SUBLANES:
            level = same(sh + 1) & jnp.logical_not(same(sh))
            ts = [_mm(jnp.where(level, a, 0.0), db) for a, db in zip(mats, dbs)]
            ds = [d + _mm(db, t) for d, db, t in zip(ds, dbs, ts)]
        else:
            n_half = L // (2 * m)
            part = [slice(q * m, (q + 1) * m) for q in range(n_half)]
            blocks = lambda x: [x[q * m:(q + 1) * m] for q in range(2 * n_half)]
            zero = jnp.zeros((m, L), F32)
            act = lambda x, up: jnp.concatenate(blocks(x)[(0 if up else 1)::2], axis=0)
            level = {False: (hcol >> sh) == 2 * (hrow >> sh), True: (hcol >> sh) == 2 * (hrow >> sh) + 1}

            def spread(x, up, rest=None):
                out = []
                for q in range(n_half):
                    other = zero if rest is None else rest[2 * q + (1 if up else 0)]
                    out += [x[part[q]], other] if up else [other, x[part[q]]]
                return jnp.concatenate(out, axis=0)

            ts = [_mm(jnp.where(level[up], act(a, up), 0.0), db) for a, db, up in zip(mats, dbs, upper)]
            upd = [_mm(act(d, up), spread(t, up)) for d, t, up in zip(ds, ts, upper)]
            ds = [spread(act(d, up) + u, up, rest=blocks(d)) for d, u, up in zip(ds, upd, upper)]
        sh += 1
    return ds


def _chunk_all(insts):
    L = insts[0][0].shape[0]
    n_inst = len(insts)
    row = lax.broadcasted_iota(jnp.int32, (L, L), 0)
    col = lax.broadcasted_iota(jnp.int32, (L, L), 1)
    incl = {False: col <= row, True: col >= row}
    strict = {False: col < row, True: col > row}
    ones = {rev: jnp.where(incl[rev], 1.0, 0.0) for rev in (False, True)}
    incl2 = {rev: jnp.concatenate([incl[rev]] * 2, axis=1) for rev in (False, True)}
    strict2 = {rev: jnp.concatenate([strict[rev]] * 2, axis=1) for rev in (False, True)}
    lane = lax.broadcasted_iota(jnp.int32, (1, LANES), 1)
    h0 = lane < RWKV_HEAD_DIM
    srow = lax.broadcasted_iota(jnp.int32, (LANES, LANES), 0)
    scol = lax.broadcasted_iota(jnp.int32, (LANES, LANES), 1)
    same_head = (srow >= RWKV_HEAD_DIM) == (scol >= RWKV_HEAD_DIM)

    cums = [_mm_exact_lhs(ones[inst[7]], inst[3], 2) for inst in insts]
    pre = []
    for (r, v, kk, lw, b, k, s, rev), cum in zip(insts, cums):
        tot = cum[0:1] if rev else cum[L - 1:L]
        mid = L // 2 if rev else L // 2 - 1
        rho = cum[mid:mid + 1]
        e1 = jnp.exp(cum - rho)
        e2 = jnp.exp(rho - cum)
        er = jnp.exp(rho)
        et = jnp.exp(tot - rho)
        a_t = -kk * e1 * jnp.exp(-lw)
        r_t = r * e1
        b_t = b * e2
        k_t = k * e2
        pre.append((a_t, r_t, b_t, k_t, er, et))
    grams = []
    for a_t, r_t, b_t, k_t, _, _ in pre:
        zero = jnp.zeros_like(a_t)
        lhs = jnp.concatenate([jnp.where(h0, a_t, zero), jnp.where(h0, zero, a_t),
                               jnp.where(h0, r_t, zero), jnp.where(h0, zero, r_t)], axis=0)
        rhs = jnp.concatenate([b_t, k_t], axis=0)
        grams.append(_mm(lhs, rhs, NT))
    a_bk, r_bk = [], []
    for inst, gram in zip(insts, grams):
        rev = inst[7]
        a_bk.append([jnp.where(strict2[rev], gram[hh * L:(hh + 1) * L], 0.0) for hh in range(2)])
        r_bk.append([jnp.where(incl2[rev], gram[(2 + hh) * L:(3 + hh) * L], 0.0) for hh in range(2)])
    t_inv = _tri_inverse_all([a_bk[n][hh][:, :L] for n in range(n_inst) for hh in range(2)],
                             [insts[n][7] for n in range(n_inst) for hh in range(2)])
    xs = [_mm(jnp.concatenate([a_t * er, r_t * er], axis=0), inst[6], NT)
          for inst, (a_t, r_t, _, _, er, _) in zip(insts, pre)]
    pick = lambda t2: jnp.where(h0, t2[:L], t2[L:])
    akv = [_mm(jnp.concatenate([a_bk[n][0][:, L:], a_bk[n][1][:, L:]], axis=0), inst[1])
           for n, inst in enumerate(insts)]
    x = [xs[n][:L] + pick(akv[n]) for n in range(n_inst)]
    u = [pick(_mm(jnp.concatenate([t_inv[2 * n], t_inv[2 * n + 1]], axis=0), x[n])) for n in range(n_inst)]
    z = [jnp.concatenate([u[n], inst[1]], axis=0) for n, inst in enumerate(insts)]
    out = []
    for n, inst in enumerate(insts):
        _, _, b_t, k_t, er, et = pre[n]
        y = xs[n][L:] + pick(_mm(jnp.concatenate([r_bk[n][0], r_bk[n][1]], axis=0), z[n]))
        w = jnp.concatenate([b_t * et, k_t * et], axis=0)
        s_new = inst[6] * (er * et) + jnp.where(same_head, _mm(z[n], w, TN), 0.0)
        out.append((y, s_new))
    return out


def _scan_kernel(rf_ref, vf_ref, kkf_ref, lwf_ref, bf_ref, kf_ref,
                 rb_ref, vb_ref, kkb_ref, lwb_ref, bb_ref, kb_ref,
                 yf_ref, yb_ref, s_ref):
    @pl.when(pl.program_id(1) == 0)
    def _():
        s_ref[...] = jnp.zeros_like(s_ref)

    n_pairs = s_ref.shape[1]
    insts = []
    for p in range(n_pairs):
        cs = slice(p * LANES, (p + 1) * LANES)
        insts.append(tuple(ref[:, cs].astype(F32) for ref in (rf_ref, vf_ref, kkf_ref, lwf_ref, bf_ref, kf_ref))
                     + (s_ref[0, p], False))
        insts.append(tuple(ref[:, cs].astype(F32) for ref in (rb_ref, vb_ref, kkb_ref, lwb_ref, bb_ref, kb_ref))
                     + (s_ref[1, p], True))
    res = _chunk_all(insts)
    for p in range(n_pairs):
        cs = slice(p * LANES, (p + 1) * LANES)
        yf_ref[:, cs], s_ref[0, p] = res[2 * p]
        yb_ref[:, cs], s_ref[1, p] = res[2 * p + 1]


def _rwkv_scan(r, v, kk, lwf, bf, kf, lwb, bb, kb, pairs_per_step=8):
    T = r.shape[0]
    nc = T // CHUNK
    width = pairs_per_step * LANES
    fwd = pl.BlockSpec((CHUNK, width), lambda p, c: (c, p))
    bwd = pl.BlockSpec((CHUNK, width), lambda p, c: (nc - 1 - c, p))
    out = jax.ShapeDtypeStruct((T, RWKV_DIM), F32)
    return pl.pallas_call(
        _scan_kernel,
        grid=(N_PAIRS // pairs_per_step, nc),
        in_specs=[fwd] * 6 + [bwd] * 6,
        out_specs=[fwd, bwd],
        out_shape=[out, out],
        scratch_shapes=[pltpu.VMEM((2, pairs_per_step, LANES, LANES), F32)],
        compiler_params=_cparams(("parallel", "arbitrary")),
        name="rwkv_scan",
    )(r, v, kk, lwf, bf, kf, r, v, kk, lwb, bb, kb)


def _rms(x, gain):
    return x * lax.rsqrt(jnp.mean(x * x, axis=-1, keepdims=True) + NORM_EPS) * gain


def _outproj_kernel(yf_ref, yb_ref, bonus_ref, g_ref, att_ref, gr_ref, gain_ref, bias_ref, bd_ref,
                    w_ref, x_ref, gpost_ref, gpre_ref, h_ref, hn_ref, m_ref):
    ones_bd = bd_ref[...]
    inv_n = 1.0 / RWKV_HEAD_DIM
    for j in range(N_PAIRS):
        cs = slice(j * LANES, (j + 1) * LANES)
        y = yf_ref[:, cs] + yb_ref[:, cs]
        mean = _head_sum(y, ones_bd) * inv_n
        d = y - mean
        var = _head_sum(d * d, ones_bd) * inv_n
        yn = d * lax.rsqrt(var + GN_EPS) * gain_ref[:, cs] + bias_ref[:, cs]
        o_rwkv = (yn + bonus_ref[:, cs].astype(F32)) * g_ref[:, cs].astype(F32)
        merged = att_ref[:, cs].astype(F32) + jax.nn.sigmoid(gr_ref[:, cs].astype(F32)) * o_rwkv
        m_ref[:, cs] = merged.astype(BF16)
    mix = jnp.dot(m_ref[...], w_ref[...], preferred_element_type=F32)
    h = x_ref[...] + _rms(mix, gpost_ref[...])
    h_ref[...] = h
    hn_ref[...] = _rms(h, gpre_ref[...]).astype(BF16)


def _outproj(yf, yb, bonus, g, att, cols, gain, bias, ones_bd, w_out, x, g_post, g_pre, tm=256):
    T = x.shape[0]
    row = pl.BlockSpec((tm, D_MODEL), lambda i: (i, 0))
    vec = pl.BlockSpec((1, D_MODEL), lambda i: (0, 0))
    return pl.pallas_call(
        _outproj_kernel,
        grid=(T // tm,),
        in_specs=[row, row, row, row, row,
                  pl.BlockSpec((tm, D_MODEL), lambda i: (i, CB_GR // 16)),
                  vec, vec, pl.BlockSpec((LANES, LANES), lambda i: (0, 0)),
                  pl.BlockSpec((D_MODEL, D_MODEL), lambda i: (0, 0), pipeline_mode=pl.Buffered(1)),
                  row, vec, vec],
        out_specs=[row, row],
        out_shape=[jax.ShapeDtypeStruct((T, D_MODEL), F32), jax.ShapeDtypeStruct((T, D_MODEL), BF16)],
        scratch_shapes=[pltpu.VMEM((tm, D_MODEL), BF16)],
        compiler_params=_cparams(("parallel",)),
        name="outproj",
    )(yf, yb, bonus, g, att, cols, gain, bias, ones_bd, w_out, x, g_post, g_pre)


def _ffn_kernel(hn_ref, wu_ref, wd_ref, h_ref, g_ref, o_ref, acc_ref):
    j = pl.program_id(1)

    @pl.when(j == 0)
    def _():
        acc_ref[...] = jnp.zeros_like(acc_ref)

    up = jnp.dot(hn_ref[...], wu_ref[...], preferred_element_type=F32)
    act = jnp.square(jnp.maximum(up, 0.0)).astype(BF16)
    acc_ref[...] += jnp.dot(act, wd_ref[...], preferred_element_type=F32)

    @pl.when(j == pl.num_programs(1) - 1)
    def _():
        o_ref[...] = h_ref[...] + _rms(acc_ref[...], g_ref[...])


def _ffn(hn, w_up, w_down, h, gain, tm=1024, tf=512):
    T = h.shape[0]
    row = pl.BlockSpec((tm, D_MODEL), lambda i, j: (i, 0))
    row_once = pl.BlockSpec((tm, D_MODEL), lambda i, j: (i, 0), pipeline_mode=pl.Buffered(1))
    return pl.pallas_call(
        _ffn_kernel,
        grid=(T // tm, D_FF // tf),
        in_specs=[row,
                  pl.BlockSpec((D_MODEL, tf), lambda i, j: (0, j)),
                  pl.BlockSpec((tf, D_MODEL), lambda i, j: (j, 0)),
                  row_once,
                  pl.BlockSpec((1, D_MODEL), lambda i, j: (0, 0))],
        out_specs=row_once,
        out_shape=jax.ShapeDtypeStruct((T, D_MODEL), F32),
        scratch_shapes=[pltpu.VMEM((tm, D_MODEL), F32)],
        compiler_params=_cparams(("parallel", "arbitrary")),
        name="ffn",
    )(hn, w_up, w_down, h, gain)


def _pad_cols(t, n):
    return jnp.pad(t, ((0, 0), (0, n - t.shape[1])))


def _pad_rows(t, n):
    return jnp.pad(t, ((0, n - t.shape[0]), (0, 0)))


def _split_cols(t, sizes):
    idx = [int(i) for i in np.cumsum(sizes)[:-1]]
    return jnp.split(t, idx, axis=-1)


def _permute_in_cols(t):
    shift_sizes = [RWKV_DIM] * 3 + [DECAY_LORA] * 2 + [ICLR_LORA] * 2 + [GATE_LORA]
    q, ak, av, rw, ga, gr = _split_cols(t, [D_MODEL, KV_COLS, KV_COLS, sum(shift_sizes), D_MODEL, D_MODEL])
    r, k, v, wdf, wdb, adf, adb, gd = _split_cols(rw, shift_sizes)
    lora = jnp.concatenate([_pad_cols(p, LORA_PAD) for p in (wdf, wdb, adf, adb)] + [gd], axis=1)
    return jnp.concatenate([q, ga, gr, r, k, v, ak, av, _pad_cols(lora, LORA_COLS)], axis=1)


def _rope_tables(T):
    pos = jnp.arange(T, dtype=F32)
    inv_freq = ROPE_THETA ** (-jnp.arange(0, ATT_HEAD_DIM, 2, dtype=F32) / ATT_HEAD_DIM)
    ang = pos[:, None] * inv_freq[None, :]
    cos, sin = jnp.cos(ang), jnp.sin(ang)
    return jnp.concatenate([cos, cos], axis=1), jnp.concatenate([-sin, sin], axis=1)


def kernel(x, norm_pre_mix, w_in, mu_shift, attn_sink, w0_fwd, w_up_fwd, w0_bwd, w_up_bwd, a0_fwd, a_up_fwd, a0_bwd, a_up_bwd, g_up, k_k, k_a, r_k, ln_x_gain, ln_x_bias, w_out, norm_post_mix, norm_pre_ffn, w_ffn_up, w_ffn_down, norm_post_ffn):
    B, T, _ = x.shape
    depth = w_in.shape[0]
    cos2, sin2 = _rope_tables(T)
    lane = np.arange(LANES)
    ones_bd = jnp.asarray((lane[:, None] // RWKV_HEAD_DIM) == (lane[None, :] // RWKV_HEAD_DIM), BF16)
    row = lambda t: t.reshape(1, -1)
    outs = []
    for bi in range(B):
        h = x[bi]
        for l in range(depth):
            w_perm = _permute_in_cols(w_in[l].astype(BF16))
            mu = mu_shift[l].reshape(1, -1)
            shift_sizes = [RWKV_DIM] * 3 + [DECAY_LORA] * 2 + [ICLR_LORA] * 2 + [GATE_LORA]
            mr, mk, mv, m1, m2, m3, m4, mg = _split_cols(mu, shift_sizes)
            mu_rkv = jnp.concatenate([mr, mk, mv], axis=1)
            mu_lora = _pad_cols(jnp.concatenate([_pad_cols(p, LORA_PAD) for p in (m1, m2, m3, m4)] + [mg], axis=1),
                                LORA_COLS)
            cols = _inproj(h, row(norm_pre_mix[l]), w_perm)
            att = _attention(cols, attn_sink[l], cos2, sin2)
            vecs = [row(t[l]) for t in (w0_fwd, w0_bwd, a0_fwd, a0_bwd, k_k, k_a, r_k)]
            r, v, kk, lwf, bf, kf, lwb, bb, kb, g, bonus = _rwkv_prep(
                cols, mu_rkv, mu_lora, vecs,
                *[_pad_rows(t[l], LORA_PAD).astype(BF16) for t in (w_up_fwd, w_up_bwd, a_up_fwd, a_up_bwd)],
                g_up[l].astype(BF16), ones_bd)
            yf, yb = _rwkv_scan(r, v, kk, lwf, bf, kf, lwb, bb, kb)
            h, hn = _outproj(yf, yb, bonus, g, att, cols, row(ln_x_gain[l]), row(ln_x_bias[l]), ones_bd,
                             w_out[l].astype(BF16), h, row(norm_post_mix[l]), row(norm_pre_ffn[l]))
            h = _ffn(hn, w_ffn_up[l].astype(BF16), w_ffn_down[l].astype(BF16), h, row(norm_post_ffn[l]))
        outs.append(h)
    return jnp.stack(outs, axis=0)
```

```python
import functools

import jax
import jax.numpy as jnp
import numpy as np
from jax import lax
from jax.experimental import pallas as pl
from jax.experimental.pallas import tpu as pltpu

F32 = jnp.float32
BF16 = jnp.bfloat16
LANES = 128
SUBLANES = 8
PACKED_SUBLANES = 16

D_MODEL = 2048
ATT_HEAD_DIM = 128
ATT_HEADS = 16
ATT_KV_HEADS = 4
ATT_GROUP = 4
WINDOW = 128
BLOCK = 128
ATT_Q_BLOCKS = 2
ROPE_THETA = 10000.0
RWKV_HEAD_DIM = 64
RWKV_DIM = 2048
N_PAIRS = RWKV_DIM // LANES
DECAY_LORA = 96
ICLR_LORA = 96
GATE_LORA = 256
LORA_PAD = 128
LORA_COLS = 1024
D_FF = 4 * D_MODEL
NORM_EPS = 1e-6
GN_EPS = 64e-5
MASK_VALUE = -1e30
DECAY_SCALE = float(np.exp(-0.5))
KV_COLS = ATT_KV_HEADS * ATT_HEAD_DIM

CB_Q, CB_GA, CB_GR, CB_R, CB_K, CB_V, CB_AK, CB_AV, CB_LORA = 0, 16, 32, 48, 64, 80, 96, 100, 104

CHUNK = 128
VMEM_LIMIT = 56 * 1024 * 1024

NN = (((1,), (0,)), ((), ()))
NT = (((1,), (1,)), ((), ()))
TN = (((0,), (0,)), ((), ()))


def _mm(a, b, dims=NN):
    return lax.dot_general(a.astype(BF16), b.astype(BF16), dims, preferred_element_type=F32)


def _split_bf16(x, parts):
    out = []
    for _ in range(parts - 1):
        hi = x.astype(BF16)
        out.append(hi)
        x = x - hi.astype(F32)
    out.append(x.astype(BF16))
    return out


def _mm_exact_lhs(a, b, parts):
    a = a.astype(BF16)
    acc = None
    for term in _split_bf16(b, parts):
        p = lax.dot_general(a, term, NN, preferred_element_type=F32)
        acc = p if acc is None else acc + p
    return acc


def _cparams(sem):
    return pltpu.CompilerParams(dimension_semantics=sem, vmem_limit_bytes=VMEM_LIMIT)


def _inproj_kernel(x_ref, g_ref, w_ref, o_ref, xn_ref):
    @pl.when(pl.program_id(1) == 0)
    def _():
        x = x_ref[...]
        ms = jnp.mean(x * x, axis=-1, keepdims=True)
        xn_ref[...] = (x * lax.rsqrt(ms + NORM_EPS) * g_ref[...]).astype(BF16)

    o_ref[...] = jnp.dot(xn_ref[...], w_ref[...], preferred_element_type=F32).astype(o_ref.dtype)


def _inproj(x, gain, w, tm=1024, tn=2048):
    T = x.shape[0]
    n = w.shape[1]
    return pl.pallas_call(
        _inproj_kernel,
        grid=(T // tm, n // tn),
        in_specs=[
            pl.BlockSpec((tm, D_MODEL), lambda i, j: (i, 0)),
            pl.BlockSpec((1, D_MODEL), lambda i, j: (0, 0)),
            pl.BlockSpec((D_MODEL, tn), lambda i, j: (0, j)),
        ],
        out_specs=pl.BlockSpec((tm, tn), lambda i, j: (i, j)),
        out_shape=jax.ShapeDtypeStruct((T, n), BF16),
        scratch_shapes=[pltpu.VMEM((tm, D_MODEL), BF16)],
        compiler_params=_cparams(("parallel", "arbitrary")),
        name="inproj",
    )(x, gain, w)


def _rope(x, c, s):
    return x * c + pltpu.roll(x, ATT_HEAD_DIM // 2, 1) * s


def _attn_kernel(sink_ref, q_ref, kp_ref, kc_ref, kn_ref, vp_ref, vc_ref, vn_ref, gate_ref,
                 cc_ref, sc_ref, cp_ref, sp_ref, cn_ref, sn_ref, o_ref, *, seq):
    i = pl.program_id(0)
    cc, sc = cc_ref[...], sc_ref[...]
    qi = lax.broadcasted_iota(jnp.int32, (BLOCK, 3 * BLOCK), 0)
    sj = lax.broadcasted_iota(jnp.int32, (BLOCK, 3 * BLOCK), 1)
    in_window = jnp.abs(sj - BLOCK - qi) <= WINDOW
    rowg = lax.broadcasted_iota(jnp.int32, (ATT_GROUP * BLOCK, 1), 0) // BLOCK
    scale = ATT_HEAD_DIM ** -0.5
    cos_all = jnp.concatenate([cp_ref[...], cc, cn_ref[...]], axis=0)
    sin_all = jnp.concatenate([sp_ref[...], sc, sn_ref[...]], axis=0)
    for g in range(ATT_KV_HEADS):
        ks = slice(g * ATT_HEAD_DIM, (g + 1) * ATT_HEAD_DIM)
        k_all = jnp.concatenate([kp_ref[:, ks], kc_ref[:, ks], kn_ref[:, ks]], axis=0).astype(F32)
        k_all = _rope(k_all, cos_all, sin_all).astype(BF16)
        v_all = jnp.concatenate([vp_ref[:, ks], vc_ref[:, ks], vn_ref[:, ks]], axis=0)
        heads = [g * ATT_GROUP + hh for hh in range(ATT_GROUP)]
        sink = jnp.zeros((ATT_GROUP * BLOCK, 1), F32)
        for hh, h in enumerate(heads):
            sink = jnp.where(rowg == hh, sink_ref[h], sink)
        for qb in range(ATT_Q_BLOCKS):
            rows = slice(qb * BLOCK, (qb + 1) * BLOCK)
            win = slice(qb * BLOCK, (qb + 3) * BLOCK)
            kpos = (i * ATT_Q_BLOCKS + qb - 1) * BLOCK + sj
            valid = in_window & (kpos >= 0) & (kpos < seq)
            valid4 = jnp.concatenate([valid] * ATT_GROUP, axis=0)
            q4 = jnp.concatenate(
                [_rope(q_ref[rows, h * ATT_HEAD_DIM:(h + 1) * ATT_HEAD_DIM].astype(F32), cc[rows], sc[rows])
                 for h in heads], axis=0) * scale
            s = _mm(q4, k_all[win], NT)
            s = jnp.where(valid4, s, MASK_VALUE)
            m = jnp.maximum(jnp.max(s, axis=-1, keepdims=True), sink)
            p = jnp.exp(s - m)
            den = jnp.sum(p, axis=-1, keepdims=True) + jnp.exp(sink - m)
            o = _mm(p, v_all[win]) / den
            for hh, h in enumerate(heads):
                cs = slice(h * ATT_HEAD_DIM, (h + 1) * ATT_HEAD_DIM)
                o_ref[rows, cs] = (o[hh * BLOCK:(hh + 1) * BLOCK]
                                   * jax.nn.sigmoid(gate_ref[rows, cs].astype(F32))).astype(o_ref.dtype)


def _attention(cols, sink, cos2, sin2):
    T = cols.shape[0]
    Q = ATT_Q_BLOCKS
    nb = T // BLOCK
    prev = lambda i: (jnp.maximum(i * Q - 1, 0), 0)
    cur = lambda i: (i, 0)
    nxt = lambda i: (jnp.minimum((i + 1) * Q, nb - 1), 0)
    kcb, vcb = CB_AK * LANES // KV_COLS, CB_AV * LANES // KV_COLS
    col = lambda f, cb: (lambda i: (f(i)[0], cb))
    halo = lambda width, f, cb: pl.BlockSpec((BLOCK, width), col(f, cb))
    main = lambda width, cb: pl.BlockSpec((Q * BLOCK, width), col(cur, cb))
    return pl.pallas_call(
        functools.partial(_attn_kernel, seq=T),
        grid=(nb // Q,),
        in_specs=[
            pl.BlockSpec(memory_space=pltpu.SMEM),
            main(D_MODEL, CB_Q // 16),
            halo(KV_COLS, prev, kcb), main(KV_COLS, kcb), halo(KV_COLS, nxt, kcb),
            halo(KV_COLS, prev, vcb), main(KV_COLS, vcb), halo(KV_COLS, nxt, vcb),
            main(D_MODEL, CB_GA // 16),
            main(LANES, 0), main(LANES, 0),
            halo(LANES, prev, 0), halo(LANES, prev, 0),
            halo(LANES, nxt, 0), halo(LANES, nxt, 0),
        ],
        out_specs=main(D_MODEL, 0),
        out_shape=jax.ShapeDtypeStruct((T, D_MODEL), BF16),
        compiler_params=_cparams(("parallel",)),
        name="attention",
    )(sink, cols, cols, cols, cols, cols, cols, cols, cols, cos2, sin2, cos2, sin2, cos2, sin2)


def _head_sum(x, ones_bd):
    return _mm(x, ones_bd)


def _prep_kernel(r_ref, rp_ref, rn_ref, k_ref, kp_ref, kn_ref, v_ref, vp_ref, vn_ref,
                 lo_ref, lop_ref, lon_ref, mur_ref, muk_ref, muv_ref, mul_ref,
                 w0f_ref, w0b_ref, a0f_ref, a0b_ref, kk_ref, ka_ref, rk_ref,
                 wuf_ref, wub_ref, auf_ref, aub_ref, gup_ref, bd_ref,
                 r_o, v_o, kk_o, lwf_o, bf_o, kf_o, lwb_o, bb_o, kb_o, g_o, bonus_o, act_ref):
    i = pl.program_id(0)
    first = i == 0
    last = i == pl.num_programs(0) - 1
    tm = r_ref.shape[0]

    def shift(c_ref, p_ref, n_ref, mu):
        c = c_ref[...].astype(F32)
        row = lax.broadcasted_iota(jnp.int32, c.shape, 0)
        prow = jnp.where(first, 0.0, p_ref[PACKED_SUBLANES - 1:PACKED_SUBLANES, :].astype(F32))
        nrow = jnp.where(last, 0.0, n_ref[0:1, :].astype(F32))
        prev = jnp.where(row == 0, prow, pltpu.roll(c, 1, 0))
        nxt = jnp.where(row == tm - 1, nrow, pltpu.roll(c, tm - 1, 0))
        return c * (1.0 - mu) + (prev + nxt) * (0.5 * mu)

    r = shift(r_ref, rp_ref, rn_ref, mur_ref[...])
    k = shift(k_ref, kp_ref, kn_ref, muk_ref[...])
    v = shift(v_ref, vp_ref, vn_ref, muv_ref[...])
    P = LORA_PAD

    @pl.when(pl.program_id(1) == 0)
    def _():
        lo = shift(lo_ref, lop_ref, lon_ref, mul_ref[...])
        act_ref[:, 0:2 * P] = jnp.tanh(lo[:, 0:2 * P]).astype(BF16)
        act_ref[:, 2 * P:4 * P] = lo[:, 2 * P:4 * P].astype(BF16)
        act_ref[:, 4 * P:] = jax.nn.sigmoid(lo[:, 4 * P:4 * P + GATE_LORA]).astype(BF16)

    ones_bd = bd_ref[...]

    kk = k * kk_ref[...]
    kk = kk * lax.rsqrt(jnp.maximum(_head_sum(kk * kk, ones_bd), 1e-24))
    k_a = ka_ref[...]

    def direction(wd, ad, w0, wu, a0, au):
        lw = -DECAY_SCALE * jax.nn.sigmoid(w0 + _mm(wd, wu))
        a = jax.nn.sigmoid(a0 + _mm(ad, au))
        k_mod = k * (1.0 + (a - 1.0) * k_a)
        return lw, kk * a, k_mod

    lwf, bf, kf = direction(act_ref[:, 0:P], act_ref[:, 2 * P:3 * P],
                            w0f_ref[...], wuf_ref[...], a0f_ref[...], auf_ref[...])
    lwb, bb, kb = direction(act_ref[:, P:2 * P], act_ref[:, 3 * P:4 * P],
                            w0b_ref[...], wub_ref[...], a0b_ref[...], aub_ref[...])
    g = _mm(act_ref[:, 4 * P:], gup_ref[...])
    bonus = _head_sum(r * (0.5 * (kf + kb)) * rk_ref[...], ones_bd) * v

    for ref, val in ((r_o, r), (v_o, v), (kk_o, kk), (lwf_o, lwf), (bf_o, bf), (kf_o, kf),
                     (lwb_o, lwb), (bb_o, bb), (kb_o, kb), (g_o, g), (bonus_o, bonus)):
        ref[...] = val.astype(ref.dtype)


def _rwkv_prep(cols, mu_rkv, mu_lora, vecs, w_up_f, w_up_b, a_up_f, a_up_b, g_up, ones_bd, tm=2048):
    T = cols.shape[0]
    nt = T // tm
    halo = PACKED_SUBLANES
    hb = tm // halo
    nhb = T // halo
    main = lambda cb: pl.BlockSpec((tm, LANES), lambda i, j: (i, cb + j))
    hprev = lambda cb: pl.BlockSpec((halo, LANES), lambda i, j: (jnp.maximum(i * hb - 1, 0), cb + j))
    hnext = lambda cb: pl.BlockSpec((halo, LANES), lambda i, j: (jnp.minimum((i + 1) * hb, nhb - 1), cb + j))
    lcb = CB_LORA * LANES // LORA_COLS
    vec = lambda off: pl.BlockSpec((1, LANES), lambda i, j: (0, off + j))
    up = lambda rows: pl.BlockSpec((rows, LANES), lambda i, j: (0, j))
    in_specs = []
    for cb in (CB_R, CB_K, CB_V):
        in_specs += [main(cb), hprev(cb), hnext(cb)]
    in_specs += [
        pl.BlockSpec((tm, LORA_COLS), lambda i, j: (i, lcb)),
        pl.BlockSpec((halo, LORA_COLS), lambda i, j: (jnp.maximum(i * hb - 1, 0), lcb)),
        pl.BlockSpec((halo, LORA_COLS), lambda i, j: (jnp.minimum((i + 1) * hb, nhb - 1), lcb)),
        vec(0), vec(N_PAIRS), vec(2 * N_PAIRS),
        pl.BlockSpec((1, LORA_COLS), lambda i, j: (0, 0)),
    ]
    in_specs += [vec(0)] * 7
    in_specs += [up(LORA_PAD)] * 4 + [up(GATE_LORA)]
    in_specs += [pl.BlockSpec((LANES, LANES), lambda i, j: (0, 0))]
    dtypes = [BF16, BF16, BF16, F32, BF16, BF16, F32, BF16, BF16, BF16, BF16]
    return pl.pallas_call(
        _prep_kernel,
        grid=(nt, N_PAIRS),
        in_specs=in_specs,
        out_specs=[pl.BlockSpec((tm, LANES), lambda i, j: (i, j))] * 11,
        out_shape=[jax.ShapeDtypeStruct((T, RWKV_DIM), dt) for dt in dtypes],
        scratch_shapes=[pltpu.VMEM((tm, 4 * LORA_PAD + GATE_LORA), BF16)],
        compiler_params=_cparams(("parallel", "arbitrary")),
        name="rwkv_prep",
    )(cols, cols, cols, cols, cols, cols, cols, cols, cols, cols, cols, cols,
      mu_rkv, mu_rkv, mu_rkv, mu_lora, *vecs, w_up_f, w_up_b, a_up_f, a_up_b, g_up, ones_bd)


def _tri_inverse_all(mats, upper):
    L = mats[0].shape[0]
    row = lax.broadcasted_iota(jnp.int32, (L, L), 0)
    col = lax.broadcasted_iota(jnp.int32, (L, L), 1)
    hrow = lax.broadcasted_iota(jnp.int32, (L // 2, L), 0)
    hcol = lax.broadcasted_iota(jnp.int32, (L // 2, L), 1)
    same = lambda sh: (row >> sh) == (col >> sh)
    eye = jnp.where(row == col, 1.0, 0.0)
    ds = [eye + jnp.where(same(1), a, 0.0) for a in mats]
    sh = 1
    while (1 << sh) < L:
        m = 1 << sh
        dbs = [d.astype(BF16) for d in ds]
        if m < SUBLANES:
            level = same(sh + 1) & jnp.logical_not(same(sh))
            ts = [_mm(jnp.where(level, a, 0.0), db) for a, db in zip(mats, dbs)]
            ds = [d + _mm(db, t) for d, db, t in zip(ds, dbs, ts)]
        else:
            n_half = L // (2 * m)
            part = [slice(q * m, (q + 1) * m) for q in range(n_half)]
            blocks = lambda x: [x[q * m:(q + 1) * m] for q in range(2 * n_half)]
            zero = jnp.zeros((m, L), F32)
            act = lambda x, up: jnp.concatenate(blocks(x)[(0 if up else 1)::2], axis=0)
            level = {False: (hcol >> sh) == 2 * (hrow >> sh), True: (hcol >> sh) == 2 * (hrow >> sh) + 1}

            def spread(x, up, rest=None):
                out = []
                for q in range(n_half):
                    other = zero if rest is None else rest[2 * q + (1 if up else 0)]
                    out += [x[part[q]], other] if up else [other, x[part[q]]]
                return jnp.concatenate(out, axis=0)

            ts = [_mm(jnp.where(level[up], act(a, up), 0.0), db) for a, db, up in zip(mats, dbs, upper)]
            upd = [_mm(act(d, up), spread(t, up)) for d, t, up in zip(ds, ts, upper)]
            ds = [spread(act(d, up) + u, up, rest=blocks(d)) for d, u, up in zip(ds, upd, upper)]
        sh += 1
    return ds


def _chunk_all(insts):
    L = insts[0][0].shape[0]
    n_inst = len(insts)
    row = lax.broadcasted_iota(jnp.int32, (L, L), 0)
    col = lax.broadcasted_iota(jnp.int32, (L, L), 1)
    incl = {False: col <= row, True: col >= row}
    strict = {False: col < row, True: col > row}
    ones = {rev: jnp.where(incl[rev], 1.0, 0.0) for rev in (False, True)}
    incl2 = {rev: jnp.concatenate([incl[rev]] * 2, axis=1) for rev in (False, True)}
    strict2 = {rev: jnp.concatenate([strict[rev]] * 2, axis=1) for rev in (False, True)}
    lane = lax.broadcasted_iota(jnp.int32, (1, LANES), 1)
    h0 = lane < RWKV_HEAD_DIM
    srow = lax.broadcasted_iota(jnp.int32, (LANES, LANES), 0)
    scol = lax.broadcasted_iota(jnp.int32, (LANES, LANES), 1)
    same_head = (srow >= RWKV_HEAD_DIM) == (scol >= RWKV_HEAD_DIM)

    cums = [_mm_exact_lhs(ones[inst[7]], inst[3], 2) for inst in insts]
    pre = []
    for (r, v, kk, lw, b, k, s, rev), cum in zip(insts, cums):
        tot = cum[0:1] if rev else cum[L - 1:L]
        mid = L // 2 if rev else L // 2 - 1
        rho = cum[mid:mid + 1]
        e1 = jnp.exp(cum - rho)
        e2 = jnp.exp(rho - cum)
        er = jnp.exp(rho)
        et = jnp.exp(tot - rho)
        a_t = -kk * e1 * jnp.exp(-lw)
        r_t = r * e1
        b_t = b * e2
        k_t = k * e2
        pre.append((a_t, r_t, b_t, k_t, er, et))
    grams = []
    for a_t, r_t, b_t, k_t, _, _ in pre:
        zero = jnp.zeros_like(a_t)
        lhs = jnp.concatenate([jnp.where(h0, a_t, zero), jnp.where(h0, zero, a_t),
                               jnp.where(h0, r_t, zero), jnp.where(h0, zero, r_t)], axis=0)
        rhs = jnp.concatenate([b_t, k_t], axis=0)
        grams.append(_mm(lhs, rhs, NT))
    a_bk, r_bk = [], []
    for inst, gram in zip(insts, grams):
        rev = inst[7]
        a_bk.append([jnp.where(strict2[rev], gram[hh * L:(hh + 1) * L], 0.0) for hh in range(2)])
        r_bk.append([jnp.where(incl2[rev], gram[(2 + hh) * L:(3 + hh) * L], 0.0) for hh in range(2)])
    t_inv = _tri_inverse_all([a_bk[n][hh][:, :L] for n in range(n_inst) for hh in range(2)],
                             [insts[n][7] for n in range(n_inst) for hh in range(2)])
    xs = [_mm(jnp.concatenate([a_t * er, r_t * er], axis=0), inst[6], NT)
          for inst, (a_t, r_t, _, _, er, _) in zip(insts, pre)]
    pick = lambda t2: jnp.where(h0, t2[:L], t2[L:])
    akv = [_mm(jnp.concatenate([a_bk[n][0][:, L:], a_bk[n][1][:, L:]], axis=0), inst[1])
           for n, inst in enumerate(insts)]
    x = [xs[n][:L] + pick(akv[n]) for n in range(n_inst)]
    u = [pick(_mm(jnp.concatenate([t_inv[2 * n], t_inv[2 * n + 1]], axis=0), x[n])) for n in range(n_inst)]
    z = [jnp.concatenate([u[n], inst[1]], axis=0) for n, inst in enumerate(insts)]
    out = []
    for n, inst in enumerate(insts):
        _, _, b_t, k_t, er, et = pre[n]
        y = xs[n][L:] + pick(_mm(jnp.concatenate([r_bk[n][0], r_bk[n][1]], axis=0), z[n]))
        w = jnp.concatenate([b_t * et, k_t * et], axis=0)
        s_new = inst[6] * (er * et) + jnp.where(same_head, _mm(z[n], w, TN), 0.0)
        out.append((y, s_new))
    return out


def _scan_kernel(rf_ref, vf_ref, kkf_ref, lwf_ref, bf_ref, kf_ref,
                 rb_ref, vb_ref, kkb_ref, lwb_ref, bb_ref, kb_ref,
                 yf_ref, yb_ref, s_ref):
    @pl.when(pl.program_id(1) == 0)
    def _():
        s_ref[...] = jnp.zeros_like(s_ref)

    n_pairs = s_ref.shape[1]
    insts = []
    for p in range(n_pairs):
        cs = slice(p * LANES, (p + 1) * LANES)
        insts.append(tuple(ref[:, cs].astype(F32) for ref in (rf_ref, vf_ref, kkf_ref, lwf_ref, bf_ref, kf_ref))
                     + (s_ref[0, p], False))
        insts.append(tuple(ref[:, cs].astype(F32) for ref in (rb_ref, vb_ref, kkb_ref, lwb_ref, bb_ref, kb_ref))
                     + (s_ref[1, p], True))
    res = _chunk_all(insts)
    for p in range(n_pairs):
        cs = slice(p * LANES, (p + 1) * LANES)
        yf_ref[:, cs], s_ref[0, p] = res[2 * p]
        yb_ref[:, cs], s_ref[1, p] = res[2 * p + 1]


def _rwkv_scan(r, v, kk, lwf, bf, kf, lwb, bb, kb, pairs_per_step=16):
    T = r.shape[0]
    nc = T // CHUNK
    width = pairs_per_step * LANES
    fwd = pl.BlockSpec((CHUNK, width), lambda p, c: (c, p))
    bwd = pl.BlockSpec((CHUNK, width), lambda p, c: (nc - 1 - c, p))
    out = jax.ShapeDtypeStruct((T, RWKV_DIM), F32)
    return pl.pallas_call(
        _scan_kernel,
        grid=(N_PAIRS // pairs_per_step, nc),
        in_specs=[fwd] * 6 + [bwd] * 6,
        out_specs=[fwd, bwd],
        out_shape=[out, out],
        scratch_shapes=[pltpu.VMEM((2, pairs_per_step, LANES, LANES), F32)],
        compiler_params=_cparams(("parallel", "arbitrary")),
        name="rwkv_scan",
    )(r, v, kk, lwf, bf, kf, r, v, kk, lwb, bb, kb)


def _rms(x, gain):
    return x * lax.rsqrt(jnp.mean(x * x, axis=-1, keepdims=True) + NORM_EPS) * gain


def _outproj_kernel(yf_ref, yb_ref, bonus_ref, g_ref, att_ref, gr_ref, gain_ref, bias_ref, bd_ref,
                    w_ref, x_ref, gpost_ref, gpre_ref, h_ref, hn_ref, m_ref):
    ones_bd = bd_ref[...]
    inv_n = 1.0 / RWKV_HEAD_DIM
    for j in range(N_PAIRS):
        cs = slice(j * LANES, (j + 1) * LANES)
        y = yf_ref[:, cs] + yb_ref[:, cs]
        mean = _head_sum(y, ones_bd) * inv_n
        d = y - mean
        var = _head_sum(d * d, ones_bd) * inv_n
        yn = d * lax.rsqrt(var + GN_EPS) * gain_ref[:, cs] + bias_ref[:, cs]
        o_rwkv = (yn + bonus_ref[:, cs].astype(F32)) * g_ref[:, cs].astype(F32)
        merged = att_ref[:, cs].astype(F32) + jax.nn.sigmoid(gr_ref[:, cs].astype(F32)) * o_rwkv
        m_ref[:, cs] = merged.astype(BF16)
    mix = jnp.dot(m_ref[...], w_ref[...], preferred_element_type=F32)
    h = x_ref[...] + _rms(mix, gpost_ref[...])
    h_ref[...] = h
    hn_ref[...] = _rms(h, gpre_ref[...]).astype(BF16)


def _outproj(yf, yb, bonus, g, att, cols, gain, bias, ones_bd, w_out, x, g_post, g_pre, tm=256):
    T = x.shape[0]
    row = pl.BlockSpec((tm, D_MODEL), lambda i: (i, 0))
    vec = pl.BlockSpec((1, D_MODEL), lambda i: (0, 0))
    return pl.pallas_call(
        _outproj_kernel,
        grid=(T // tm,),
        in_specs=[row, row, row, row, row,
                  pl.BlockSpec((tm, D_MODEL), lambda i: (i, CB_GR // 16)),
                  vec, vec, pl.BlockSpec((LANES, LANES), lambda i: (0, 0)),
                  pl.BlockSpec((D_MODEL, D_MODEL), lambda i: (0, 0), pipeline_mode=pl.Buffered(1)),
                  row, vec, vec],
        out_specs=[row, row],
        out_shape=[jax.ShapeDtypeStruct((T, D_MODEL), F32), jax.ShapeDtypeStruct((T, D_MODEL), BF16)],
        scratch_shapes=[pltpu.VMEM((tm, D_MODEL), BF16)],
        compiler_params=_cparams(("parallel",)),
        name="outproj",
    )(yf, yb, bonus, g, att, cols, gain, bias, ones_bd, w_out, x, g_post, g_pre)


def _ffn_kernel(hn_ref, wu_ref, wd_ref, h_ref, g_ref, o_ref, acc_ref):
    j = pl.program_id(1)

    @pl.when(j == 0)
    def _():
        acc_ref[...] = jnp.zeros_like(acc_ref)

    up = jnp.dot(hn_ref[...], wu_ref[...], preferred_element_type=F32)
    act = jnp.square(jnp.maximum(up, 0.0)).astype(BF16)
    acc_ref[...] += jnp.dot(act, wd_ref[...], preferred_element_type=F32)

    @pl.when(j == pl.num_programs(1) - 1)
    def _():
        o_ref[...] = h_ref[...] + _rms(acc_ref[...], g_ref[...])


def _ffn(hn, w_up, w_down, h, gain, tm=1024, tf=512):
    T = h.shape[0]
    row = pl.BlockSpec((tm, D_MODEL), lambda i, j: (i, 0))
    row_once = pl.BlockSpec((tm, D_MODEL), lambda i, j: (i, 0), pipeline_mode=pl.Buffered(1))
    return pl.pallas_call(
        _ffn_kernel,
        grid=(T // tm, D_FF // tf),
        in_specs=[row,
                  pl.BlockSpec((D_MODEL, tf), lambda i, j: (0, j)),
                  pl.BlockSpec((tf, D_MODEL), lambda i, j: (j, 0)),
                  row_once,
                  pl.BlockSpec((1, D_MODEL), lambda i, j: (0, 0))],
        out_specs=row_once,
        out_shape=jax.ShapeDtypeStruct((T, D_MODEL), F32),
        scratch_shapes=[pltpu.VMEM((tm, D_MODEL), F32)],
        compiler_params=_cparams(("parallel", "arbitrary")),
        name="ffn",
    )(hn, w_up, w_down, h, gain)


def _pad_cols(t, n):
    return jnp.pad(t, ((0, 0), (0, n - t.shape[1])))


def _pad_rows(t, n):
    return jnp.pad(t, ((0, n - t.shape[0]), (0, 0)))


def _split_cols(t, sizes):
    idx = [int(i) for i in np.cumsum(sizes)[:-1]]
    return jnp.split(t, idx, axis=-1)


def _permute_in_cols(t):
    shift_sizes = [RWKV_DIM] * 3 + [DECAY_LORA] * 2 + [ICLR_LORA] * 2 + [GATE_LORA]
    q, ak, av, rw, ga, gr = _split_cols(t, [D_MODEL, KV_COLS, KV_COLS, sum(shift_sizes), D_MODEL, D_MODEL])
    r, k, v, wdf, wdb, adf, adb, gd = _split_cols(rw, shift_sizes)
    lora = jnp.concatenate([_pad_cols(p, LORA_PAD) for p in (wdf, wdb, adf, adb)] + [gd], axis=1)
    return jnp.concatenate([q, ga, gr, r, k, v, ak, av, _pad_cols(lora, LORA_COLS)], axis=1)


def _rope_tables(T):
    pos = jnp.arange(T, dtype=F32)
    inv_freq = ROPE_THETA ** (-jnp.arange(0, ATT_HEAD_DIM, 2, dtype=F32) / ATT_HEAD_DIM)
    ang = pos[:, None] * inv_freq[None, :]
    cos, sin = jnp.cos(ang), jnp.sin(ang)
    return jnp.concatenate([cos, cos], axis=1), jnp.concatenate([-sin, sin], axis=1)


def kernel(x, norm_pre_mix, w_in, mu_shift, attn_sink, w0_fwd, w_up_fwd, w0_bwd, w_up_bwd, a0_fwd, a_up_fwd, a0_bwd, a_up_bwd, g_up, k_k, k_a, r_k, ln_x_gain, ln_x_bias, w_out, norm_post_mix, norm_pre_ffn, w_ffn_up, w_ffn_down, norm_post_ffn):
    B, T, _ = x.shape
    depth = w_in.shape[0]
    cos2, sin2 = _rope_tables(T)
    lane = np.arange(LANES)
    ones_bd = jnp.asarray((lane[:, None] // RWKV_HEAD_DIM) == (lane[None, :] // RWKV_HEAD_DIM), BF16)
    row = lambda t: t.reshape(1, -1)
    outs = []
    for bi in range(B):
        h = x[bi]
        for l in range(depth):
            w_perm = _permute_in_cols(w_in[l].astype(BF16))
            mu = mu_shift[l].reshape(1, -1)
            shift_sizes = [RWKV_DIM] * 3 + [DECAY_LORA] * 2 + [ICLR_LORA] * 2 + [GATE_LORA]
            mr, mk, mv, m1, m2, m3, m4, mg = _split_cols(mu, shift_sizes)
            mu_rkv = jnp.concatenate([mr, mk, mv], axis=1)
            mu_lora = _pad_cols(jnp.concatenate([_pad_cols(p, LORA_PAD) for p in (m1, m2, m3, m4)] + [mg], axis=1),
                                LORA_COLS)
            cols = _inproj(h, row(norm_pre_mix[l]), w_perm)
            att = _attention(cols, attn_sink[l], cos2, sin2)
            vecs = [row(t[l]) for t in (w0_fwd, w0_bwd, a0_fwd, a0_bwd, k_k, k_a, r_k)]
            r, v, kk, lwf, bf, kf, lwb, bb, kb, g, bonus = _rwkv_prep(
                cols, mu_rkv, mu_lora, vecs,
                *[_pad_rows(t[l], LORA_PAD).astype(BF16) for t in (w_up_fwd, w_up_bwd, a_up_fwd, a_up_bwd)],
                g_up[l].astype(BF16), ones_bd)
            yf, yb = _rwkv_scan(r, v, kk, lwf, bf, kf, lwb, bb, kb)
            h, hn = _outproj(yf, yb, bonus, g, att, cols, row(ln_x_gain[l]), row(ln_x_bias[l]), ones_bd,
                             w_out[l].astype(BF16), h, row(norm_post_mix[l]), row(norm_pre_ffn[l]))
            h = _ffn(hn, w_ffn_up[l].astype(BF16), w_ffn_down[l].astype(BF16), h, row(norm_post_ffn[l]))
        outs.append(h)
    return jnp.stack(outs, axis=0)
```

```python
import functools

import jax
import jax.numpy as jnp
import numpy as np
from jax import lax
from jax.experimental import pallas as pl
from jax.experimental.pallas import tpu as pltpu

F32 = jnp.float32
BF16 = jnp.bfloat16
LANES = 128
SUBLANES = 8
PACKED_SUBLANES = 16

D_MODEL = 2048
ATT_HEAD_DIM = 128
ATT_HEADS = 16
ATT_KV_HEADS = 4
ATT_GROUP = 4
WINDOW = 128
BLOCK = 128
ATT_Q_BLOCKS = 4
ROPE_THETA = 10000.0
RWKV_HEAD_DIM = 64
RWKV_DIM = 2048
N_PAIRS = RWKV_DIM // LANES
DECAY_LORA = 96
ICLR_LORA = 96
GATE_LORA = 256
LORA_PAD = 128
LORA_COLS = 1024
D_FF = 4 * D_MODEL
NORM_EPS = 1e-6
GN_EPS = 64e-5
MASK_VALUE = -1e30
DECAY_SCALE = float(np.exp(-0.5))
KV_COLS = ATT_KV_HEADS * ATT_HEAD_DIM

CB_Q, CB_GA, CB_GR, CB_R, CB_K, CB_V, CB_AK, CB_AV, CB_LORA = 0, 16, 32, 48, 64, 80, 96, 100, 104

CHUNK = 128
VMEM_LIMIT = 56 * 1024 * 1024

NN = (((1,), (0,)), ((), ()))
NT = (((1,), (1,)), ((), ()))
TN = (((0,), (0,)), ((), ()))


def _mm(a, b, dims=NN):
    return lax.dot_general(a.astype(BF16), b.astype(BF16), dims, preferred_element_type=F32)


def _split_bf16(x, parts):
    out = []
    for _ in range(parts - 1):
        hi = x.astype(BF16)
        out.append(hi)
        x = x - hi.astype(F32)
    out.append(x.astype(BF16))
    return out


def _mm_exact_lhs(a, b, parts):
    a = a.astype(BF16)
    acc = None
    for term in _split_bf16(b, parts):
        p = lax.dot_general(a, term, NN, preferred_element_type=F32)
        acc = p if acc is None else acc + p
    return acc


def _cparams(sem):
    return pltpu.CompilerParams(dimension_semantics=sem, vmem_limit_bytes=VMEM_LIMIT)


def _inproj_kernel(x_ref, g_ref, w_ref, o_ref, xn_ref):
    @pl.when(pl.program_id(1) == 0)
    def _():
        x = x_ref[...]
        ms = jnp.mean(x * x, axis=-1, keepdims=True)
        xn_ref[...] = (x * lax.rsqrt(ms + NORM_EPS) * g_ref[...]).astype(BF16)

    o_ref[...] = jnp.dot(xn_ref[...], w_ref[...], preferred_element_type=F32).astype(o_ref.dtype)


def _inproj(x, gain, w, tm=1024, tn=2048):
    T = x.shape[0]
    n = w.shape[1]
    return pl.pallas_call(
        _inproj_kernel,
        grid=(T // tm, n // tn),
        in_specs=[
            pl.BlockSpec((tm, D_MODEL), lambda i, j: (i, 0)),
            pl.BlockSpec((1, D_MODEL), lambda i, j: (0, 0)),
            pl.BlockSpec((D_MODEL, tn), lambda i, j: (0, j)),
        ],
        out_specs=pl.BlockSpec((tm, tn), lambda i, j: (i, j)),
        out_shape=jax.ShapeDtypeStruct((T, n), BF16),
        scratch_shapes=[pltpu.VMEM((tm, D_MODEL), BF16)],
        compiler_params=_cparams(("parallel", "arbitrary")),
        name="inproj",
    )(x, gain, w)


def _rope(x, c, s):
    return x * c + pltpu.roll(x, ATT_HEAD_DIM // 2, 1) * s


def _attn_kernel(sink_ref, q_ref, kp_ref, kc_ref, kn_ref, vp_ref, vc_ref, vn_ref, gate_ref,
                 cc_ref, sc_ref, cp_ref, sp_ref, cn_ref, sn_ref, o_ref, *, seq):
    i = pl.program_id(0)
    cc, sc = cc_ref[...], sc_ref[...]
    qi = lax.broadcasted_iota(jnp.int32, (BLOCK, 3 * BLOCK), 0)
    sj = lax.broadcasted_iota(jnp.int32, (BLOCK, 3 * BLOCK), 1)
    in_window = jnp.abs(sj - BLOCK - qi) <= WINDOW
    rowg = lax.broadcasted_iota(jnp.int32, (ATT_GROUP * BLOCK, 1), 0) // BLOCK
    scale = ATT_HEAD_DIM ** -0.5
    cos_all = jnp.concatenate([cp_ref[...], cc, cn_ref[...]], axis=0)
    sin_all = jnp.concatenate([sp_ref[...], sc, sn_ref[...]], axis=0)
    for g in range(ATT_KV_HEADS):
        ks = slice(g * ATT_HEAD_DIM, (g + 1) * ATT_HEAD_DIM)
        k_all = jnp.concatenate([kp_ref[:, ks], kc_ref[:, ks], kn_ref[:, ks]], axis=0).astype(F32)
        k_all = _rope(k_all, cos_all, sin_all).astype(BF16)
        v_all = jnp.concatenate([vp_ref[:, ks], vc_ref[:, ks], vn_ref[:, ks]], axis=0)
        heads = [g * ATT_GROUP + hh for hh in range(ATT_GROUP)]
        sink = jnp.zeros((ATT_GROUP * BLOCK, 1), F32)
        for hh, h in enumerate(heads):
            sink = jnp.where(rowg == hh, sink_ref[h], sink)
        for qb in range(ATT_Q_BLOCKS):
            rows = slice(qb * BLOCK, (qb + 1) * BLOCK)
            win = slice(qb * BLOCK, (qb + 3) * BLOCK)
            kpos = (i * ATT_Q_BLOCKS + qb - 1) * BLOCK + sj
            valid = in_window & (kpos >= 0) & (kpos < seq)
            valid4 = jnp.concatenate([valid] * ATT_GROUP, axis=0)
            q4 = jnp.concatenate(
                [_rope(q_ref[rows, h * ATT_HEAD_DIM:(h + 1) * ATT_HEAD_DIM].astype(F32), cc[rows], sc[rows])
                 for h in heads], axis=0) * scale
            s = _mm(q4, k_all[win], NT)
            s = jnp.where(valid4, s, MASK_VALUE)
            m = jnp.maximum(jnp.max(s, axis=-1, keepdims=True), sink)
            p = jnp.exp(s - m)
            den = jnp.sum(p, axis=-1, keepdims=True) + jnp.exp(sink - m)
            o = _mm(p, v_all[win]) / den
            for hh, h in enumerate(heads):
                cs = slice(h * ATT_HEAD_DIM, (h + 1) * ATT_HEAD_DIM)
                o_ref[rows, cs] = (o[hh * BLOCK:(hh + 1) * BLOCK]
                                   * jax.nn.sigmoid(gate_ref[rows, cs].astype(F32))).astype(o_ref.dtype)


def _attention(cols, sink, cos2, sin2):
    T = cols.shape[0]
    Q = ATT_Q_BLOCKS
    nb = T // BLOCK
    prev = lambda i: (jnp.maximum(i * Q - 1, 0), 0)
    cur = lambda i: (i, 0)
    nxt = lambda i: (jnp.minimum((i + 1) * Q, nb - 1), 0)
    kcb, vcb = CB_AK * LANES // KV_COLS, CB_AV * LANES // KV_COLS
    col = lambda f, cb: (lambda i: (f(i)[0], cb))
    halo = lambda width, f, cb: pl.BlockSpec((BLOCK, width), col(f, cb))
    main = lambda width, cb: pl.BlockSpec((Q * BLOCK, width), col(cur, cb))
    return pl.pallas_call(
        functools.partial(_attn_kernel, seq=T),
        grid=(nb // Q,),
        in_specs=[
            pl.BlockSpec(memory_space=pltpu.SMEM),
            main(D_MODEL, CB_Q // 16),
            halo(KV_COLS, prev, kcb), main(KV_COLS, kcb), halo(KV_COLS, nxt, kcb),
            halo(KV_COLS, prev, vcb), main(KV_COLS, vcb), halo(KV_COLS, nxt, vcb),
            main(D_MODEL, CB_GA // 16),
            main(LANES, 0), main(LANES, 0),
            halo(LANES, prev, 0), halo(LANES, prev, 0),
            halo(LANES, nxt, 0), halo(LANES, nxt, 0),
        ],
        out_specs=main(D_MODEL, 0),
        out_shape=jax.ShapeDtypeStruct((T, D_MODEL), BF16),
        compiler_params=_cparams(("parallel",)),
        name="attention",
    )(sink, cols, cols, cols, cols, cols, cols, cols, cols, cos2, sin2, cos2, sin2, cos2, sin2)


def _head_sum(x, ones_bd):
    return _mm(x, ones_bd)


def _prep_kernel(r_ref, rp_ref, rn_ref, k_ref, kp_ref, kn_ref, v_ref, vp_ref, vn_ref,
                 lo_ref, lop_ref, lon_ref, mur_ref, muk_ref, muv_ref, mul_ref,
                 w0f_ref, w0b_ref, a0f_ref, a0b_ref, kk_ref, ka_ref, rk_ref,
                 wuf_ref, wub_ref, auf_ref, aub_ref, gup_ref, bd_ref,
                 r_o, v_o, kk_o, lwf_o, bf_o, kf_o, lwb_o, bb_o, kb_o, g_o, bonus_o, act_ref):
    i = pl.program_id(0)
    first = i == 0
    last = i == pl.num_programs(0) - 1
    tm = r_ref.shape[0]

    def shift(c_ref, p_ref, n_ref, mu):
        c = c_ref[...].astype(F32)
        row = lax.broadcasted_iota(jnp.int32, c.shape, 0)
        prow = jnp.where(first, 0.0, p_ref[PACKED_SUBLANES - 1:PACKED_SUBLANES, :].astype(F32))
        nrow = jnp.where(last, 0.0, n_ref[0:1, :].astype(F32))
        prev = jnp.where(row == 0, prow, pltpu.roll(c, 1, 0))
        nxt = jnp.where(row == tm - 1, nrow, pltpu.roll(c, tm - 1, 0))
        return c * (1.0 - mu) + (prev + nxt) * (0.5 * mu)

    r = shift(r_ref, rp_ref, rn_ref, mur_ref[...])
    k = shift(k_ref, kp_ref, kn_ref, muk_ref[...])
    v = shift(v_ref, vp_ref, vn_ref, muv_ref[...])
    P = LORA_PAD

    @pl.when(pl.program_id(1) == 0)
    def _():
        lo = shift(lo_ref, lop_ref, lon_ref, mul_ref[...])
        act_ref[:, 0:2 * P] = jnp.tanh(lo[:, 0:2 * P]).astype(BF16)
        act_ref[:, 2 * P:4 * P] = lo[:, 2 * P:4 * P].astype(BF16)
        act_ref[:, 4 * P:] = jax.nn.sigmoid(lo[:, 4 * P:4 * P + GATE_LORA]).astype(BF16)

    ones_bd = bd_ref[...]

    kk = k * kk_ref[...]
    kk = kk * lax.rsqrt(jnp.maximum(_head_sum(kk * kk, ones_bd), 1e-24))
    k_a = ka_ref[...]

    def direction(wd, ad, w0, wu, a0, au):
        lw = -DECAY_SCALE * jax.nn.sigmoid(w0 + _mm(wd, wu))
        a = jax.nn.sigmoid(a0 + _mm(ad, au))
        k_mod = k * (1.0 + (a - 1.0) * k_a)
        return lw, kk * a, k_mod

    lwf, bf, kf = direction(act_ref[:, 0:P], act_ref[:, 2 * P:3 * P],
                            w0f_ref[...], wuf_ref[...], a0f_ref[...], auf_ref[...])
    lwb, bb, kb = direction(act_ref[:, P:2 * P], act_ref[:, 3 * P:4 * P],
                            w0b_ref[...], wub_ref[...], a0b_ref[...], aub_ref[...])
    g = _mm(act_ref[:, 4 * P:], gup_ref[...])
    bonus = _head_sum(r * (0.5 * (kf + kb)) * rk_ref[...], ones_bd) * v

    for ref, val in ((r_o, r), (v_o, v), (kk_o, kk), (lwf_o, lwf), (bf_o, bf), (kf_o, kf),
                     (lwb_o, lwb), (bb_o, bb), (kb_o, kb), (g_o, g), (bonus_o, bonus)):
        ref[...] = val.astype(ref.dtype)


def _rwkv_prep(cols, mu_rkv, mu_lora, vecs, w_up_f, w_up_b, a_up_f, a_up_b, g_up, ones_bd, tm=2048):
    T = cols.shape[0]
    nt = T // tm
    halo = PACKED_SUBLANES
    hb = tm // halo
    nhb = T // halo
    main = lambda cb: pl.BlockSpec((tm, LANES), lambda i, j: (i, cb + j))
    hprev = lambda cb: pl.BlockSpec((halo, LANES), lambda i, j: (jnp.maximum(i * hb - 1, 0), cb + j))
    hnext = lambda cb: pl.BlockSpec((halo, LANES), lambda i, j: (jnp.minimum((i + 1) * hb, nhb - 1), cb + j))
    lcb = CB_LORA * LANES // LORA_COLS
    vec = lambda off: pl.BlockSpec((1, LANES), lambda i, j: (0, off + j))
    up = lambda rows: pl.BlockSpec((rows, LANES), lambda i, j: (0, j))
    in_specs = []
    for cb in (CB_R, CB_K, CB_V):
        in_specs += [main(cb), hprev(cb), hnext(cb)]
    in_specs += [
        pl.BlockSpec((tm, LORA_COLS), lambda i, j: (i, lcb)),
        pl.BlockSpec((halo, LORA_COLS), lambda i, j: (jnp.maximum(i * hb - 1, 0), lcb)),
        pl.BlockSpec((halo, LORA_COLS), lambda i, j: (jnp.minimum((i + 1) * hb, nhb - 1), lcb)),
        vec(0), vec(N_PAIRS), vec(2 * N_PAIRS),
        pl.BlockSpec((1, LORA_COLS), lambda i, j: (0, 0)),
    ]
    in_specs += [vec(0)] * 7
    in_specs += [up(LORA_PAD)] * 4 + [up(GATE_LORA)]
    in_specs += [pl.BlockSpec((LANES, LANES), lambda i, j: (0, 0))]
    dtypes = [BF16, BF16, BF16, F32, BF16, BF16, F32, BF16, BF16, BF16, BF16]
    return pl.pallas_call(
        _prep_kernel,
        grid=(nt, N_PAIRS),
        in_specs=in_specs,
        out_specs=[pl.BlockSpec((tm, LANES), lambda i, j: (i, j))] * 11,
        out_shape=[jax.ShapeDtypeStruct((T, RWKV_DIM), dt) for dt in dtypes],
        scratch_shapes=[pltpu.VMEM((tm, 4 * LORA_PAD + GATE_LORA), BF16)],
        compiler_params=_cparams(("parallel", "arbitrary")),
        name="rwkv_prep",
    )(cols, cols, cols, cols, cols, cols, cols, cols, cols, cols, cols, cols,
      mu_rkv, mu_rkv, mu_rkv, mu_lora, *vecs, w_up_f, w_up_b, a_up_f, a_up_b, g_up, ones_bd)


def _tri_inverse_all(mats, upper):
    L = mats[0].shape[0]
    row = lax.broadcasted_iota(jnp.int32, (L, L), 0)
    col = lax.broadcasted_iota(jnp.int32, (L, L), 1)
    hrow = lax.broadcasted_iota(jnp.int32, (L // 2, L), 0)
    hcol = lax.broadcasted_iota(jnp.int32, (L // 2, L), 1)
    same = lambda sh: (row >> sh) == (col >> sh)
    eye = jnp.where(row == col, 1.0, 0.0)
    ds = [eye + jnp.where(same(1), a, 0.0) for a in mats]
    sh = 1
    while (1 << sh) < L:
        m = 1 << sh
        dbs = [d.astype(BF16) for d in ds]
        if m < SUBLANES:
            level = same(sh + 1) & jnp.logical_not(same(sh))
            ts = [_mm(jnp.where(level, a, 0.0), db) for a, db in zip(mats, dbs)]
            ds = [d + _mm(db, t) for d, db, t in zip(ds, dbs, ts)]
        else:
            n_half = L // (2 * m)
            part = [slice(q * m, (q + 1) * m) for q in range(n_half)]
            blocks = lambda x: [x[q * m:(q + 1) * m] for q in range(2 * n_half)]
            zero = jnp.zeros((m, L), F32)
            act = lambda x, up: jnp.concatenate(blocks(x)[(0 if up else 1)::2], axis=0)
            level = {False: (hcol >> sh) == 2 * (hrow >> sh), True: (hcol >> sh) == 2 * (hrow >> sh) + 1}

            def spread(x, up, rest=None):
                out = []
                for q in range(n_half):
                    other = zero if rest is None else rest[2 * q + (1 if up else 0)]
                    out += [x[part[q]], other] if up else [other, x[part[q]]]
                return jnp.concatenate(out, axis=0)

            ts = [_mm(jnp.where(level[up], act(a, up), 0.0), db) for a, db, up in zip(mats, dbs, upper)]
            upd = [_mm(act(d, up), spread(t, up)) for d, t, up in zip(ds, ts, upper)]
            ds = [spread(act(d, up) + u, up, rest=blocks(d)) for d, u, up in zip(ds, upd, upper)]
        sh += 1
    return ds


def _chunk_all(insts):
    L = insts[0][0].shape[0]
    n_inst = len(insts)
    row = lax.broadcasted_iota(jnp.int32, (L, L), 0)
    col = lax.broadcasted_iota(jnp.int32, (L, L), 1)
    incl = {False: col <= row, True: col >= row}
    strict = {False: col < row, True: col > row}
    ones = {rev: jnp.where(incl[rev], 1.0, 0.0) for rev in (False, True)}
    incl2 = {rev: jnp.concatenate([incl[rev]] * 2, axis=1) for rev in (False, True)}
    strict2 = {rev: jnp.concatenate([strict[rev]] * 2, axis=1) for rev in (False, True)}
    lane = lax.broadcasted_iota(jnp.int32, (1, LANES), 1)
    h0 = lane < RWKV_HEAD_DIM
    srow = lax.broadcasted_iota(jnp.int32, (LANES, LANES), 0)
    scol = lax.broadcasted_iota(jnp.int32, (LANES, LANES), 1)
    same_head = (srow >= RWKV_HEAD_DIM) == (scol >= RWKV_HEAD_DIM)

    cums = [_mm_exact_lhs(ones[inst[7]], inst[3], 2) for inst in insts]
    pre = []
    for (r, v, kk, lw, b, k, s, rev), cum in zip(insts, cums):
        tot = cum[0:1] if rev else cum[L - 1:L]
        mid = L // 2 if rev else L // 2 - 1
        rho = cum[mid:mid + 1]
        e1 = jnp.exp(cum - rho)
        e2 = jnp.exp(rho - cum)
        er = jnp.exp(rho)
        et = jnp.exp(tot - rho)
        a_t = -kk * e1 * jnp.exp(-lw)
        r_t = r * e1
        b_t = b * e2
        k_t = k * e2
        pre.append((a_t, r_t, b_t, k_t, er, et))
    grams = []
    for a_t, r_t, b_t, k_t, _, _ in pre:
        zero = jnp.zeros_like(a_t)
        lhs = jnp.concatenate([jnp.where(h0, a_t, zero), jnp.where(h0, zero, a_t),
                               jnp.where(h0, r_t, zero), jnp.where(h0, zero, r_t)], axis=0)
        rhs = jnp.concatenate([b_t, k_t], axis=0)
        grams.append(_mm(lhs, rhs, NT))
    a_bk, r_bk = [], []
    for inst, gram in zip(insts, grams):
        rev = inst[7]
        a_bk.append([jnp.where(strict2[rev], gram[hh * L:(hh + 1) * L], 0.0) for hh in range(2)])
        r_bk.append([jnp.where(incl2[rev], gram[(2 + hh) * L:(3 + hh) * L], 0.0) for hh in range(2)])
    t_inv = _tri_inverse_all([a_bk[n][hh][:, :L] for n in range(n_inst) for hh in range(2)],
                             [insts[n][7] for n in range(n_inst) for hh in range(2)])
    xs = [_mm(jnp.concatenate([a_t * er, r_t * er], axis=0), inst[6], NT)
          for inst, (a_t, r_t, _, _, er, _) in zip(insts, pre)]
    pick = lambda t2: jnp.where(h0, t2[:L], t2[L:])
    akv = [_mm(jnp.concatenate([a_bk[n][0][:, L:], a_bk[n][1][:, L:]], axis=0), inst[1])
           for n, inst in enumerate(insts)]
    x = [xs[n][:L] + pick(akv[n]) for n in range(n_inst)]
    u = [pick(_mm(jnp.concatenate([t_inv[2 * n], t_inv[2 * n + 1]], axis=0), x[n])) for n in range(n_inst)]
    z = [jnp.concatenate([u[n], inst[1]], axis=0) for n, inst in enumerate(insts)]
    out = []
    for n, inst in enumerate(insts):
        _, _, b_t, k_t, er, et = pre[n]
        y = xs[n][L:] + pick(_mm(jnp.concatenate([r_bk[n][0], r_bk[n][1]], axis=0), z[n]))
        w = jnp.concatenate([b_t * et, k_t * et], axis=0)
        s_new = inst[6] * (er * et) + jnp.where(same_head, _mm(z[n], w, TN), 0.0)
        out.append((y, s_new))
    return out


def _scan_kernel(rf_ref, vf_ref, kkf_ref, lwf_ref, bf_ref, kf_ref,
                 rb_ref, vb_ref, kkb_ref, lwb_ref, bb_ref, kb_ref,
                 yf_ref, yb_ref, s_ref):
    @pl.when(pl.program_id(1) == 0)
    def _():
        s_ref[...] = jnp.zeros_like(s_ref)

    n_pairs = s_ref.shape[1]
    insts = []
    for p in range(n_pairs):
        cs = slice(p * LANES, (p + 1) * LANES)
        insts.append(tuple(ref[:, cs].astype(F32) for ref in (rf_ref, vf_ref, kkf_ref, lwf_ref, bf_ref, kf_ref))
                     + (s_ref[0, p], False))
        insts.append(tuple(ref[:, cs].astype(F32) for ref in (rb_ref, vb_ref, kkb_ref, lwb_ref, bb_ref, kb_ref))
                     + (s_ref[1, p], True))
    res = _chunk_all(insts)
    for p in range(n_pairs):
        cs = slice(p * LANES, (p + 1) * LANES)
        yf_ref[:, cs], s_ref[0, p] = res[2 * p]
        yb_ref[:, cs], s_ref[1, p] = res[2 * p + 1]


def _rwkv_scan(r, v, kk, lwf, bf, kf, lwb, bb, kb, pairs_per_step=16):
    T = r.shape[0]
    nc = T // CHUNK
    width = pairs_per_step * LANES
    fwd = pl.BlockSpec((CHUNK, width), lambda p, c: (c, p))
    bwd = pl.BlockSpec((CHUNK, width), lambda p, c: (nc - 1 - c, p))
    out = jax.ShapeDtypeStruct((T, RWKV_DIM), F32)
    return pl.pallas_call(
        _scan_kernel,
        grid=(N_PAIRS // pairs_per_step, nc),
        in_specs=[fwd] * 6 + [bwd] * 6,
        out_specs=[fwd, bwd],
        out_shape=[out, out],
        scratch_shapes=[pltpu.VMEM((2, pairs_per_step, LANES, LANES), F32)],
        compiler_params=_cparams(("parallel", "arbitrary")),
        name="rwkv_scan",
    )(r, v, kk, lwf, bf, kf, r, v, kk, lwb, bb, kb)


def _rms(x, gain):
    return x * lax.rsqrt(jnp.mean(x * x, axis=-1, keepdims=True) + NORM_EPS) * gain


def _outproj_kernel(yf_ref, yb_ref, bonus_ref, g_ref, att_ref, gr_ref, gain_ref, bias_ref, bd_ref,
                    w_ref, x_ref, gpost_ref, gpre_ref, h_ref, hn_ref, m_ref):
    ones_bd = bd_ref[...]
    inv_n = 1.0 / RWKV_HEAD_DIM
    for j in range(N_PAIRS):
        cs = slice(j * LANES, (j + 1) * LANES)
        y = yf_ref[:, cs] + yb_ref[:, cs]
        mean = _head_sum(y, ones_bd) * inv_n
        d = y - mean
        var = _head_sum(d * d, ones_bd) * inv_n
        yn = d * lax.rsqrt(var + GN_EPS) * gain_ref[:, cs] + bias_ref[:, cs]
        o_rwkv = (yn + bonus_ref[:, cs].astype(F32)) * g_ref[:, cs].astype(F32)
        merged = att_ref[:, cs].astype(F32) + jax.nn.sigmoid(gr_ref[:, cs].astype(F32)) * o_rwkv
        m_ref[:, cs] = merged.astype(BF16)
    mix = jnp.dot(m_ref[...], w_ref[...], preferred_element_type=F32)
    h = x_ref[...] + _rms(mix, gpost_ref[...])
    h_ref[...] = h
    hn_ref[...] = _rms(h, gpre_ref[...]).astype(BF16)


def _outproj(yf, yb, bonus, g, att, cols, gain, bias, ones_bd, w_out, x, g_post, g_pre, tm=256):
    T = x.shape[0]
    row = pl.BlockSpec((tm, D_MODEL), lambda i: (i, 0))
    vec = pl.BlockSpec((1, D_MODEL), lambda i: (0, 0))
    return pl.pallas_call(
        _outproj_kernel,
        grid=(T // tm,),
        in_specs=[row, row, row, row, row,
                  pl.BlockSpec((tm, D_MODEL), lambda i: (i, CB_GR // 16)),
                  vec, vec, pl.BlockSpec((LANES, LANES), lambda i: (0, 0)),
                  pl.BlockSpec((D_MODEL, D_MODEL), lambda i: (0, 0), pipeline_mode=pl.Buffered(1)),
                  row, vec, vec],
        out_specs=[row, row],
        out_shape=[jax.ShapeDtypeStruct((T, D_MODEL), F32), jax.ShapeDtypeStruct((T, D_MODEL), BF16)],
        scratch_shapes=[pltpu.VMEM((tm, D_MODEL), BF16)],
        compiler_params=_cparams(("parallel",)),
        name="outproj",
    )(yf, yb, bonus, g, att, cols, gain, bias, ones_bd, w_out, x, g_post, g_pre)


def _ffn_kernel(hn_ref, wu_ref, wd_ref, h_ref, g_ref, o_ref, acc_ref):
    j = pl.program_id(1)

    @pl.when(j == 0)
    def _():
        acc_ref[...] = jnp.zeros_like(acc_ref)

    up = jnp.dot(hn_ref[...], wu_ref[...], preferred_element_type=F32)
    act = jnp.square(jnp.maximum(up, 0.0)).astype(BF16)
    acc_ref[...] += jnp.dot(act, wd_ref[...], preferred_element_type=F32)

    @pl.when(j == pl.num_programs(1) - 1)
    def _():
        o_ref[...] = h_ref[...] + _rms(acc_ref[...], g_ref[...])


def _ffn(hn, w_up, w_down, h, gain, tm=1024, tf=512):
    T = h.shape[0]
    row = pl.BlockSpec((tm, D_MODEL), lambda i, j: (i, 0))
    row_once = pl.BlockSpec((tm, D_MODEL), lambda i, j: (i, 0), pipeline_mode=pl.Buffered(1))
    return pl.pallas_call(
        _ffn_kernel,
        grid=(T // tm, D_FF // tf),
        in_specs=[row,
                  pl.BlockSpec((D_MODEL, tf), lambda i, j: (0, j)),
                  pl.BlockSpec((tf, D_MODEL), lambda i, j: (j, 0)),
                  row_once,
                  pl.BlockSpec((1, D_MODEL), lambda i, j: (0, 0))],
        out_specs=row_once,
        out_shape=jax.ShapeDtypeStruct((T, D_MODEL), F32),
        scratch_shapes=[pltpu.VMEM((tm, D_MODEL), F32)],
        compiler_params=_cparams(("parallel", "arbitrary")),
        name="ffn",
    )(hn, w_up, w_down, h, gain)


def _pad_cols(t, n):
    return jnp.pad(t, ((0, 0), (0, n - t.shape[1])))


def _pad_rows(t, n):
    return jnp.pad(t, ((0, n - t.shape[0]), (0, 0)))


def _split_cols(t, sizes):
    idx = [int(i) for i in np.cumsum(sizes)[:-1]]
    return jnp.split(t, idx, axis=-1)


def _permute_in_cols(t):
    shift_sizes = [RWKV_DIM] * 3 + [DECAY_LORA] * 2 + [ICLR_LORA] * 2 + [GATE_LORA]
    q, ak, av, rw, ga, gr = _split_cols(t, [D_MODEL, KV_COLS, KV_COLS, sum(shift_sizes), D_MODEL, D_MODEL])
    r, k, v, wdf, wdb, adf, adb, gd = _split_cols(rw, shift_sizes)
    lora = jnp.concatenate([_pad_cols(p, LORA_PAD) for p in (wdf, wdb, adf, adb)] + [gd], axis=1)
    return jnp.concatenate([q, ga, gr, r, k, v, ak, av, _pad_cols(lora, LORA_COLS)], axis=1)


def _rope_tables(T):
    inv_freq = ROPE_THETA ** (-jnp.arange(0, ATT_HEAD_DIM, 2, dtype=F32) / ATT_HEAD_DIM)
    hi = (jnp.arange(T // BLOCK, dtype=F32) * BLOCK)[:, None, None] * inv_freq
    lo = jnp.arange(BLOCK, dtype=F32)[None, :, None] * inv_freq
    cos = (jnp.cos(hi) * jnp.cos(lo) - jnp.sin(hi) * jnp.sin(lo)).reshape(T, -1)
    sin = (jnp.sin(hi) * jnp.cos(lo) + jnp.cos(hi) * jnp.sin(lo)).reshape(T, -1)
    return jnp.concatenate([cos, cos], axis=1), jnp.concatenate([-sin, sin], axis=1)


def kernel(x, norm_pre_mix, w_in, mu_shift, attn_sink, w0_fwd, w_up_fwd, w0_bwd, w_up_bwd, a0_fwd, a_up_fwd, a0_bwd, a_up_bwd, g_up, k_k, k_a, r_k, ln_x_gain, ln_x_bias, w_out, norm_post_mix, norm_pre_ffn, w_ffn_up, w_ffn_down, norm_post_ffn):
    B, T, _ = x.shape
    depth = w_in.shape[0]
    cos2, sin2 = _rope_tables(T)
    lane = np.arange(LANES)
    ones_bd = jnp.asarray((lane[:, None] // RWKV_HEAD_DIM) == (lane[None, :] // RWKV_HEAD_DIM), BF16)
    row = lambda t: t.reshape(1, -1)
    outs = []
    for bi in range(B):
        h = x[bi]
        for l in range(depth):
            w_perm = _permute_in_cols(w_in[l].astype(BF16))
            mu = mu_shift[l].reshape(1, -1)
            shift_sizes = [RWKV_DIM] * 3 + [DECAY_LORA] * 2 + [ICLR_LORA] * 2 + [GATE_LORA]
            mr, mk, mv, m1, m2, m3, m4, mg = _split_cols(mu, shift_sizes)
            mu_rkv = jnp.concatenate([mr, mk, mv], axis=1)
            mu_lora = _pad_cols(jnp.concatenate([_pad_cols(p, LORA_PAD) for p in (m1, m2, m3, m4)] + [mg], axis=1),
                                LORA_COLS)
            cols = _inproj(h, row(norm_pre_mix[l]), w_perm)
            att = _attention(cols, attn_sink[l], cos2, sin2)
            vecs = [row(t[l]) for t in (w0_fwd, w0_bwd, a0_fwd, a0_bwd, k_k, k_a, r_k)]
            r, v, kk, lwf, bf, kf, lwb, bb, kb, g, bonus = _rwkv_prep(
                cols, mu_rkv, mu_lora, vecs,
                *[_pad_rows(t[l], LORA_PAD).astype(BF16) for t in (w_up_fwd, w_up_bwd, a_up_fwd, a_up_bwd)],
                g_up[l].astype(BF16), ones_bd)
            yf, yb = _rwkv_scan(r, v, kk, lwf, bf, kf, lwb, bb, kb)
            h, hn = _outproj(yf, yb, bonus, g, att, cols, row(ln_x_gain[l]), row(ln_x_bias[l]), ones_bd,
                             w_out[l].astype(BF16), h, row(norm_post_mix[l]), row(norm_pre_ffn[l]))
            h = _ffn(hn, w_ffn_up[l].astype(BF16), w_ffn_down[l].astype(BF16), h, row(norm_post_ffn[l]))
        outs.append(h)
    return jnp.stack(outs, axis=0)
```

```python
import functools

import jax
import jax.numpy as jnp
import numpy as np
from jax import lax
from jax.experimental import pallas as pl
from jax.experimental.pallas import tpu as pltpu

F32 = jnp.float32
BF16 = jnp.bfloat16
LANES = 128
SUBLANES = 8
PACKED_SUBLANES = 16

D_MODEL = 2048
ATT_HEAD_DIM = 128
ATT_KV_HEADS = 4
ATT_GROUP = 4
WINDOW = 128
BLOCK = 128
ATT_Q_BLOCKS = 4
ROPE_THETA = 10000.0
RWKV_HEAD_DIM = 64
RWKV_DIM = 2048
N_PAIRS = RWKV_DIM // LANES
DECAY_LORA = 96
ICLR_LORA = 96
GATE_LORA = 256
LORA_PAD = 128
LORA_COLS = 1024
D_FF = 4 * D_MODEL
NORM_EPS = 1e-6
GN_EPS = 64e-5
MASK_VALUE = -1e30
DECAY_SCALE = float(np.exp(-0.5))
KV_COLS = ATT_KV_HEADS * ATT_HEAD_DIM

CB_Q, CB_GA, CB_GR, CB_R, CB_K, CB_V, CB_AK, CB_AV, CB_LORA = 0, 16, 32, 48, 64, 80, 96, 100, 104

CHUNK = 128
VMEM_LIMIT = 56 * 1024 * 1024

NN = (((1,), (0,)), ((), ()))
NT = (((1,), (1,)), ((), ()))
TN = (((0,), (0,)), ((), ()))


def _mm(a, b, dims=NN):
    return lax.dot_general(a.astype(BF16), b.astype(BF16), dims, preferred_element_type=F32)


def _split_bf16(x, parts):
    out = []
    for _ in range(parts - 1):
        hi = x.astype(BF16)
        out.append(hi)
        x = x - hi.astype(F32)
    out.append(x.astype(BF16))
    return out


def _mm_exact_lhs(a, b, parts):
    a = a.astype(BF16)
    acc = None
    for term in _split_bf16(b, parts):
        p = lax.dot_general(a, term, NN, preferred_element_type=F32)
        acc = p if acc is None else acc + p
    return acc


def _sigmoid(x):
    return 0.5 * jnp.tanh(0.5 * x) + 0.5


def _cparams(sem):
    return pltpu.CompilerParams(dimension_semantics=sem, vmem_limit_bytes=VMEM_LIMIT)


def _inproj_kernel(x_ref, g_ref, w_ref, o_ref, xn_ref):
    @pl.when(pl.program_id(1) == 0)
    def _():
        x = x_ref[...]
        ms = jnp.mean(x * x, axis=-1, keepdims=True)
        xn_ref[...] = (x * lax.rsqrt(ms + NORM_EPS) * g_ref[...]).astype(BF16)

    o_ref[...] = jnp.dot(xn_ref[...], w_ref[...], preferred_element_type=F32).astype(o_ref.dtype)


def _inproj(x, gain, w, tm=1024, tn=2048):
    T = x.shape[0]
    n = w.shape[1]
    return pl.pallas_call(
        _inproj_kernel,
        grid=(T // tm, n // tn),
        in_specs=[
            pl.BlockSpec((tm, D_MODEL), lambda i, j: (i, 0)),
            pl.BlockSpec((1, D_MODEL), lambda i, j: (0, 0)),
            pl.BlockSpec((D_MODEL, tn), lambda i, j: (0, j)),
        ],
        out_specs=pl.BlockSpec((tm, tn), lambda i, j: (i, j)),
        out_shape=jax.ShapeDtypeStruct((T, n), BF16),
        scratch_shapes=[pltpu.VMEM((tm, D_MODEL), BF16)],
        compiler_params=_cparams(("parallel", "arbitrary")),
        name="inproj",
    )(x, gain, w)


def _rope(x, c, s):
    return x * c + pltpu.roll(x, ATT_HEAD_DIM // 2, 1) * s


def _attn_kernel(sink_ref, q_ref, kp_ref, kc_ref, kn_ref, vp_ref, vc_ref, vn_ref, gate_ref,
                 cc_ref, sc_ref, cp_ref, sp_ref, cn_ref, sn_ref, o_ref, *, seq):
    i = pl.program_id(0)
    cc, sc = cc_ref[...], sc_ref[...]
    qi = lax.broadcasted_iota(jnp.int32, (BLOCK, 3 * BLOCK), 0)
    sj = lax.broadcasted_iota(jnp.int32, (BLOCK, 3 * BLOCK), 1)
    in_window = jnp.abs(sj - BLOCK - qi) <= WINDOW
    rowg = lax.broadcasted_iota(jnp.int32, (ATT_GROUP * BLOCK, 1), 0) // BLOCK
    scale = ATT_HEAD_DIM ** -0.5
    cos_all = jnp.concatenate([cp_ref[...], cc, cn_ref[...]], axis=0)
    sin_all = jnp.concatenate([sp_ref[...], sc, sn_ref[...]], axis=0)
    for g in range(ATT_KV_HEADS):
        ks = slice(g * ATT_HEAD_DIM, (g + 1) * ATT_HEAD_DIM)
        k_all = jnp.concatenate([kp_ref[:, ks], kc_ref[:, ks], kn_ref[:, ks]], axis=0).astype(F32)
        k_all = _rope(k_all, cos_all, sin_all).astype(BF16)
        v_all = jnp.concatenate([vp_ref[:, ks], vc_ref[:, ks], vn_ref[:, ks]], axis=0)
        heads = [g * ATT_GROUP + hh for hh in range(ATT_GROUP)]
        sink = jnp.zeros((ATT_GROUP * BLOCK, 1), F32)
        for hh, h in enumerate(heads):
            sink = jnp.where(rowg == hh, sink_ref[h], sink)
        for qb in range(ATT_Q_BLOCKS):
            rows = slice(qb * BLOCK, (qb + 1) * BLOCK)
            win = slice(qb * BLOCK, (qb + 3) * BLOCK)
            kpos = (i * ATT_Q_BLOCKS + qb - 1) * BLOCK + sj
            valid = in_window & (kpos >= 0) & (kpos < seq)
            valid4 = jnp.concatenate([valid] * ATT_GROUP, axis=0)
            q4 = jnp.concatenate(
                [_rope(q_ref[rows, h * ATT_HEAD_DIM:(h + 1) * ATT_HEAD_DIM].astype(F32), cc[rows], sc[rows])
                 for h in heads], axis=0) * scale
            s = _mm(q4, k_all[win], NT)
            s = jnp.where(valid4, s, MASK_VALUE)
            m = jnp.maximum(jnp.max(s, axis=-1, keepdims=True), sink)
            p = jnp.exp(s - m)
            den = jnp.sum(p, axis=-1, keepdims=True) + jnp.exp(sink - m)
            o = _mm(p, v_all[win]) / den
            for hh, h in enumerate(heads):
                cs = slice(h * ATT_HEAD_DIM, (h + 1) * ATT_HEAD_DIM)
                o_ref[rows, cs] = (o[hh * BLOCK:(hh + 1) * BLOCK]
                                   * _sigmoid(gate_ref[rows, cs].astype(F32))).astype(o_ref.dtype)


def _attention(cols, sink, cos2, sin2):
    T = cols.shape[0]
    Q = ATT_Q_BLOCKS
    nb = T // BLOCK
    prev = lambda i: (jnp.maximum(i * Q - 1, 0), 0)
    cur = lambda i: (i, 0)
    nxt = lambda i: (jnp.minimum((i + 1) * Q, nb - 1), 0)
    kcb, vcb = CB_AK * LANES // KV_COLS, CB_AV * LANES // KV_COLS
    col = lambda f, cb: (lambda i: (f(i)[0], cb))
    halo = lambda width, f, cb: pl.BlockSpec((BLOCK, width), col(f, cb))
    main = lambda width, cb: pl.BlockSpec((Q * BLOCK, width), col(cur, cb))
    return pl.pallas_call(
        functools.partial(_attn_kernel, seq=T),
        grid=(nb // Q,),
        in_specs=[
            pl.BlockSpec(memory_space=pltpu.SMEM),
            main(D_MODEL, CB_Q // 16),
            halo(KV_COLS, prev, kcb), main(KV_COLS, kcb), halo(KV_COLS, nxt, kcb),
            halo(KV_COLS, prev, vcb), main(KV_COLS, vcb), halo(KV_COLS, nxt, vcb),
            main(D_MODEL, CB_GA // 16),
            main(LANES, 0), main(LANES, 0),
            halo(LANES, prev, 0), halo(LANES, prev, 0),
            halo(LANES, nxt, 0), halo(LANES, nxt, 0),
        ],
        out_specs=main(D_MODEL, 0),
        out_shape=jax.ShapeDtypeStruct((T, D_MODEL), BF16),
        compiler_params=_cparams(("parallel",)),
        name="attention",
    )(sink, cols, cols, cols, cols, cols, cols, cols, cols, cos2, sin2, cos2, sin2, cos2, sin2)


def _head_sum(x, ones_bd):
    return _mm(x, ones_bd)


def _prep_kernel(r_ref, rp_ref, rn_ref, k_ref, kp_ref, kn_ref, v_ref, vp_ref, vn_ref,
                 lo_ref, lop_ref, lon_ref, mur_ref, muk_ref, muv_ref, mul_ref,
                 w0f_ref, w0b_ref, a0f_ref, a0b_ref, kk_ref, ka_ref, rk_ref,
                 wuf_ref, wub_ref, auf_ref, aub_ref, gup_ref, bd_ref,
                 r_o, v_o, kk_o, lwf_o, bf_o, kf_o, lwb_o, bb_o, kb_o, g_o, bonus_o, act_ref):
    i = pl.program_id(0)
    first = i == 0
    last = i == pl.num_programs(0) - 1
    tm = r_ref.shape[0]

    def shift(c_ref, p_ref, n_ref, mu):
        c = c_ref[...].astype(F32)
        row = lax.broadcasted_iota(jnp.int32, c.shape, 0)
        prow = jnp.where(first, 0.0, p_ref[PACKED_SUBLANES - 1:PACKED_SUBLANES, :].astype(F32))
        nrow = jnp.where(last, 0.0, n_ref[0:1, :].astype(F32))
        prev = jnp.where(row == 0, prow, pltpu.roll(c, 1, 0))
        nxt = jnp.where(row == tm - 1, nrow, pltpu.roll(c, tm - 1, 0))
        return c * (1.0 - mu) + (prev + nxt) * (0.5 * mu)

    r = shift(r_ref, rp_ref, rn_ref, mur_ref[...])
    k = shift(k_ref, kp_ref, kn_ref, muk_ref[...])
    v = shift(v_ref, vp_ref, vn_ref, muv_ref[...])
    P = LORA_PAD

    @pl.when(pl.program_id(1) == 0)
    def _():
        lo = shift(lo_ref, lop_ref, lon_ref, mul_ref[...])
        act_ref[:, 0:2 * P] = jnp.tanh(lo[:, 0:2 * P]).astype(BF16)
        act_ref[:, 2 * P:4 * P] = lo[:, 2 * P:4 * P].astype(BF16)
        act_ref[:, 4 * P:] = _sigmoid(lo[:, 4 * P:4 * P + GATE_LORA]).astype(BF16)

    ones_bd = bd_ref[...]

    kk = k * kk_ref[...]
    kk = kk * lax.rsqrt(jnp.maximum(_head_sum(kk * kk, ones_bd), 1e-24))
    k_a = ka_ref[...]

    def direction(wd, ad, w0, wu, a0, au):
        lw = -DECAY_SCALE * _sigmoid(w0 + _mm(wd, wu))
        a = _sigmoid(a0 + _mm(ad, au))
        k_mod = k * (1.0 + (a - 1.0) * k_a)
        return lw, kk * a, k_mod

    lwf, bf, kf = direction(act_ref[:, 0:P], act_ref[:, 2 * P:3 * P],
                            w0f_ref[...], wuf_ref[...], a0f_ref[...], auf_ref[...])
    lwb, bb, kb = direction(act_ref[:, P:2 * P], act_ref[:, 3 * P:4 * P],
                            w0b_ref[...], wub_ref[...], a0b_ref[...], aub_ref[...])
    g = _mm(act_ref[:, 4 * P:], gup_ref[...])
    bonus = _head_sum(r * (0.5 * (kf + kb)) * rk_ref[...], ones_bd) * v

    for ref, val in ((r_o, r), (v_o, v), (kk_o, kk), (lwf_o, lwf), (bf_o, bf), (kf_o, kf),
                     (lwb_o, lwb), (bb_o, bb), (kb_o, kb), (g_o, g), (bonus_o, bonus)):
        ref[...] = val.astype(ref.dtype)


def _rwkv_prep(cols, mu_rkv, mu_lora, vecs, w_up_f, w_up_b, a_up_f, a_up_b, g_up, ones_bd, tm=2048):
    T = cols.shape[0]
    nt = T // tm
    halo = PACKED_SUBLANES
    hb = tm // halo
    nhb = T // halo
    main = lambda cb: pl.BlockSpec((tm, LANES), lambda i, j: (i, cb + j))
    hprev = lambda cb: pl.BlockSpec((halo, LANES), lambda i, j: (jnp.maximum(i * hb - 1, 0), cb + j))
    hnext = lambda cb: pl.BlockSpec((halo, LANES), lambda i, j: (jnp.minimum((i + 1) * hb, nhb - 1), cb + j))
    lcb = CB_LORA * LANES // LORA_COLS
    vec = lambda off: pl.BlockSpec((1, LANES), lambda i, j: (0, off + j))
    up = lambda rows: pl.BlockSpec((rows, LANES), lambda i, j: (0, j))
    in_specs = []
    for cb in (CB_R, CB_K, CB_V):
        in_specs += [main(cb), hprev(cb), hnext(cb)]
    in_specs += [
        pl.BlockSpec((tm, LORA_COLS), lambda i, j: (i, lcb)),
        pl.BlockSpec((halo, LORA_COLS), lambda i, j: (jnp.maximum(i * hb - 1, 0), lcb)),
        pl.BlockSpec((halo, LORA_COLS), lambda i, j: (jnp.minimum((i + 1) * hb, nhb - 1), lcb)),
        vec(0), vec(N_PAIRS), vec(2 * N_PAIRS),
        pl.BlockSpec((1, LORA_COLS), lambda i, j: (0, 0)),
    ]
    in_specs += [vec(0)] * 7
    in_specs += [up(LORA_PAD)] * 4 + [up(GATE_LORA)]
    in_specs += [pl.BlockSpec((LANES, LANES), lambda i, j: (0, 0))]
    dtypes = [BF16, BF16, BF16, F32, BF16, BF16, F32, BF16, BF16, BF16, BF16]
    return pl.pallas_call(
        _prep_kernel,
        grid=(nt, N_PAIRS),
        in_specs=in_specs,
        out_specs=[pl.BlockSpec((tm, LANES), lambda i, j: (i, j))] * 11,
        out_shape=[jax.ShapeDtypeStruct((T, RWKV_DIM), dt) for dt in dtypes],
        scratch_shapes=[pltpu.VMEM((tm, 4 * LORA_PAD + GATE_LORA), BF16)],
        compiler_params=_cparams(("parallel", "arbitrary")),
        name="rwkv_prep",
    )(cols, cols, cols, cols, cols, cols, cols, cols, cols, cols, cols, cols,
      mu_rkv, mu_rkv, mu_rkv, mu_lora, *vecs, w_up_f, w_up_b, a_up_f, a_up_b, g_up, ones_bd)


def _tri_inverse_all(mats, upper):
    L = mats[0].shape[0]
    row = lax.broadcasted_iota(jnp.int32, (L, L), 0)
    col = lax.broadcasted_iota(jnp.int32, (L, L), 1)
    hrow = lax.broadcasted_iota(jnp.int32, (L // 2, L), 0)
    hcol = lax.broadcasted_iota(jnp.int32, (L // 2, L), 1)
    same = lambda sh: (row >> sh) == (col >> sh)
    eye = jnp.where(row == col, 1.0, 0.0)
    ds = [eye + jnp.where(same(1), a, 0.0) for a in mats]
    sh = 1
    while (1 << sh) < L:
        m = 1 << sh
        dbs = [d.astype(BF16) for d in ds]
        if m < SUBLANES:
            level = same(sh + 1) & jnp.logical_not(same(sh))
            ts = [_mm(jnp.where(level, a, 0.0), db) for a, db in zip(mats, dbs)]
            ds = [d + _mm(db, t) for d, db, t in zip(ds, dbs, ts)]
        else:
            n_half = L // (2 * m)
            part = [slice(q * m, (q + 1) * m) for q in range(n_half)]
            blocks = lambda x: [x[q * m:(q + 1) * m] for q in range(2 * n_half)]
            zero = jnp.zeros((m, L), F32)
            act = lambda x, up: jnp.concatenate(blocks(x)[(0 if up else 1)::2], axis=0)
            level = {False: (hcol >> sh) == 2 * (hrow >> sh), True: (hcol >> sh) == 2 * (hrow >> sh) + 1}

            def spread(x, up, rest=None):
                out = []
                for q in range(n_half):
                    other = zero if rest is None else rest[2 * q + (1 if up else 0)]
                    out += [x[part[q]], other] if up else [other, x[part[q]]]
                return jnp.concatenate(out, axis=0)

            ts = [_mm(jnp.where(level[up], act(a, up), 0.0), db) for a, db, up in zip(mats, dbs, upper)]
            upd = [_mm(act(d, up), spread(t, up)) for d, t, up in zip(ds, ts, upper)]
            ds = [spread(act(d, up) + u, up, rest=blocks(d)) for d, u, up in zip(ds, upd, upper)]
        sh += 1
    return ds


def _chunk_all(insts):
    L = insts[0][0].shape[0]
    n_inst = len(insts)
    row = lax.broadcasted_iota(jnp.int32, (L, L), 0)
    col = lax.broadcasted_iota(jnp.int32, (L, L), 1)
    incl = {False: col <= row, True: col >= row}
    strict = {False: col < row, True: col > row}
    ones = {rev: jnp.where(incl[rev], 1.0, 0.0) for rev in (False, True)}
    incl2 = {rev: jnp.concatenate([incl[rev]] * 2, axis=1) for rev in (False, True)}
    strict2 = {rev: jnp.concatenate([strict[rev]] * 2, axis=1) for rev in (False, True)}
    lane = lax.broadcasted_iota(jnp.int32, (1, LANES), 1)
    h0 = lane < RWKV_HEAD_DIM
    srow = lax.broadcasted_iota(jnp.int32, (LANES, LANES), 0)
    scol = lax.broadcasted_iota(jnp.int32, (LANES, LANES), 1)
    same_head = (srow >= RWKV_HEAD_DIM) == (scol >= RWKV_HEAD_DIM)

    cums = [_mm_exact_lhs(ones[inst[7]], inst[3], 2) for inst in insts]
    pre = []
    for (r, v, kk, lw, b, k, s, rev), cum in zip(insts, cums):
        tot = cum[0:1] if rev else cum[L - 1:L]
        mid = L // 2 if rev else L // 2 - 1
        rho = cum[mid:mid + 1]
        e1 = jnp.exp(cum - rho)
        e2 = jnp.exp(rho - cum)
        er = jnp.exp(rho)
        et = jnp.exp(tot - rho)
        a_t = -kk * e1 * jnp.exp(-lw)
        r_t = r * e1
        b_t = b * e2
        k_t = k * e2
        pre.append((a_t, r_t, b_t, k_t, er, et))
    grams = []
    for a_t, r_t, b_t, k_t, _, _ in pre:
        zero = jnp.zeros_like(a_t)
        lhs = jnp.concatenate([jnp.where(h0, a_t, zero), jnp.where(h0, zero, a_t),
                               jnp.where(h0, r_t, zero), jnp.where(h0, zero, r_t)], axis=0)
        rhs = jnp.concatenate([b_t, k_t], axis=0)
        grams.append(_mm(lhs, rhs, NT))
    a_bk, r_bk = [], []
    for inst, gram in zip(insts, grams):
        rev = inst[7]
        a_bk.append([jnp.where(strict2[rev], gram[hh * L:(hh + 1) * L], 0.0) for hh in range(2)])
        r_bk.append([jnp.where(incl2[rev], gram[(2 + hh) * L:(3 + hh) * L], 0.0) for hh in range(2)])
    t_inv = _tri_inverse_all([a_bk[n][hh][:, :L] for n in range(n_inst) for hh in range(2)],
                             [insts[n][7] for n in range(n_inst) for hh in range(2)])
    xs = [_mm(jnp.concatenate([a_t * er, r_t * er], axis=0), inst[6], NT)
          for inst, (a_t, r_t, _, _, er, _) in zip(insts, pre)]
    pick = lambda t2: jnp.where(h0, t2[:L], t2[L:])
    akv = [_mm(jnp.concatenate([a_bk[n][0][:, L:], a_bk[n][1][:, L:]], axis=0), inst[1])
           for n, inst in enumerate(insts)]
    x = [xs[n][:L] + pick(akv[n]) for n in range(n_inst)]
    u = [pick(_mm(jnp.concatenate([t_inv[2 * n], t_inv[2 * n + 1]], axis=0), x[n])) for n in range(n_inst)]
    z = [jnp.concatenate([u[n], inst[1]], axis=0) for n, inst in enumerate(insts)]
    out = []
    for n, inst in enumerate(insts):
        _, _, b_t, k_t, er, et = pre[n]
        y = xs[n][L:] + pick(_mm(jnp.concatenate([r_bk[n][0], r_bk[n][1]], axis=0), z[n]))
        w = jnp.concatenate([b_t * et, k_t * et], axis=0)
        s_new = inst[6] * (er * et) + jnp.where(same_head, _mm(z[n], w, TN), 0.0)
        out.append((y, s_new))
    return out


def _scan_kernel(rf_ref, vf_ref, kkf_ref, lwf_ref, bf_ref, kf_ref,
                 rb_ref, vb_ref, kkb_ref, lwb_ref, bb_ref, kb_ref,
                 yf_ref, yb_ref, s_ref):
    @pl.when(pl.program_id(1) == 0)
    def _():
        s_ref[...] = jnp.zeros_like(s_ref)

    n_pairs = s_ref.shape[1]
    insts = []
    for p in range(n_pairs):
        cs = slice(p * LANES, (p + 1) * LANES)
        insts.append(tuple(ref[:, cs].astype(F32) for ref in (rf_ref, vf_ref, kkf_ref, lwf_ref, bf_ref, kf_ref))
                     + (s_ref[0, p], False))
        insts.append(tuple(ref[:, cs].astype(F32) for ref in (rb_ref, vb_ref, kkb_ref, lwb_ref, bb_ref, kb_ref))
                     + (s_ref[1, p], True))
    res = _chunk_all(insts)
    for p in range(n_pairs):
        cs = slice(p * LANES, (p + 1) * LANES)
        yf_ref[:, cs], s_ref[0, p] = res[2 * p]
        yb_ref[:, cs], s_ref[1, p] = res[2 * p + 1]


def _rwkv_scan(r, v, kk, lwf, bf, kf, lwb, bb, kb, pairs_per_step=16):
    T = r.shape[0]
    nc = T // CHUNK
    width = pairs_per_step * LANES
    fwd = pl.BlockSpec((CHUNK, width), lambda p, c: (c, p))
    bwd = pl.BlockSpec((CHUNK, width), lambda p, c: (nc - 1 - c, p))
    out = jax.ShapeDtypeStruct((T, RWKV_DIM), F32)
    return pl.pallas_call(
        _scan_kernel,
        grid=(N_PAIRS // pairs_per_step, nc),
        in_specs=[fwd] * 6 + [bwd] * 6,
        out_specs=[fwd, bwd],
        out_shape=[out, out],
        scratch_shapes=[pltpu.VMEM((2, pairs_per_step, LANES, LANES), F32)],
        compiler_params=_cparams(("parallel", "arbitrary")),
        name="rwkv_scan",
    )(r, v, kk, lwf, bf, kf, r, v, kk, lwb, bb, kb)


def _rms(x, gain):
    return x * lax.rsqrt(jnp.mean(x * x, axis=-1, keepdims=True) + NORM_EPS) * gain


def _outproj_kernel(yf_ref, yb_ref, bonus_ref, g_ref, att_ref, gr_ref, gain_ref, bias_ref, bd_ref,
                    w_ref, x_ref, gpost_ref, gpre_ref, h_ref, hn_ref, m_ref):
    ones_bd = bd_ref[...]
    inv_n = 1.0 / RWKV_HEAD_DIM
    for j in range(N_PAIRS):
        cs = slice(j * LANES, (j + 1) * LANES)
        y = yf_ref[:, cs] + yb_ref[:, cs]
        mean = _head_sum(y, ones_bd) * inv_n
        d = y - mean
        var = _head_sum(d * d, ones_bd) * inv_n
        yn = d * lax.rsqrt(var + GN_EPS) * gain_ref[:, cs] + bias_ref[:, cs]
        o_rwkv = (yn + bonus_ref[:, cs].astype(F32)) * g_ref[:, cs].astype(F32)
        merged = att_ref[:, cs].astype(F32) + _sigmoid(gr_ref[:, cs].astype(F32)) * o_rwkv
        m_ref[:, cs] = merged.astype(BF16)
    mix = jnp.dot(m_ref[...], w_ref[...], preferred_element_type=F32)
    h = x_ref[...] + _rms(mix, gpost_ref[...])
    h_ref[...] = h
    hn_ref[...] = _rms(h, gpre_ref[...]).astype(BF16)


def _outproj(yf, yb, bonus, g, att, cols, gain, bias, ones_bd, w_out, x, g_post, g_pre, tm=256):
    T = x.shape[0]
    row = pl.BlockSpec((tm, D_MODEL), lambda i: (i, 0))
    vec = pl.BlockSpec((1, D_MODEL), lambda i: (0, 0))
    return pl.pallas_call(
        _outproj_kernel,
        grid=(T // tm,),
        in_specs=[row, row, row, row, row,
                  pl.BlockSpec((tm, D_MODEL), lambda i: (i, CB_GR // 16)),
                  vec, vec, pl.BlockSpec((LANES, LANES), lambda i: (0, 0)),
                  pl.BlockSpec((D_MODEL, D_MODEL), lambda i: (0, 0), pipeline_mode=pl.Buffered(1)),
                  row, vec, vec],
        out_specs=[row, row],
        out_shape=[jax.ShapeDtypeStruct((T, D_MODEL), F32), jax.ShapeDtypeStruct((T, D_MODEL), BF16)],
        scratch_shapes=[pltpu.VMEM((tm, D_MODEL), BF16)],
        compiler_params=_cparams(("parallel",)),
        name="outproj",
    )(yf, yb, bonus, g, att, cols, gain, bias, ones_bd, w_out, x, g_post, g_pre)


def _ffn_kernel(hn_ref, wu_ref, wd_ref, h_ref, g_ref, o_ref):
    j = pl.program_id(1)

    @pl.when(j == 0)
    def _():
        o_ref[...] = jnp.zeros_like(o_ref)

    up = jnp.dot(hn_ref[...], wu_ref[...], preferred_element_type=F32)
    act = jnp.square(jnp.maximum(up, 0.0)).astype(BF16)
    o_ref[...] += jnp.dot(act, wd_ref[...], preferred_element_type=F32)

    @pl.when(j == pl.num_programs(1) - 1)
    def _():
        o_ref[...] = h_ref[...] + _rms(o_ref[...], g_ref[...])


def _ffn(hn, w_up, w_down, h, gain, tm=1024, tf=1024):
    T = h.shape[0]
    row = pl.BlockSpec((tm, D_MODEL), lambda i, j: (i, 0))
    row_once = pl.BlockSpec((tm, D_MODEL), lambda i, j: (i, 0), pipeline_mode=pl.Buffered(1))
    return pl.pallas_call(
        _ffn_kernel,
        grid=(T // tm, D_FF // tf),
        in_specs=[row,
                  pl.BlockSpec((D_MODEL, tf), lambda i, j: (0, j)),
                  pl.BlockSpec((tf, D_MODEL), lambda i, j: (j, 0)),
                  row_once,
                  pl.BlockSpec((1, D_MODEL), lambda i, j: (0, 0))],
        out_specs=row_once,
        out_shape=jax.ShapeDtypeStruct((T, D_MODEL), F32),
        compiler_params=_cparams(("parallel", "arbitrary")),
        name="ffn",
    )(hn, w_up, w_down, h, gain)


def _pad_cols(t, n):
    return jnp.pad(t, ((0, 0), (0, n - t.shape[1])))


def _pad_rows(t, n):
    return jnp.pad(t, ((0, n - t.shape[0]), (0, 0)))


def _split_cols(t, sizes):
    idx = [int(i) for i in np.cumsum(sizes)[:-1]]
    return jnp.split(t, idx, axis=-1)


def _permute_in_cols(t):
    shift_sizes = [RWKV_DIM] * 3 + [DECAY_LORA] * 2 + [ICLR_LORA] * 2 + [GATE_LORA]
    q, ak, av, rw, ga, gr = _split_cols(t, [D_MODEL, KV_COLS, KV_COLS, sum(shift_sizes), D_MODEL, D_MODEL])
    r, k, v, wdf, wdb, adf, adb, gd = _split_cols(rw, shift_sizes)
    lora = jnp.concatenate([_pad_cols(p, LORA_PAD) for p in (wdf, wdb, adf, adb)] + [gd], axis=1)
    return jnp.concatenate([q, ga, gr, r, k, v, ak, av, _pad_cols(lora, LORA_COLS)], axis=1)


def _rope_tables(T):
    inv_freq = ROPE_THETA ** (-jnp.arange(0, ATT_HEAD_DIM, 2, dtype=F32) / ATT_HEAD_DIM)
    hi = (jnp.arange(T // BLOCK, dtype=F32) * BLOCK)[:, None, None] * inv_freq
    lo = jnp.arange(BLOCK, dtype=F32)[None, :, None] * inv_freq
    cos = (jnp.cos(hi) * jnp.cos(lo) - jnp.sin(hi) * jnp.sin(lo)).reshape(T, -1)
    sin = (jnp.sin(hi) * jnp.cos(lo) + jnp.cos(hi) * jnp.sin(lo)).reshape(T, -1)
    return jnp.concatenate([cos, cos], axis=1), jnp.concatenate([-sin, sin], axis=1)


def kernel(x, norm_pre_mix, w_in, mu_shift, attn_sink, w0_fwd, w_up_fwd, w0_bwd, w_up_bwd, a0_fwd, a_up_fwd, a0_bwd, a_up_bwd, g_up, k_k, k_a, r_k, ln_x_gain, ln_x_bias, w_out, norm_post_mix, norm_pre_ffn, w_ffn_up, w_ffn_down, norm_post_ffn):
    B, T, _ = x.shape
    depth = w_in.shape[0]
    cos2, sin2 = _rope_tables(T)
    lane = np.arange(LANES)
    ones_bd = jnp.asarray((lane[:, None] // RWKV_HEAD_DIM) == (lane[None, :] // RWKV_HEAD_DIM), BF16)
    row = lambda t: t.reshape(1, -1)
    outs = []
    for bi in range(B):
        h = x[bi]
        for l in range(depth):
            w_perm = _permute_in_cols(w_in[l].astype(BF16))
            mu = mu_shift[l].reshape(1, -1)
            shift_sizes = [RWKV_DIM] * 3 + [DECAY_LORA] * 2 + [ICLR_LORA] * 2 + [GATE_LORA]
            mr, mk, mv, m1, m2, m3, m4, mg = _split_cols(mu, shift_sizes)
            mu_rkv = jnp.concatenate([mr, mk, mv], axis=1)
            mu_lora = _pad_cols(jnp.concatenate([_pad_cols(p, LORA_PAD) for p in (m1, m2, m3, m4)] + [mg], axis=1),
                                LORA_COLS)
            cols = _inproj(h, row(norm_pre_mix[l]), w_perm)
            att = _attention(cols, attn_sink[l], cos2, sin2)
            vecs = [row(t[l]) for t in (w0_fwd, w0_bwd, a0_fwd, a0_bwd, k_k, k_a, r_k)]
            r, v, kk, lwf, bf, kf, lwb, bb, kb, g, bonus = _rwkv_prep(
                cols, mu_rkv, mu_lora, vecs,
                *[_pad_rows(t[l], LORA_PAD).astype(BF16) for t in (w_up_fwd, w_up_bwd, a_up_fwd, a_up_bwd)],
                g_up[l].astype(BF16), ones_bd)
            yf, yb = _rwkv_scan(r, v, kk, lwf, bf, kf, lwb, bb, kb)
            h, hn = _outproj(yf, yb, bonus, g, att, cols, row(ln_x_gain[l]), row(ln_x_bias[l]), ones_bd,
                             w_out[l].astype(BF16), h, row(norm_post_mix[l]), row(norm_pre_ffn[l]))
            h = _ffn(hn, w_ffn_up[l].astype(BF16), w_ffn_down[l].astype(BF16), h, row(norm_post_ffn[l]))
        outs.append(h)
    return jnp.stack(outs, axis=0)
```

```python
import functools

import jax
import jax.numpy as jnp
import numpy as np
from jax import lax
from jax.experimental import pallas as pl
from jax.experimental.pallas import tpu as pltpu

F32 = jnp.float32
BF16 = jnp.bfloat16
LANES = 128
SUBLANES = 8
PACKED_SUBLANES = 16
SHIFT_BLOCK = 128

D_MODEL = 2048
ATT_HEAD_DIM = 128
ATT_KV_HEADS = 4
ATT_GROUP = 4
WINDOW = 128
BLOCK = 128
ATT_Q_BLOCKS = 4
ROPE_THETA = 10000.0
RWKV_HEAD_DIM = 64
RWKV_DIM = 2048
N_PAIRS = RWKV_DIM // LANES
DECAY_LORA = 96
ICLR_LORA = 96
GATE_LORA = 256
LORA_PAD = 128
LORA_COLS = 1024
D_FF = 4 * D_MODEL
NORM_EPS = 1e-6
GN_EPS = 64e-5
MASK_VALUE = -1e30
DECAY_SCALE = float(np.exp(-0.5))
KV_COLS = ATT_KV_HEADS * ATT_HEAD_DIM

CB_Q, CB_GA, CB_GR, CB_R, CB_K, CB_V, CB_AK, CB_AV, CB_LORA = 0, 16, 32, 48, 64, 80, 96, 100, 104

CHUNK = 128
VMEM_LIMIT = 56 * 1024 * 1024

NN = (((1,), (0,)), ((), ()))
NT = (((1,), (1,)), ((), ()))
TN = (((0,), (0,)), ((), ()))


def _mm(a, b, dims=NN):
    return lax.dot_general(a.astype(BF16), b.astype(BF16), dims, preferred_element_type=F32)


def _split_bf16(x, parts):
    out = []
    for _ in range(parts - 1):
        hi = x.astype(BF16)
        out.append(hi)
        x = x - hi.astype(F32)
    out.append(x.astype(BF16))
    return out


def _mm_exact_lhs(a, b, parts):
    a = a.astype(BF16)
    acc = None
    for term in _split_bf16(b, parts):
        p = lax.dot_general(a, term, NN, preferred_element_type=F32)
        acc = p if acc is None else acc + p
    return acc


def _sigmoid(x):
    return 0.5 * jnp.tanh(0.5 * x) + 0.5


def _cparams(sem):
    return pltpu.CompilerParams(dimension_semantics=sem, vmem_limit_bytes=VMEM_LIMIT)


def _inproj_kernel(x_ref, g_ref, w_ref, o_ref, xn_ref):
    @pl.when(pl.program_id(1) == 0)
    def _():
        x = x_ref[...]
        ms = jnp.mean(x * x, axis=-1, keepdims=True)
        xn_ref[...] = (x * lax.rsqrt(ms + NORM_EPS) * g_ref[...]).astype(BF16)

    o_ref[...] = jnp.dot(xn_ref[...], w_ref[...], preferred_element_type=F32).astype(o_ref.dtype)


def _inproj(x, gain, w, tm=1024, tn=2048):
    T = x.shape[0]
    n = w.shape[1]
    return pl.pallas_call(
        _inproj_kernel,
        grid=(T // tm, n // tn),
        in_specs=[
            pl.BlockSpec((tm, D_MODEL), lambda i, j: (i, 0)),
            pl.BlockSpec((1, D_MODEL), lambda i, j: (0, 0)),
            pl.BlockSpec((D_MODEL, tn), lambda i, j: (0, j)),
        ],
        out_specs=pl.BlockSpec((tm, tn), lambda i, j: (i, j)),
        out_shape=jax.ShapeDtypeStruct((T, n), BF16),
        scratch_shapes=[pltpu.VMEM((tm, D_MODEL), BF16)],
        compiler_params=_cparams(("parallel", "arbitrary")),
        name="inproj",
    )(x, gain, w)


def _rope(x, c, s):
    return x * c + pltpu.roll(x, ATT_HEAD_DIM // 2, 1) * s


def _attn_kernel(sink_ref, q_ref, kp_ref, kc_ref, kn_ref, vp_ref, vc_ref, vn_ref, gate_ref,
                 cc_ref, sc_ref, cp_ref, sp_ref, cn_ref, sn_ref, o_ref, *, seq):
    i = pl.program_id(0)
    cc, sc = cc_ref[...], sc_ref[...]
    qi = lax.broadcasted_iota(jnp.int32, (BLOCK, 3 * BLOCK), 0)
    sj = lax.broadcasted_iota(jnp.int32, (BLOCK, 3 * BLOCK), 1)
    in_window = jnp.abs(sj - BLOCK - qi) <= WINDOW
    rowg = lax.broadcasted_iota(jnp.int32, (ATT_GROUP * BLOCK, 1), 0) // BLOCK
    scale = ATT_HEAD_DIM ** -0.5
    cos_all = jnp.concatenate([cp_ref[...], cc, cn_ref[...]], axis=0)
    sin_all = jnp.concatenate([sp_ref[...], sc, sn_ref[...]], axis=0)
    for g in range(ATT_KV_HEADS):
        ks = slice(g * ATT_HEAD_DIM, (g + 1) * ATT_HEAD_DIM)
        k_all = jnp.concatenate([kp_ref[:, ks], kc_ref[:, ks], kn_ref[:, ks]], axis=0).astype(F32)
        k_all = _rope(k_all, cos_all, sin_all).astype(BF16)
        v_all = jnp.concatenate([vp_ref[:, ks], vc_ref[:, ks], vn_ref[:, ks]], axis=0)
        heads = [g * ATT_GROUP + hh for hh in range(ATT_GROUP)]
        sink = jnp.zeros((ATT_GROUP * BLOCK, 1), F32)
        for hh, h in enumerate(heads):
            sink = jnp.where(rowg == hh, sink_ref[h], sink)
        for qb in range(ATT_Q_BLOCKS):
            rows = slice(qb * BLOCK, (qb + 1) * BLOCK)
            win = slice(qb * BLOCK, (qb + 3) * BLOCK)
            kpos = (i * ATT_Q_BLOCKS + qb - 1) * BLOCK + sj
            valid = in_window & (kpos >= 0) & (kpos < seq)
            valid4 = jnp.concatenate([valid] * ATT_GROUP, axis=0)
            q4 = jnp.concatenate(
                [_rope(q_ref[rows, h * ATT_HEAD_DIM:(h + 1) * ATT_HEAD_DIM].astype(F32), cc[rows], sc[rows])
                 for h in heads], axis=0) * scale
            s = _mm(q4, k_all[win], NT)
            s = jnp.where(valid4, s, MASK_VALUE)
            m = jnp.maximum(jnp.max(s, axis=-1, keepdims=True), sink)
            p = jnp.exp(s - m)
            den = jnp.sum(p, axis=-1, keepdims=True) + jnp.exp(sink - m)
            o = _mm(p, v_all[win]) / den
            for hh, h in enumerate(heads):
                cs = slice(h * ATT_HEAD_DIM, (h + 1) * ATT_HEAD_DIM)
                o_ref[rows, cs] = (o[hh * BLOCK:(hh + 1) * BLOCK]
                                   * _sigmoid(gate_ref[rows, cs].astype(F32))).astype(o_ref.dtype)


def _attention(cols, sink, cos2, sin2):
    T = cols.shape[0]
    Q = ATT_Q_BLOCKS
    nb = T // BLOCK
    prev = lambda i: (jnp.maximum(i * Q - 1, 0), 0)
    cur = lambda i: (i, 0)
    nxt = lambda i: (jnp.minimum((i + 1) * Q, nb - 1), 0)
    kcb, vcb = CB_AK * LANES // KV_COLS, CB_AV * LANES // KV_COLS
    col = lambda f, cb: (lambda i: (f(i)[0], cb))
    halo = lambda width, f, cb: pl.BlockSpec((BLOCK, width), col(f, cb))
    main = lambda width, cb: pl.BlockSpec((Q * BLOCK, width), col(cur, cb))
    return pl.pallas_call(
        functools.partial(_attn_kernel, seq=T),
        grid=(nb // Q,),
        in_specs=[
            pl.BlockSpec(memory_space=pltpu.SMEM),
            main(D_MODEL, CB_Q // 16),
            halo(KV_COLS, prev, kcb), main(KV_COLS, kcb), halo(KV_COLS, nxt, kcb),
            halo(KV_COLS, prev, vcb), main(KV_COLS, vcb), halo(KV_COLS, nxt, vcb),
            main(D_MODEL, CB_GA // 16),
            main(LANES, 0), main(LANES, 0),
            halo(LANES, prev, 0), halo(LANES, prev, 0),
            halo(LANES, nxt, 0), halo(LANES, nxt, 0),
        ],
        out_specs=main(D_MODEL, 0),
        out_shape=jax.ShapeDtypeStruct((T, D_MODEL), BF16),
        compiler_params=_cparams(("parallel",)),
        name="attention",
    )(sink, cols, cols, cols, cols, cols, cols, cols, cols, cos2, sin2, cos2, sin2, cos2, sin2)


def _head_sum(x, ones_bd):
    return _mm(x, ones_bd)


def _prep_kernel(r_ref, rp_ref, rn_ref, k_ref, kp_ref, kn_ref, v_ref, vp_ref, vn_ref,
                 lo_ref, lop_ref, lon_ref, mur_ref, muk_ref, muv_ref, mul_ref,
                 w0f_ref, w0b_ref, a0f_ref, a0b_ref, kk_ref, ka_ref, rk_ref,
                 wuf_ref, wub_ref, auf_ref, aub_ref, gup_ref, bd_ref, band_ref,
                 r_o, v_o, kk_o, lwf_o, bf_o, kf_o, lwb_o, bb_o, kb_o, g_o, bonus_o, act_ref):
    i = pl.program_id(0)
    first = i == 0
    last = i == pl.num_programs(0) - 1
    tm = r_ref.shape[0]

    band = band_ref[...]

    def shift(c_ref, p_ref, n_ref, mu):
        cb = c_ref[...]
        ph = jnp.where(first, 0.0, p_ref[...].astype(F32)).astype(BF16)
        nh = jnp.where(last, 0.0, n_ref[...].astype(F32)).astype(BF16)
        fill = jnp.zeros((SHIFT_BLOCK - 2 * PACKED_SUBLANES, cb.shape[1]), BF16)
        ext = jnp.concatenate([ph, cb, nh, fill], axis=0)
        both = jnp.concatenate(
            [jnp.dot(band, ext[b * SHIFT_BLOCK:(b + 2) * SHIFT_BLOCK], preferred_element_type=F32)
             for b in range(tm // SHIFT_BLOCK)], axis=0)
        return cb.astype(F32) * (1.0 - mu) + both * (0.5 * mu)

    r = shift(r_ref, rp_ref, rn_ref, mur_ref[...])
    k = shift(k_ref, kp_ref, kn_ref, muk_ref[...])
    v = shift(v_ref, vp_ref, vn_ref, muv_ref[...])
    P = LORA_PAD

    @pl.when(pl.program_id(1) == 0)
    def _():
        lo = shift(lo_ref, lop_ref, lon_ref, mul_ref[...])
        act_ref[:, 0:2 * P] = jnp.tanh(lo[:, 0:2 * P]).astype(BF16)
        act_ref[:, 2 * P:4 * P] = lo[:, 2 * P:4 * P].astype(BF16)
        act_ref[:, 4 * P:] = _sigmoid(lo[:, 4 * P:4 * P + GATE_LORA]).astype(BF16)

    ones_bd = bd_ref[...]

    kk = k * kk_ref[...]
    kk = kk * lax.rsqrt(jnp.maximum(_head_sum(kk * kk, ones_bd), 1e-24))
    k_a = ka_ref[...]

    def direction(wd, ad, w0, wu, a0, au):
        lw = -DECAY_SCALE * _sigmoid(w0 + _mm(wd, wu))
        a = _sigmoid(a0 + _mm(ad, au))
        k_mod = k * (1.0 + (a - 1.0) * k_a)
        return lw, kk * a, k_mod

    lwf, bf, kf = direction(act_ref[:, 0:P], act_ref[:, 2 * P:3 * P],
                            w0f_ref[...], wuf_ref[...], a0f_ref[...], auf_ref[...])
    lwb, bb, kb = direction(act_ref[:, P:2 * P], act_ref[:, 3 * P:4 * P],
                            w0b_ref[...], wub_ref[...], a0b_ref[...], aub_ref[...])
    g = _mm(act_ref[:, 4 * P:], gup_ref[...])
    bonus = _head_sum(r * (0.5 * (kf + kb)) * rk_ref[...], ones_bd) * v

    for ref, val in ((r_o, r), (v_o, v), (kk_o, kk), (lwf_o, lwf), (bf_o, bf), (kf_o, kf),
                     (lwb_o, lwb), (bb_o, bb), (kb_o, kb), (g_o, g), (bonus_o, bonus)):
        ref[...] = val.astype(ref.dtype)


def _rwkv_prep(cols, mu_rkv, mu_lora, vecs, w_up_f, w_up_b, a_up_f, a_up_b, g_up, ones_bd, band, tm=2048):
    T = cols.shape[0]
    nt = T // tm
    halo = PACKED_SUBLANES
    hb = tm // halo
    nhb = T // halo
    main = lambda cb: pl.BlockSpec((tm, LANES), lambda i, j: (i, cb + j))
    hprev = lambda cb: pl.BlockSpec((halo, LANES), lambda i, j: (jnp.maximum(i * hb - 1, 0), cb + j))
    hnext = lambda cb: pl.BlockSpec((halo, LANES), lambda i, j: (jnp.minimum((i + 1) * hb, nhb - 1), cb + j))
    lcb = CB_LORA * LANES // LORA_COLS
    vec = lambda off: pl.BlockSpec((1, LANES), lambda i, j: (0, off + j))
    up = lambda rows: pl.BlockSpec((rows, LANES), lambda i, j: (0, j))
    in_specs = []
    for cb in (CB_R, CB_K, CB_V):
        in_specs += [main(cb), hprev(cb), hnext(cb)]
    in_specs += [
        pl.BlockSpec((tm, LORA_COLS), lambda i, j: (i, lcb)),
        pl.BlockSpec((halo, LORA_COLS), lambda i, j: (jnp.maximum(i * hb - 1, 0), lcb)),
        pl.BlockSpec((halo, LORA_COLS), lambda i, j: (jnp.minimum((i + 1) * hb, nhb - 1), lcb)),
        vec(0), vec(N_PAIRS), vec(2 * N_PAIRS),
        pl.BlockSpec((1, LORA_COLS), lambda i, j: (0, 0)),
    ]
    in_specs += [vec(0)] * 7
    in_specs += [up(LORA_PAD)] * 4 + [up(GATE_LORA)]
    in_specs += [pl.BlockSpec((LANES, LANES), lambda i, j: (0, 0)),
                 pl.BlockSpec((SHIFT_BLOCK, 2 * SHIFT_BLOCK), lambda i, j: (0, 0))]
    dtypes = [BF16, BF16, BF16, F32, BF16, BF16, F32, BF16, BF16, BF16, BF16]
    return pl.pallas_call(
        _prep_kernel,
        grid=(nt, N_PAIRS),
        in_specs=in_specs,
        out_specs=[pl.BlockSpec((tm, LANES), lambda i, j: (i, j))] * 11,
        out_shape=[jax.ShapeDtypeStruct((T, RWKV_DIM), dt) for dt in dtypes],
        scratch_shapes=[pltpu.VMEM((tm, 4 * LORA_PAD + GATE_LORA), BF16)],
        compiler_params=_cparams(("parallel", "arbitrary")),
        name="rwkv_prep",
    )(cols, cols, cols, cols, cols, cols, cols, cols, cols, cols, cols, cols,
      mu_rkv, mu_rkv, mu_rkv, mu_lora, *vecs, w_up_f, w_up_b, a_up_f, a_up_b, g_up, ones_bd, band)


def _tri_inverse_all(mats, upper):
    L = mats[0].shape[0]
    row = lax.broadcasted_iota(jnp.int32, (L, L), 0)
    col = lax.broadcasted_iota(jnp.int32, (L, L), 1)
    hrow = lax.broadcasted_iota(jnp.int32, (L // 2, L), 0)
    hcol = lax.broadcasted_iota(jnp.int32, (L // 2, L), 1)
    same = lambda sh: (row >> sh) == (col >> sh)
    eye = jnp.where(row == col, 1.0, 0.0)
    ds = [eye + jnp.where(same(1), a, 0.0) for a in mats]
    sh = 1
    while (1 << sh) < L:
        m = 1 << sh
        dbs = [d.astype(BF16) for d in ds]
        if m < SUBLANES:
            level = same(sh + 1) & jnp.logical_not(same(sh))
            ts = [_mm(jnp.where(level, a, 0.0), db) for a, db in zip(mats, dbs)]
            ds = [d + _mm(db, t) for d, db, t in zip(ds, dbs, ts)]
        else:
            n_half = L // (2 * m)
            part = [slice(q * m, (q + 1) * m) for q in range(n_half)]
            blocks = lambda x: [x[q * m:(q + 1) * m] for q in range(2 * n_half)]
            zero = jnp.zeros((m, L), F32)
            act = lambda x, up: jnp.concatenate(blocks(x)[(0 if up else 1)::2], axis=0)
            level = {False: (hcol >> sh) == 2 * (hrow >> sh), True: (hcol >> sh) == 2 * (hrow >> sh) + 1}

            def spread(x, up, rest=None):
                out = []
                for q in range(n_half):
                    other = zero if rest is None else rest[2 * q + (1 if up else 0)]
                    out += [x[part[q]], other] if up else [other, x[part[q]]]
                return jnp.concatenate(out, axis=0)

            ts = [_mm(jnp.where(level[up], act(a, up), 0.0), db) for a, db, up in zip(mats, dbs, upper)]
            upd = [_mm(act(d, up), spread(t, up)) for d, t, up in zip(ds, ts, upper)]
            ds = [spread(act(d, up) + u, up, rest=blocks(d)) for d, u, up in zip(ds, upd, upper)]
        sh += 1
    return ds


def _chunk_all(insts):
    L = insts[0][0].shape[0]
    n_inst = len(insts)
    row = lax.broadcasted_iota(jnp.int32, (L, L), 0)
    col = lax.broadcasted_iota(jnp.int32, (L, L), 1)
    incl = {False: col <= row, True: col >= row}
    strict = {False: col < row, True: col > row}
    ones = {rev: jnp.where(incl[rev], 1.0, 0.0) for rev in (False, True)}
    incl2 = {rev: jnp.concatenate([incl[rev]] * 2, axis=1) for rev in (False, True)}
    strict2 = {rev: jnp.concatenate([strict[rev]] * 2, axis=1) for rev in (False, True)}
    lane = lax.broadcasted_iota(jnp.int32, (1, LANES), 1)
    h0 = lane < RWKV_HEAD_DIM
    srow = lax.broadcasted_iota(jnp.int32, (LANES, LANES), 0)
    scol = lax.broadcasted_iota(jnp.int32, (LANES, LANES), 1)
    same_head = (srow >= RWKV_HEAD_DIM) == (scol >= RWKV_HEAD_DIM)

    cums = [_mm_exact_lhs(ones[inst[7]], inst[3], 2) for inst in insts]
    pre = []
    for (r, v, kk, lw, b, k, s, rev), cum in zip(insts, cums):
        tot = cum[0:1] if rev else cum[L - 1:L]
        mid = L // 2 if rev else L // 2 - 1
        rho = cum[mid:mid + 1]
        e1 = jnp.exp(cum - rho)
        e2 = jnp.exp(rho - cum)
        er = jnp.exp(rho)
        et = jnp.exp(tot - rho)
        a_t = -kk * e1 * jnp.exp(-lw)
        r_t = r * e1
        b_t = b * e2
        k_t = k * e2
        pre.append((a_t, r_t, b_t, k_t, er, et))
    grams = []
    for a_t, r_t, b_t, k_t, _, _ in pre:
        zero = jnp.zeros_like(a_t)
        lhs = jnp.concatenate([jnp.where(h0, a_t, zero), jnp.where(h0, zero, a_t),
                               jnp.where(h0, r_t, zero), jnp.where(h0, zero, r_t)], axis=0)
        rhs = jnp.concatenate([b_t, k_t], axis=0)
        grams.append(_mm(lhs, rhs, NT))
    a_bk, r_bk = [], []
    for inst, gram in zip(insts, grams):
        rev = inst[7]
        a_bk.append([jnp.where(strict2[rev], gram[hh * L:(hh + 1) * L], 0.0) for hh in range(2)])
        r_bk.append([jnp.where(incl2[rev], gram[(2 + hh) * L:(3 + hh) * L], 0.0) for hh in range(2)])
    t_inv = _tri_inverse_all([a_bk[n][hh][:, :L] for n in range(n_inst) for hh in range(2)],
                             [insts[n][7] for n in range(n_inst) for hh in range(2)])
    xs = [_mm(jnp.concatenate([a_t * er, r_t * er], axis=0), inst[6], NT)
          for inst, (a_t, r_t, _, _, er, _) in zip(insts, pre)]
    pick = lambda t2: jnp.where(h0, t2[:L], t2[L:])
    akv = [_mm(jnp.concatenate([a_bk[n][0][:, L:], a_bk[n][1][:, L:]], axis=0), inst[1])
           for n, inst in enumerate(insts)]
    x = [xs[n][:L] + pick(akv[n]) for n in range(n_inst)]
    u = [pick(_mm(jnp.concatenate([t_inv[2 * n], t_inv[2 * n + 1]], axis=0), x[n])) for n in range(n_inst)]
    z = [jnp.concatenate([u[n], inst[1]], axis=0) for n, inst in enumerate(insts)]
    out = []
    for n, inst in enumerate(insts):
        _, _, b_t, k_t, er, et = pre[n]
        y = xs[n][L:] + pick(_mm(jnp.concatenate([r_bk[n][0], r_bk[n][1]], axis=0), z[n]))
        w = jnp.concatenate([b_t * et, k_t * et], axis=0)
        s_new = inst[6] * (er * et) + jnp.where(same_head, _mm(z[n], w, TN), 0.0)
        out.append((y, s_new))
    return out


def _scan_kernel(rf_ref, vf_ref, kkf_ref, lwf_ref, bf_ref, kf_ref,
                 rb_ref, vb_ref, kkb_ref, lwb_ref, bb_ref, kb_ref,
                 yf_ref, yb_ref, s_ref):
    @pl.when(pl.program_id(1) == 0)
    def _():
        s_ref[...] = jnp.zeros_like(s_ref)

    n_pairs = s_ref.shape[1]
    insts = []
    for p in range(n_pairs):
        cs = slice(p * LANES, (p + 1) * LANES)
        insts.append(tuple(ref[:, cs].astype(F32) for ref in (rf_ref, vf_ref, kkf_ref, lwf_ref, bf_ref, kf_ref))
                     + (s_ref[0, p], False))
        insts.append(tuple(ref[:, cs].astype(F32) for ref in (rb_ref, vb_ref, kkb_ref, lwb_ref, bb_ref, kb_ref))
                     + (s_ref[1, p], True))
    res = _chunk_all(insts)
    for p in range(n_pairs):
        cs = slice(p * LANES, (p + 1) * LANES)
        yf_ref[:, cs], s_ref[0, p] = res[2 * p]
        yb_ref[:, cs], s_ref[1, p] = res[2 * p + 1]


def _rwkv_scan(r, v, kk, lwf, bf, kf, lwb, bb, kb, pairs_per_step=16):
    T = r.shape[0]
    nc = T // CHUNK
    width = pairs_per_step * LANES
    fwd = pl.BlockSpec((CHUNK, width), lambda p, c: (c, p))
    bwd = pl.BlockSpec((CHUNK, width), lambda p, c: (nc - 1 - c, p))
    out = jax.ShapeDtypeStruct((T, RWKV_DIM), F32)
    return pl.pallas_call(
        _scan_kernel,
        grid=(N_PAIRS // pairs_per_step, nc),
        in_specs=[fwd] * 6 + [bwd] * 6,
        out_specs=[fwd, bwd],
        out_shape=[out, out],
        scratch_shapes=[pltpu.VMEM((2, pairs_per_step, LANES, LANES), F32)],
        compiler_params=_cparams(("parallel", "arbitrary")),
        name="rwkv_scan",
    )(r, v, kk, lwf, bf, kf, r, v, kk, lwb, bb, kb)


def _rms(x, gain):
    return x * lax.rsqrt(jnp.mean(x * x, axis=-1, keepdims=True) + NORM_EPS) * gain


def _outproj_kernel(yf_ref, yb_ref, bonus_ref, g_ref, att_ref, gr_ref, gain_ref, bias_ref, bd_ref,
                    w_ref, x_ref, gpost_ref, gpre_ref, h_ref, hn_ref, m_ref):
    ones_bd = bd_ref[...]
    inv_n = 1.0 / RWKV_HEAD_DIM
    for j in range(N_PAIRS):
        cs = slice(j * LANES, (j + 1) * LANES)
        y = yf_ref[:, cs] + yb_ref[:, cs]
        mean = _head_sum(y, ones_bd) * inv_n
        d = y - mean
        var = _head_sum(d * d, ones_bd) * inv_n
        yn = d * lax.rsqrt(var + GN_EPS) * gain_ref[:, cs] + bias_ref[:, cs]
        o_rwkv = (yn + bonus_ref[:, cs].astype(F32)) * g_ref[:, cs].astype(F32)
        merged = att_ref[:, cs].astype(F32) + _sigmoid(gr_ref[:, cs].astype(F32)) * o_rwkv
        m_ref[:, cs] = merged.astype(BF16)
    mix = jnp.dot(m_ref[...], w_ref[...], preferred_element_type=F32)
    h = x_ref[...] + _rms(mix, gpost_ref[...])
    h_ref[...] = h
    hn_ref[...] = _rms(h, gpre_ref[...]).astype(BF16)


def _outproj(yf, yb, bonus, g, att, cols, gain, bias, ones_bd, w_out, x, g_post, g_pre, tm=256):
    T = x.shape[0]
    row = pl.BlockSpec((tm, D_MODEL), lambda i: (i, 0))
    vec = pl.BlockSpec((1, D_MODEL), lambda i: (0, 0))
    return pl.pallas_call(
        _outproj_kernel,
        grid=(T // tm,),
        in_specs=[row, row, row, row, row,
                  pl.BlockSpec((tm, D_MODEL), lambda i: (i, CB_GR // 16)),
                  vec, vec, pl.BlockSpec((LANES, LANES), lambda i: (0, 0)),
                  pl.BlockSpec((D_MODEL, D_MODEL), lambda i: (0, 0), pipeline_mode=pl.Buffered(1)),
                  row, vec, vec],
        out_specs=[row, row],
        out_shape=[jax.ShapeDtypeStruct((T, D_MODEL), F32), jax.ShapeDtypeStruct((T, D_MODEL), BF16)],
        scratch_shapes=[pltpu.VMEM((tm, D_MODEL), BF16)],
        compiler_params=_cparams(("parallel",)),
        name="outproj",
    )(yf, yb, bonus, g, att, cols, gain, bias, ones_bd, w_out, x, g_post, g_pre)


def _ffn_kernel(hn_ref, wu_ref, wd_ref, h_ref, g_ref, o_ref):
    j = pl.program_id(1)

    @pl.when(j == 0)
    def _():
        o_ref[...] = jnp.zeros_like(o_ref)

    up = jnp.dot(hn_ref[...], wu_ref[...], preferred_element_type=F32)
    act = jnp.square(jnp.maximum(up, 0.0)).astype(BF16)
    o_ref[...] += jnp.dot(act, wd_ref[...], preferred_element_type=F32)

    @pl.when(j == pl.num_programs(1) - 1)
    def _():
        o_ref[...] = h_ref[...] + _rms(o_ref[...], g_ref[...])


def _ffn(hn, w_up, w_down, h, gain, tm=1024, tf=1024):
    T = h.shape[0]
    row = pl.BlockSpec((tm, D_MODEL), lambda i, j: (i, 0))
    row_once = pl.BlockSpec((tm, D_MODEL), lambda i, j: (i, 0), pipeline_mode=pl.Buffered(1))
    return pl.pallas_call(
        _ffn_kernel,
        grid=(T // tm, D_FF // tf),
        in_specs=[row,
                  pl.BlockSpec((D_MODEL, tf), lambda i, j: (0, j)),
                  pl.BlockSpec((tf, D_MODEL), lambda i, j: (j, 0)),
                  row_once,
                  pl.BlockSpec((1, D_MODEL), lambda i, j: (0, 0))],
        out_specs=row_once,
        out_shape=jax.ShapeDtypeStruct((T, D_MODEL), F32),
        compiler_params=_cparams(("parallel", "arbitrary")),
        name="ffn",
    )(hn, w_up, w_down, h, gain)


def _pad_cols(t, n):
    return jnp.pad(t, ((0, 0), (0, n - t.shape[1])))


def _pad_rows(t, n):
    return jnp.pad(t, ((0, n - t.shape[0]), (0, 0)))


def _split_cols(t, sizes):
    idx = [int(i) for i in np.cumsum(sizes)[:-1]]
    return jnp.split(t, idx, axis=-1)


def _permute_in_cols(t):
    shift_sizes = [RWKV_DIM] * 3 + [DECAY_LORA] * 2 + [ICLR_LORA] * 2 + [GATE_LORA]
    q, ak, av, rw, ga, gr = _split_cols(t, [D_MODEL, KV_COLS, KV_COLS, sum(shift_sizes), D_MODEL, D_MODEL])
    r, k, v, wdf, wdb, adf, adb, gd = _split_cols(rw, shift_sizes)
    lora = jnp.concatenate([_pad_cols(p, LORA_PAD) for p in (wdf, wdb, adf, adb)] + [gd], axis=1)
    return jnp.concatenate([q, ga, gr, r, k, v, ak, av, _pad_cols(lora, LORA_COLS)], axis=1)


def _rope_tables(T):
    inv_freq = ROPE_THETA ** (-jnp.arange(0, ATT_HEAD_DIM, 2, dtype=F32) / ATT_HEAD_DIM)
    hi = (jnp.arange(T // BLOCK, dtype=F32) * BLOCK)[:, None, None] * inv_freq
    lo = jnp.arange(BLOCK, dtype=F32)[None, :, None] * inv_freq
    cos = (jnp.cos(hi) * jnp.cos(lo) - jnp.sin(hi) * jnp.sin(lo)).reshape(T, -1)
    sin = (jnp.sin(hi) * jnp.cos(lo) + jnp.cos(hi) * jnp.sin(lo)).reshape(T, -1)
    return jnp.concatenate([cos, cos], axis=1), jnp.concatenate([-sin, sin], axis=1)


def kernel(x, norm_pre_mix, w_in, mu_shift, attn_sink, w0_fwd, w_up_fwd, w0_bwd, w_up_bwd, a0_fwd, a_up_fwd, a0_bwd, a_up_bwd, g_up, k_k, k_a, r_k, ln_x_gain, ln_x_bias, w_out, norm_post_mix, norm_pre_ffn, w_ffn_up, w_ffn_down, norm_post_ffn):
    B, T, _ = x.shape
    depth = w_in.shape[0]
    cos2, sin2 = _rope_tables(T)
    lane = np.arange(LANES)
    ones_bd = jnp.asarray((lane[:, None] // RWKV_HEAD_DIM) == (lane[None, :] // RWKV_HEAD_DIM), BF16)
    tok = np.arange(SHIFT_BLOCK)[:, None] + PACKED_SUBLANES
    win = np.arange(2 * SHIFT_BLOCK)[None, :]
    band = jnp.asarray((win == tok - 1) | (win == tok + 1), BF16)
    row = lambda t: t.reshape(1, -1)
    outs = []
    for bi in range(B):
        h = x[bi]
        for l in range(depth):
            w_perm = _permute_in_cols(w_in[l].astype(BF16))
            mu = mu_shift[l].reshape(1, -1)
            shift_sizes = [RWKV_DIM] * 3 + [DECAY_LORA] * 2 + [ICLR_LORA] * 2 + [GATE_LORA]
            mr, mk, mv, m1, m2, m3, m4, mg = _split_cols(mu, shift_sizes)
            mu_rkv = jnp.concatenate([mr, mk, mv], axis=1)
            mu_lora = _pad_cols(jnp.concatenate([_pad_cols(p, LORA_PAD) for p in (m1, m2, m3, m4)] + [mg], axis=1),
                                LORA_COLS)
            cols = _inproj(h, row(norm_pre_mix[l]), w_perm)
            att = _attention(cols, attn_sink[l], cos2, sin2)
            vecs = [row(t[l]) for t in (w0_fwd, w0_bwd, a0_fwd, a0_bwd, k_k, k_a, r_k)]
            r, v, kk, lwf, bf, kf, lwb, bb, kb, g, bonus = _rwkv_prep(
                cols, mu_rkv, mu_lora, vecs,
                *[_pad_rows(t[l], LORA_PAD).astype(BF16) for t in (w_up_fwd, w_up_bwd, a_up_fwd, a_up_bwd)],
                g_up[l].astype(BF16), ones_bd, band)
            yf, yb = _rwkv_scan(r, v, kk, lwf, bf, kf, lwb, bb, kb)
            h, hn = _outproj(yf, yb, bonus, g, att, cols, row(ln_x_gain[l]), row(ln_x_bias[l]), ones_bd,
                             w_out[l].astype(BF16), h, row(norm_post_mix[l]), row(norm_pre_ffn[l]))
            h = _ffn(hn, w_ffn_up[l].astype(BF16), w_ffn_down[l].astype(BF16), h, row(norm_post_ffn[l]))
        outs.append(h)
    return jnp.stack(outs, axis=0)
```

```python
import functools

import jax
import jax.numpy as jnp
import numpy as np
from jax import lax
from jax.experimental import pallas as pl
from jax.experimental.pallas import tpu as pltpu

F32 = jnp.float32
BF16 = jnp.bfloat16
LANES = 128
SUBLANES = 8
PACKED_SUBLANES = 16
SHIFT_BLOCK = 128

D_MODEL = 2048
ATT_HEAD_DIM = 128
ATT_KV_HEADS = 4
ATT_GROUP = 4
WINDOW = 128
BLOCK = 128
ATT_Q_BLOCKS = 4
ROPE_THETA = 10000.0
RWKV_HEAD_DIM = 64
RWKV_DIM = 2048
N_PAIRS = RWKV_DIM // LANES
DECAY_LORA = 96
ICLR_LORA = 96
GATE_LORA = 256
LORA_PAD = 128
LORA_COLS = 1024
D_FF = 4 * D_MODEL
NORM_EPS = 1e-6
GN_EPS = 64e-5
MASK_VALUE = -1e30
DECAY_SCALE = float(np.exp(-0.5))
KV_COLS = ATT_KV_HEADS * ATT_HEAD_DIM

CB_Q, CB_GA, CB_GR, CB_R, CB_K, CB_V, CB_AK, CB_AV, CB_LORA = 0, 16, 32, 48, 64, 80, 96, 100, 104

CHUNK = 128
VMEM_LIMIT = 56 * 1024 * 1024

NN = (((1,), (0,)), ((), ()))
NT = (((1,), (1,)), ((), ()))
TN = (((0,), (0,)), ((), ()))


def _mm(a, b, dims=NN):
    return lax.dot_general(a.astype(BF16), b.astype(BF16), dims, preferred_element_type=F32)


def _split_bf16(x, parts):
    out = []
    for _ in range(parts - 1):
        hi = x.astype(BF16)
        out.append(hi)
        x = x - hi.astype(F32)
    out.append(x.astype(BF16))
    return out


def _mm_exact_lhs(a, b, parts):
    a = a.astype(BF16)
    acc = None
    for term in _split_bf16(b, parts):
        p = lax.dot_general(a, term, NN, preferred_element_type=F32)
        acc = p if acc is None else acc + p
    return acc


def _sigmoid(x):
    return 0.5 * jnp.tanh(0.5 * x) + 0.5


def _cparams(sem):
    return pltpu.CompilerParams(dimension_semantics=sem, vmem_limit_bytes=VMEM_LIMIT)


def _inproj_kernel(x_ref, g_ref, w_ref, o_ref, xn_ref):
    @pl.when(pl.program_id(1) == 0)
    def _():
        x = x_ref[...]
        ms = jnp.mean(x * x, axis=-1, keepdims=True)
        xn_ref[...] = (x * lax.rsqrt(ms + NORM_EPS) * g_ref[...]).astype(BF16)

    o_ref[...] = jnp.dot(xn_ref[...], w_ref[...], preferred_element_type=F32).astype(o_ref.dtype)


def _inproj(x, gain, w, tm=1024, tn=2048):
    T = x.shape[0]
    n = w.shape[1]
    return pl.pallas_call(
        _inproj_kernel,
        grid=(T // tm, n // tn),
        in_specs=[
            pl.BlockSpec((tm, D_MODEL), lambda i, j: (i, 0)),
            pl.BlockSpec((1, D_MODEL), lambda i, j: (0, 0)),
            pl.BlockSpec((D_MODEL, tn), lambda i, j: (0, j)),
        ],
        out_specs=pl.BlockSpec((tm, tn), lambda i, j: (i, j)),
        out_shape=jax.ShapeDtypeStruct((T, n), BF16),
        scratch_shapes=[pltpu.VMEM((tm, D_MODEL), BF16)],
        compiler_params=_cparams(("parallel", "arbitrary")),
        name="inproj",
    )(x, gain, w)


def _rope(x, c, s):
    return x * c + pltpu.roll(x, ATT_HEAD_DIM // 2, 1) * s


def _attn_kernel(sink_ref, q_ref, kp_ref, kc_ref, kn_ref, vp_ref, vc_ref, vn_ref, gate_ref,
                 cc_ref, sc_ref, cp_ref, sp_ref, cn_ref, sn_ref, o_ref, *, seq):
    i = pl.program_id(0)
    cc, sc = cc_ref[...], sc_ref[...]
    qi = lax.broadcasted_iota(jnp.int32, (BLOCK, 3 * BLOCK), 0)
    sj = lax.broadcasted_iota(jnp.int32, (BLOCK, 3 * BLOCK), 1)
    in_window = jnp.abs(sj - BLOCK - qi) <= WINDOW
    rowg = lax.broadcasted_iota(jnp.int32, (ATT_GROUP * BLOCK, 1), 0) // BLOCK
    scale = ATT_HEAD_DIM ** -0.5
    cos_all = jnp.concatenate([cp_ref[...], cc, cn_ref[...]], axis=0)
    sin_all = jnp.concatenate([sp_ref[...], sc, sn_ref[...]], axis=0)
    for g in range(ATT_KV_HEADS):
        ks = slice(g * ATT_HEAD_DIM, (g + 1) * ATT_HEAD_DIM)
        k_all = jnp.concatenate([kp_ref[:, ks], kc_ref[:, ks], kn_ref[:, ks]], axis=0).astype(F32)
        k_all = _rope(k_all, cos_all, sin_all).astype(BF16)
        v_all = jnp.concatenate([vp_ref[:, ks], vc_ref[:, ks], vn_ref[:, ks]], axis=0)
        heads = [g * ATT_GROUP + hh for hh in range(ATT_GROUP)]
        sink = jnp.zeros((ATT_GROUP * BLOCK, 1), F32)
        for hh, h in enumerate(heads):
            sink = jnp.where(rowg == hh, sink_ref[h], sink)
        for qb in range(ATT_Q_BLOCKS):
            rows = slice(qb * BLOCK, (qb + 1) * BLOCK)
            win = slice(qb * BLOCK, (qb + 3) * BLOCK)
            kpos = (i * ATT_Q_BLOCKS + qb - 1) * BLOCK + sj
            valid = in_window & (kpos >= 0) & (kpos < seq)
            valid4 = jnp.concatenate([valid] * ATT_GROUP, axis=0)
            q4 = jnp.concatenate(
                [_rope(q_ref[rows, h * ATT_HEAD_DIM:(h + 1) * ATT_HEAD_DIM].astype(F32), cc[rows], sc[rows])
                 for h in heads], axis=0) * scale
            s = _mm(q4, k_all[win], NT)
            s = jnp.concatenate([jnp.where(valid4[:, :BLOCK], s[:, :BLOCK], MASK_VALUE), s[:, BLOCK:2 * BLOCK],
                                 jnp.where(valid4[:, 2 * BLOCK:], s[:, 2 * BLOCK:], MASK_VALUE)], axis=1)
            m = jnp.maximum(jnp.max(s, axis=-1, keepdims=True), sink)
            p = jnp.exp(s - m)
            den = jnp.sum(p, axis=-1, keepdims=True) + jnp.exp(sink - m)
            o = _mm(p, v_all[win]) / den
            for hh, h in enumerate(heads):
                cs = slice(h * ATT_HEAD_DIM, (h + 1) * ATT_HEAD_DIM)
                o_ref[rows, cs] = (o[hh * BLOCK:(hh + 1) * BLOCK]
                                   * _sigmoid(gate_ref[rows, cs].astype(F32))).astype(o_ref.dtype)


def _attention(cols, sink, cos2, sin2):
    T = cols.shape[0]
    Q = ATT_Q_BLOCKS
    nb = T // BLOCK
    prev = lambda i: (jnp.maximum(i * Q - 1, 0), 0)
    cur = lambda i: (i, 0)
    nxt = lambda i: (jnp.minimum((i + 1) * Q, nb - 1), 0)
    kcb, vcb = CB_AK * LANES // KV_COLS, CB_AV * LANES // KV_COLS
    col = lambda f, cb: (lambda i: (f(i)[0], cb))
    halo = lambda width, f, cb: pl.BlockSpec((BLOCK, width), col(f, cb))
    main = lambda width, cb: pl.BlockSpec((Q * BLOCK, width), col(cur, cb))
    return pl.pallas_call(
        functools.partial(_attn_kernel, seq=T),
        grid=(nb // Q,),
        in_specs=[
            pl.BlockSpec(memory_space=pltpu.SMEM),
            main(D_MODEL, CB_Q // 16),
            halo(KV_COLS, prev, kcb), main(KV_COLS, kcb), halo(KV_COLS, nxt, kcb),
            halo(KV_COLS, prev, vcb), main(KV_COLS, vcb), halo(KV_COLS, nxt, vcb),
            main(D_MODEL, CB_GA // 16),
            main(LANES, 0), main(LANES, 0),
            halo(LANES, prev, 0), halo(LANES, prev, 0),
            halo(LANES, nxt, 0), halo(LANES, nxt, 0),
        ],
        out_specs=main(D_MODEL, 0),
        out_shape=jax.ShapeDtypeStruct((T, D_MODEL), BF16),
        compiler_params=_cparams(("parallel",)),
        name="attention",
    )(sink, cols, cols, cols, cols, cols, cols, cols, cols, cos2, sin2, cos2, sin2, cos2, sin2)


def _head_sum(x, ones_bd):
    return _mm(x, ones_bd)


def _prep_kernel(r_ref, rp_ref, rn_ref, k_ref, kp_ref, kn_ref, v_ref, vp_ref, vn_ref,
                 lo_ref, lop_ref, lon_ref, mur_ref, muk_ref, muv_ref, mul_ref,
                 w0f_ref, w0b_ref, a0f_ref, a0b_ref, kk_ref, ka_ref, rk_ref,
                 wuf_ref, wub_ref, auf_ref, aub_ref, gup_ref, bd_ref, band_ref,
                 r_o, v_o, kk_o, lwf_o, bf_o, kf_o, lwb_o, bb_o, kb_o, g_o, bonus_o, act_ref):
    i = pl.program_id(0)
    first = i == 0
    last = i == pl.num_programs(0) - 1
    tm = r_ref.shape[0]

    band = band_ref[...]

    def shift(c_ref, p_ref, n_ref, mu):
        cb = c_ref[...]
        ph = jnp.where(first, 0.0, p_ref[...].astype(F32)).astype(BF16)
        nh = jnp.where(last, 0.0, n_ref[...].astype(F32)).astype(BF16)
        fill = jnp.zeros((SHIFT_BLOCK - 2 * PACKED_SUBLANES, cb.shape[1]), BF16)
        ext = jnp.concatenate([ph, cb, nh, fill], axis=0)
        both = jnp.concatenate(
            [jnp.dot(band, ext[b * SHIFT_BLOCK:(b + 2) * SHIFT_BLOCK], preferred_element_type=F32)
             for b in range(tm // SHIFT_BLOCK)], axis=0)
        return cb.astype(F32) * (1.0 - mu) + both * (0.5 * mu)

    r = shift(r_ref, rp_ref, rn_ref, mur_ref[...])
    k = shift(k_ref, kp_ref, kn_ref, muk_ref[...])
    v = shift(v_ref, vp_ref, vn_ref, muv_ref[...])
    P = LORA_PAD

    @pl.when(pl.program_id(1) == 0)
    def _():
        lo = shift(lo_ref, lop_ref, lon_ref, mul_ref[...])
        act_ref[:, 0:2 * P] = jnp.tanh(lo[:, 0:2 * P]).astype(BF16)
        act_ref[:, 2 * P:4 * P] = lo[:, 2 * P:4 * P].astype(BF16)
        act_ref[:, 4 * P:] = _sigmoid(lo[:, 4 * P:4 * P + GATE_LORA]).astype(BF16)

    ones_bd = bd_ref[...]

    kk = k * kk_ref[...]
    kk = kk * lax.rsqrt(jnp.maximum(_head_sum(kk * kk, ones_bd), 1e-24))
    k_a = ka_ref[...]

    def direction(wd, ad, w0, wu, a0, au):
        lw = -DECAY_SCALE * _sigmoid(w0 + _mm(wd, wu))
        a = _sigmoid(a0 + _mm(ad, au))
        k_mod = k * (1.0 + (a - 1.0) * k_a)
        return lw, kk * a, k_mod

    lwf, bf, kf = direction(act_ref[:, 0:P], act_ref[:, 2 * P:3 * P],
                            w0f_ref[...], wuf_ref[...], a0f_ref[...], auf_ref[...])
    lwb, bb, kb = direction(act_ref[:, P:2 * P], act_ref[:, 3 * P:4 * P],
                            w0b_ref[...], wub_ref[...], a0b_ref[...], aub_ref[...])
    g = _mm(act_ref[:, 4 * P:], gup_ref[...])
    bonus = _head_sum(r * (0.5 * (kf + kb)) * rk_ref[...], ones_bd) * v

    for ref, val in ((r_o, r), (v_o, v), (kk_o, kk), (lwf_o, lwf), (bf_o, bf), (kf_o, kf),
                     (lwb_o, lwb), (bb_o, bb), (kb_o, kb), (g_o, g), (bonus_o, bonus)):
        ref[...] = val.astype(ref.dtype)


def _rwkv_prep(cols, mu_rkv, mu_lora, vecs, w_up_f, w_up_b, a_up_f, a_up_b, g_up, ones_bd, band, tm=2048):
    T = cols.shape[0]
    nt = T // tm
    halo = PACKED_SUBLANES
    hb = tm // halo
    nhb = T // halo
    main = lambda cb: pl.BlockSpec((tm, LANES), lambda i, j: (i, cb + j))
    hprev = lambda cb: pl.BlockSpec((halo, LANES), lambda i, j: (jnp.maximum(i * hb - 1, 0), cb + j))
    hnext = lambda cb: pl.BlockSpec((halo, LANES), lambda i, j: (jnp.minimum((i + 1) * hb, nhb - 1), cb + j))
    lcb = CB_LORA * LANES // LORA_COLS
    vec = lambda off: pl.BlockSpec((1, LANES), lambda i, j: (0, off + j))
    up = lambda rows: pl.BlockSpec((rows, LANES), lambda i, j: (0, j))
    in_specs = []
    for cb in (CB_R, CB_K, CB_V):
        in_specs += [main(cb), hprev(cb), hnext(cb)]
    in_specs += [
        pl.BlockSpec((tm, LORA_COLS), lambda i, j: (i, lcb)),
        pl.BlockSpec((halo, LORA_COLS), lambda i, j: (jnp.maximum(i * hb - 1, 0), lcb)),
        pl.BlockSpec((halo, LORA_COLS), lambda i, j: (jnp.minimum((i + 1) * hb, nhb - 1), lcb)),
        vec(0), vec(N_PAIRS), vec(2 * N_PAIRS),
        pl.BlockSpec((1, LORA_COLS), lambda i, j: (0, 0)),
    ]
    in_specs += [vec(0)] * 7
    in_specs += [up(LORA_PAD)] * 4 + [up(GATE_LORA)]
    in_specs += [pl.BlockSpec((LANES, LANES), lambda i, j: (0, 0)),
                 pl.BlockSpec((SHIFT_BLOCK, 2 * SHIFT_BLOCK), lambda i, j: (0, 0))]
    dtypes = [BF16, BF16, BF16, F32, BF16, BF16, F32, BF16, BF16, BF16, BF16]
    return pl.pallas_call(
        _prep_kernel,
        grid=(nt, N_PAIRS),
        in_specs=in_specs,
        out_specs=[pl.BlockSpec((tm, LANES), lambda i, j: (i, j))] * 11,
        out_shape=[jax.ShapeDtypeStruct((T, RWKV_DIM), dt) for dt in dtypes],
        scratch_shapes=[pltpu.VMEM((tm, 4 * LORA_PAD + GATE_LORA), BF16)],
        compiler_params=_cparams(("parallel", "arbitrary")),
        name="rwkv_prep",
    )(cols, cols, cols, cols, cols, cols, cols, cols, cols, cols, cols, cols,
      mu_rkv, mu_rkv, mu_rkv, mu_lora, *vecs, w_up_f, w_up_b, a_up_f, a_up_b, g_up, ones_bd, band)


def _tri_inverse_all(mats, upper):
    L = mats[0].shape[0]
    row = lax.broadcasted_iota(jnp.int32, (L, L), 0)
    col = lax.broadcasted_iota(jnp.int32, (L, L), 1)
    hrow = lax.broadcasted_iota(jnp.int32, (L // 2, L), 0)
    hcol = lax.broadcasted_iota(jnp.int32, (L // 2, L), 1)
    same = lambda sh: (row >> sh) == (col >> sh)
    eye = jnp.where(row == col, 1.0, 0.0)
    ds = [eye + jnp.where(same(1), a, 0.0) for a in mats]
    sh = 1
    while (1 << sh) < L:
        m = 1 << sh
        dbs = [d.astype(BF16) for d in ds]
        if m < SUBLANES:
            level = same(sh + 1) & jnp.logical_not(same(sh))
            ts = [_mm(jnp.where(level, a, 0.0), db) for a, db in zip(mats, dbs)]
            ds = [d + _mm(db, t) for d, db, t in zip(ds, dbs, ts)]
        else:
            n_half = L // (2 * m)
            part = [slice(q * m, (q + 1) * m) for q in range(n_half)]
            blocks = lambda x: [x[q * m:(q + 1) * m] for q in range(2 * n_half)]
            zero = jnp.zeros((m, L), F32)
            act = lambda x, up: jnp.concatenate(blocks(x)[(0 if up else 1)::2], axis=0)
            level = {False: (hcol >> sh) == 2 * (hrow >> sh), True: (hcol >> sh) == 2 * (hrow >> sh) + 1}

            def spread(x, up, rest=None):
                out = []
                for q in range(n_half):
                    other = zero if rest is None else rest[2 * q + (1 if up else 0)]
                    out += [x[part[q]], other] if up else [other, x[part[q]]]
                return jnp.concatenate(out, axis=0)

            ts = [_mm(jnp.where(level[up], act(a, up), 0.0), db) for a, db, up in zip(mats, dbs, upper)]
            upd = [_mm(act(d, up), spread(t, up)) for d, t, up in zip(ds, ts, upper)]
            ds = [spread(act(d, up) + u, up, rest=blocks(d)) for d, u, up in zip(ds, upd, upper)]
        sh += 1
    return ds


def _chunk_all(insts):
    L = insts[0][0].shape[0]
    n_inst = len(insts)
    row = lax.broadcasted_iota(jnp.int32, (L, L), 0)
    col = lax.broadcasted_iota(jnp.int32, (L, L), 1)
    incl = {False: col <= row, True: col >= row}
    strict = {False: col < row, True: col > row}
    ones = {rev: jnp.where(incl[rev], 1.0, 0.0) for rev in (False, True)}
    incl2 = {rev: jnp.concatenate([incl[rev]] * 2, axis=1) for rev in (False, True)}
    strict2 = {rev: jnp.concatenate([strict[rev]] * 2, axis=1) for rev in (False, True)}
    lane = lax.broadcasted_iota(jnp.int32, (1, LANES), 1)
    h0 = lane < RWKV_HEAD_DIM
    srow = lax.broadcasted_iota(jnp.int32, (LANES, LANES), 0)
    scol = lax.broadcasted_iota(jnp.int32, (LANES, LANES), 1)
    same_head = (srow >= RWKV_HEAD_DIM) == (scol >= RWKV_HEAD_DIM)

    cums = [_mm_exact_lhs(ones[inst[7]], inst[3], 2) for inst in insts]
    pre = []
    for (r, v, kk, lw, b, k, s, rev), cum in zip(insts, cums):
        tot = cum[0:1] if rev else cum[L - 1:L]
        mid = L // 2 if rev else L // 2 - 1
        rho = cum[mid:mid + 1]
        e1 = jnp.exp(cum - rho)
        e2 = jnp.exp(rho - cum)
        er = jnp.exp(rho)
        et = jnp.exp(tot - rho)
        a_t = -kk * e1 * jnp.exp(-lw)
        r_t = r * e1
        b_t = b * e2
        k_t = k * e2
        pre.append((a_t, r_t, b_t, k_t, er, et))
    grams = []
    for a_t, r_t, b_t, k_t, _, _ in pre:
        zero = jnp.zeros_like(a_t)
        lhs = jnp.concatenate([jnp.where(h0, a_t, zero), jnp.where(h0, zero, a_t),
                               jnp.where(h0, r_t, zero), jnp.where(h0, zero, r_t)], axis=0)
        rhs = jnp.concatenate([b_t, k_t], axis=0)
        grams.append(_mm(lhs, rhs, NT))
    a_bk, r_bk = [], []
    for inst, gram in zip(insts, grams):
        rev = inst[7]
        a_bk.append([jnp.where(strict2[rev], gram[hh * L:(hh + 1) * L], 0.0) for hh in range(2)])
        r_bk.append([jnp.where(incl2[rev], gram[(2 + hh) * L:(3 + hh) * L], 0.0) for hh in range(2)])
    t_inv = _tri_inverse_all([a_bk[n][hh][:, :L] for n in range(n_inst) for hh in range(2)],
                             [insts[n][7] for n in range(n_inst) for hh in range(2)])
    xs = [_mm(jnp.concatenate([a_t * er, r_t * er], axis=0), inst[6], NT)
          for inst, (a_t, r_t, _, _, er, _) in zip(insts, pre)]
    pick = lambda t2: jnp.where(h0, t2[:L], t2[L:])
    akv = [_mm(jnp.concatenate([a_bk[n][0][:, L:], a_bk[n][1][:, L:]], axis=0), inst[1])
           for n, inst in enumerate(insts)]
    x = [xs[n][:L] + pick(akv[n]) for n in range(n_inst)]
    u = [pick(_mm(jnp.concatenate([t_inv[2 * n], t_inv[2 * n + 1]], axis=0), x[n])) for n in range(n_inst)]
    z = [jnp.concatenate([u[n], inst[1]], axis=0) for n, inst in enumerate(insts)]
    out = []
    for n, inst in enumerate(insts):
        _, _, b_t, k_t, er, et = pre[n]
        y = xs[n][L:] + pick(_mm(jnp.concatenate([r_bk[n][0], r_bk[n][1]], axis=0), z[n]))
        w = jnp.concatenate([b_t * et, k_t * et], axis=0)
        s_new = inst[6] * (er * et) + jnp.where(same_head, _mm(z[n], w, TN), 0.0)
        out.append((y, s_new))
    return out


def _scan_kernel(rf_ref, vf_ref, kkf_ref, lwf_ref, bf_ref, kf_ref,
                 rb_ref, vb_ref, kkb_ref, lwb_ref, bb_ref, kb_ref,
                 yf_ref, yb_ref, s_ref):
    @pl.when(pl.program_id(1) == 0)
    def _():
        s_ref[...] = jnp.zeros_like(s_ref)

    n_pairs = s_ref.shape[1]
    insts = []
    for p in range(n_pairs):
        cs = slice(p * LANES, (p + 1) * LANES)
        insts.append(tuple(ref[:, cs].astype(F32) for ref in (rf_ref, vf_ref, kkf_ref, lwf_ref, bf_ref, kf_ref))
                     + (s_ref[0, p], False))
        insts.append(tuple(ref[:, cs].astype(F32) for ref in (rb_ref, vb_ref, kkb_ref, lwb_ref, bb_ref, kb_ref))
                     + (s_ref[1, p], True))
    res = _chunk_all(insts)
    for p in range(n_pairs):
        cs = slice(p * LANES, (p + 1) * LANES)
        yf_ref[:, cs], s_ref[0, p] = res[2 * p]
        yb_ref[:, cs], s_ref[1, p] = res[2 * p + 1]


def _rwkv_scan(r, v, kk, lwf, bf, kf, lwb, bb, kb, pairs_per_step=16):
    T = r.shape[0]
    nc = T // CHUNK
    width = pairs_per_step * LANES
    fwd = pl.BlockSpec((CHUNK, width), lambda p, c: (c, p))
    bwd = pl.BlockSpec((CHUNK, width), lambda p, c: (nc - 1 - c, p))
    out = jax.ShapeDtypeStruct((T, RWKV_DIM), F32)
    return pl.pallas_call(
        _scan_kernel,
        grid=(N_PAIRS // pairs_per_step, nc),
        in_specs=[fwd] * 6 + [bwd] * 6,
        out_specs=[fwd, bwd],
        out_shape=[out, out],
        scratch_shapes=[pltpu.VMEM((2, pairs_per_step, LANES, LANES), F32)],
        compiler_params=_cparams(("parallel", "arbitrary")),
        name="rwkv_scan",
    )(r, v, kk, lwf, bf, kf, r, v, kk, lwb, bb, kb)


def _rms(x, gain):
    return x * lax.rsqrt(jnp.mean(x * x, axis=-1, keepdims=True) + NORM_EPS) * gain


def _outproj_kernel(yf_ref, yb_ref, bonus_ref, g_ref, att_ref, gr_ref, gain_ref, bias_ref, bd_ref,
                    w_ref, x_ref, gpost_ref, gpre_ref, h_ref, hn_ref, m_ref):
    ones_bd = bd_ref[...]
    inv_n = 1.0 / RWKV_HEAD_DIM
    tm = x_ref.shape[0]
    halves = [slice(0, tm // 2), slice(tm // 2, tm)]
    for rows in halves:
        for j in range(N_PAIRS):
            cs = slice(j * LANES, (j + 1) * LANES)
            y = yf_ref[rows, cs] + yb_ref[rows, cs]
            mean = _head_sum(y, ones_bd) * inv_n
            d = y - mean
            var = _head_sum(d * d, ones_bd) * inv_n
            yn = d * lax.rsqrt(var + GN_EPS) * gain_ref[:, cs] + bias_ref[:, cs]
            o_rwkv = (yn + bonus_ref[rows, cs].astype(F32)) * g_ref[rows, cs].astype(F32)
            merged = att_ref[rows, cs].astype(F32) + _sigmoid(gr_ref[rows, cs].astype(F32)) * o_rwkv
            m_ref[rows, cs] = merged.astype(BF16)
    mixes = [jnp.dot(m_ref[rows], w_ref[...], preferred_element_type=F32) for rows in halves]
    for rows, mix in zip(halves, mixes):
        h = x_ref[rows] + _rms(mix, gpost_ref[...])
        h_ref[rows] = h
        hn_ref[rows] = _rms(h, gpre_ref[...]).astype(BF16)


def _outproj(yf, yb, bonus, g, att, cols, gain, bias, ones_bd, w_out, x, g_post, g_pre, tm=256):
    T = x.shape[0]
    row = pl.BlockSpec((tm, D_MODEL), lambda i: (i, 0))
    vec = pl.BlockSpec((1, D_MODEL), lambda i: (0, 0))
    return pl.pallas_call(
        _outproj_kernel,
        grid=(T // tm,),
        in_specs=[row, row, row, row, row,
                  pl.BlockSpec((tm, D_MODEL), lambda i: (i, CB_GR // 16)),
                  vec, vec, pl.BlockSpec((LANES, LANES), lambda i: (0, 0)),
                  pl.BlockSpec((D_MODEL, D_MODEL), lambda i: (0, 0), pipeline_mode=pl.Buffered(1)),
                  row, vec, vec],
        out_specs=[row, row],
        out_shape=[jax.ShapeDtypeStruct((T, D_MODEL), F32), jax.ShapeDtypeStruct((T, D_MODEL), BF16)],
        scratch_shapes=[pltpu.VMEM((tm, D_MODEL), BF16)],
        compiler_params=_cparams(("parallel",)),
        name="outproj",
    )(yf, yb, bonus, g, att, cols, gain, bias, ones_bd, w_out, x, g_post, g_pre)


def _ffn_kernel(hn_ref, wu_ref, wd_ref, h_ref, g_ref, o_ref):
    j = pl.program_id(1)

    @pl.when(j == 0)
    def _():
        o_ref[...] = jnp.zeros_like(o_ref)

    up = jnp.dot(hn_ref[...], wu_ref[...], preferred_element_type=F32)
    act = jnp.square(jnp.maximum(up, 0.0)).astype(BF16)
    o_ref[...] += jnp.dot(act, wd_ref[...], preferred_element_type=F32)

    @pl.when(j == pl.num_programs(1) - 1)
    def _():
        o_ref[...] = h_ref[...] + _rms(o_ref[...], g_ref[...])


def _ffn(hn, w_up, w_down, h, gain, tm=1024, tf=1024):
    T = h.shape[0]
    row = pl.BlockSpec((tm, D_MODEL), lambda i, j: (i, 0))
    row_once = pl.BlockSpec((tm, D_MODEL), lambda i, j: (i, 0), pipeline_mode=pl.Buffered(1))
    return pl.pallas_call(
        _ffn_kernel,
        grid=(T // tm, D_FF // tf),
        in_specs=[row,
                  pl.BlockSpec((D_MODEL, tf), lambda i, j: (0, j)),
                  pl.BlockSpec((tf, D_MODEL), lambda i, j: (j, 0)),
                  row_once,
                  pl.BlockSpec((1, D_MODEL), lambda i, j: (0, 0))],
        out_specs=row_once,
        out_shape=jax.ShapeDtypeStruct((T, D_MODEL), F32),
        compiler_params=_cparams(("parallel", "arbitrary")),
        name="ffn",
    )(hn, w_up, w_down, h, gain)


def _pad_cols(t, n):
    return jnp.pad(t, ((0, 0), (0, n - t.shape[1])))


def _pad_rows(t, n):
    return jnp.pad(t, ((0, n - t.shape[0]), (0, 0)))


def _split_cols(t, sizes):
    idx = [int(i) for i in np.cumsum(sizes)[:-1]]
    return jnp.split(t, idx, axis=-1)


def _permute_in_cols(t):
    shift_sizes = [RWKV_DIM] * 3 + [DECAY_LORA] * 2 + [ICLR_LORA] * 2 + [GATE_LORA]
    q, ak, av, rw, ga, gr = _split_cols(t, [D_MODEL, KV_COLS, KV_COLS, sum(shift_sizes), D_MODEL, D_MODEL])
    r, k, v, wdf, wdb, adf, adb, gd = _split_cols(rw, shift_sizes)
    lora = jnp.concatenate([_pad_cols(p, LORA_PAD) for p in (wdf, wdb, adf, adb)] + [gd], axis=1)
    return jnp.concatenate([q, ga, gr, r, k, v, ak, av, _pad_cols(lora, LORA_COLS)], axis=1)


def _rope_tables(T):
    inv_freq = ROPE_THETA ** (-jnp.arange(0, ATT_HEAD_DIM, 2, dtype=F32) / ATT_HEAD_DIM)
    hi = (jnp.arange(T // BLOCK, dtype=F32) * BLOCK)[:, None, None] * inv_freq
    lo = jnp.arange(BLOCK, dtype=F32)[None, :, None] * inv_freq
    cos = (jnp.cos(hi) * jnp.cos(lo) - jnp.sin(hi) * jnp.sin(lo)).reshape(T, -1)
    sin = (jnp.sin(hi) * jnp.cos(lo) + jnp.cos(hi) * jnp.sin(lo)).reshape(T, -1)
    return jnp.concatenate([cos, cos], axis=1), jnp.concatenate([-sin, sin], axis=1)


def kernel(x, norm_pre_mix, w_in, mu_shift, attn_sink, w0_fwd, w_up_fwd, w0_bwd, w_up_bwd, a0_fwd, a_up_fwd, a0_bwd, a_up_bwd, g_up, k_k, k_a, r_k, ln_x_gain, ln_x_bias, w_out, norm_post_mix, norm_pre_ffn, w_ffn_up, w_ffn_down, norm_post_ffn):
    B, T, _ = x.shape
    depth = w_in.shape[0]
    cos2, sin2 = _rope_tables(T)
    lane = np.arange(LANES)
    ones_bd = jnp.asarray((lane[:, None] // RWKV_HEAD_DIM) == (lane[None, :] // RWKV_HEAD_DIM), BF16)
    tok = np.arange(SHIFT_BLOCK)[:, None] + PACKED_SUBLANES
    win = np.arange(2 * SHIFT_BLOCK)[None, :]
    band = jnp.asarray((win == tok - 1) | (win == tok + 1), BF16)
    row = lambda t: t.reshape(1, -1)
    outs = []
    for bi in range(B):
        h = x[bi]
        for l in range(depth):
            w_perm = _permute_in_cols(w_in[l].astype(BF16))
            mu = mu_shift[l].reshape(1, -1)
            shift_sizes = [RWKV_DIM] * 3 + [DECAY_LORA] * 2 + [ICLR_LORA] * 2 + [GATE_LORA]
            mr, mk, mv, m1, m2, m3, m4, mg = _split_cols(mu, shift_sizes)
            mu_rkv = jnp.concatenate([mr, mk, mv], axis=1)
            mu_lora = _pad_cols(jnp.concatenate([_pad_cols(p, LORA_PAD) for p in (m1, m2, m3, m4)] + [mg], axis=1),
                                LORA_COLS)
            cols = _inproj(h, row(norm_pre_mix[l]), w_perm)
            att = _attention(cols, attn_sink[l], cos2, sin2)
            vecs = [row(t[l]) for t in (w0_fwd, w0_bwd, a0_fwd, a0_bwd, k_k, k_a, r_k)]
            r, v, kk, lwf, bf, kf, lwb, bb, kb, g, bonus = _rwkv_prep(
                cols, mu_rkv, mu_lora, vecs,
                *[_pad_rows(t[l], LORA_PAD).astype(BF16) for t in (w_up_fwd, w_up_bwd, a_up_fwd, a_up_bwd)],
                g_up[l].astype(BF16), ones_bd, band)
            yf, yb = _rwkv_scan(r, v, kk, lwf, bf, kf, lwb, bb, kb)
            h, hn = _outproj(yf, yb, bonus, g, att, cols, row(ln_x_gain[l]), row(ln_x_bias[l]), ones_bd,
                             w_out[l].astype(BF16), h, row(norm_post_mix[l]), row(norm_pre_ffn[l]))
            h = _ffn(hn, w_ffn_up[l].astype(BF16), w_ffn_down[l].astype(BF16), h, row(norm_post_ffn[l]))
        outs.append(h)
    return jnp.stack(outs, axis=0)
```

```python
import functools

import jax
import jax.numpy as jnp
import numpy as np
from jax import lax
from jax.experimental import pallas as pl
from jax.experimental.pallas import tpu as pltpu

F32 = jnp.float32
BF16 = jnp.bfloat16
LANES = 128
SUBLANES = 8
PACKED_SUBLANES = 16
SHIFT_BLOCK = 128

D_MODEL = 2048
ATT_HEAD_DIM = 128
ATT_KV_HEADS = 4
ATT_GROUP = 4
WINDOW = 128
BLOCK = 128
ATT_Q_BLOCKS = 4
ROPE_THETA = 10000.0
RWKV_HEAD_DIM = 64
RWKV_DIM = 2048
N_PAIRS = RWKV_DIM // LANES
DECAY_LORA = 96
ICLR_LORA = 96
GATE_LORA = 256
LORA_PAD = 128
LORA_COLS = 1024
D_FF = 4 * D_MODEL
NORM_EPS = 1e-6
GN_EPS = 64e-5
MASK_VALUE = -1e30
DECAY_SCALE = float(np.exp(-0.5))
KV_COLS = ATT_KV_HEADS * ATT_HEAD_DIM

CB_Q, CB_AK, CB_AV, CB_R, CB_K, CB_V, CB_LORA, CB_GA, CB_GR = 0, 16, 20, 24, 40, 56, 72, 80, 96

CHUNK = 128
VMEM_LIMIT = 56 * 1024 * 1024

NN = (((1,), (0,)), ((), ()))
NT = (((1,), (1,)), ((), ()))
TN = (((0,), (0,)), ((), ()))


def _mm(a, b, dims=NN):
    return lax.dot_general(a.astype(BF16), b.astype(BF16), dims, preferred_element_type=F32)


def _split_bf16(x, parts):
    out = []
    for _ in range(parts - 1):
        hi = x.astype(BF16)
        out.append(hi)
        x = x - hi.astype(F32)
    out.append(x.astype(BF16))
    return out


def _mm_exact_lhs(a, b, parts):
    a = a.astype(BF16)
    acc = None
    for term in _split_bf16(b, parts):
        p = lax.dot_general(a, term, NN, preferred_element_type=F32)
        acc = p if acc is None else acc + p
    return acc


def _sigmoid(x):
    return 0.5 * jnp.tanh(0.5 * x) + 0.5


def _cparams(sem):
    return pltpu.CompilerParams(dimension_semantics=sem, vmem_limit_bytes=VMEM_LIMIT)


def _inproj_kernel(x_ref, g_ref, w_ref, o_ref, xn_ref):
    @pl.when(pl.program_id(1) == 0)
    def _():
        x = x_ref[...]
        ms = jnp.mean(x * x, axis=-1, keepdims=True)
        xn_ref[...] = (x * lax.rsqrt(ms + NORM_EPS) * g_ref[...]).astype(BF16)

    o_ref[...] = jnp.dot(xn_ref[...], w_ref[...], preferred_element_type=F32).astype(o_ref.dtype)


def _inproj(x, gain, w, tm=1024, tn=2048):
    T = x.shape[0]
    n = w.shape[1]
    return pl.pallas_call(
        _inproj_kernel,
        grid=(T // tm, n // tn),
        in_specs=[
            pl.BlockSpec((tm, D_MODEL), lambda i, j: (i, 0)),
            pl.BlockSpec((1, D_MODEL), lambda i, j: (0, 0)),
            pl.BlockSpec((D_MODEL, tn), lambda i, j: (0, j)),
        ],
        out_specs=pl.BlockSpec((tm, tn), lambda i, j: (i, j)),
        out_shape=jax.ShapeDtypeStruct((T, n), BF16),
        scratch_shapes=[pltpu.VMEM((tm, D_MODEL), BF16)],
        compiler_params=_cparams(("parallel", "arbitrary")),
        name="inproj",
    )(x, gain, w)


def _rope(x, c, s):
    return x * c + pltpu.roll(x, ATT_HEAD_DIM // 2, 1) * s


def _attn_kernel(sink_ref, q_ref, kp_ref, kc_ref, kn_ref, vp_ref, vc_ref, vn_ref, gate_ref,
                 cc_ref, sc_ref, cp_ref, sp_ref, cn_ref, sn_ref, o_ref, *, seq):
    i = pl.program_id(0)
    cc, sc = cc_ref[...], sc_ref[...]
    qi = lax.broadcasted_iota(jnp.int32, (BLOCK, 3 * BLOCK), 0)
    sj = lax.broadcasted_iota(jnp.int32, (BLOCK, 3 * BLOCK), 1)
    in_window = jnp.abs(sj - BLOCK - qi) <= WINDOW
    rowg = lax.broadcasted_iota(jnp.int32, (ATT_GROUP * BLOCK, 1), 0) // BLOCK
    scale = ATT_HEAD_DIM ** -0.5
    cos_all = jnp.concatenate([cp_ref[...], cc, cn_ref[...]], axis=0)
    sin_all = jnp.concatenate([sp_ref[...], sc, sn_ref[...]], axis=0)
    for g in range(ATT_KV_HEADS):
        ks = slice(g * ATT_HEAD_DIM, (g + 1) * ATT_HEAD_DIM)
        k_all = jnp.concatenate([kp_ref[:, ks], kc_ref[:, ks], kn_ref[:, ks]], axis=0).astype(F32)
        k_all = _rope(k_all, cos_all, sin_all).astype(BF16)
        v_all = jnp.concatenate([vp_ref[:, ks], vc_ref[:, ks], vn_ref[:, ks]], axis=0)
        heads = [g * ATT_GROUP + hh for hh in range(ATT_GROUP)]
        sink = jnp.zeros((ATT_GROUP * BLOCK, 1), F32)
        for hh, h in enumerate(heads):
            sink = jnp.where(rowg == hh, sink_ref[h], sink)
        for qb in range(ATT_Q_BLOCKS):
            rows = slice(qb * BLOCK, (qb + 1) * BLOCK)
            win = slice(qb * BLOCK, (qb + 3) * BLOCK)
            kpos = (i * ATT_Q_BLOCKS + qb - 1) * BLOCK + sj
            valid = in_window & (kpos >= 0) & (kpos < seq)
            valid4 = jnp.concatenate([valid] * ATT_GROUP, axis=0)
            q4 = jnp.concatenate(
                [_rope(q_ref[rows, h * ATT_HEAD_DIM:(h + 1) * ATT_HEAD_DIM].astype(F32), cc[rows], sc[rows])
                 for h in heads], axis=0) * scale
            s = _mm(q4, k_all[win], NT)
            s = jnp.where(valid4, s, MASK_VALUE)
            m = jnp.maximum(jnp.max(s, axis=-1, keepdims=True), sink)
            p = jnp.exp(s - m)
            den = jnp.sum(p, axis=-1, keepdims=True) + jnp.exp(sink - m)
            o = _mm(p, v_all[win]) / den
            for hh, h in enumerate(heads):
                cs = slice(h * ATT_HEAD_DIM, (h + 1) * ATT_HEAD_DIM)
                o_ref[rows, cs] = (o[hh * BLOCK:(hh + 1) * BLOCK]
                                   * _sigmoid(gate_ref[rows, cs].astype(F32))).astype(o_ref.dtype)


def _attention(cols, sink, cos2, sin2):
    T = cols.shape[0]
    Q = ATT_Q_BLOCKS
    nb = T // BLOCK
    prev = lambda i: (jnp.maximum(i * Q - 1, 0), 0)
    cur = lambda i: (i, 0)
    nxt = lambda i: (jnp.minimum((i + 1) * Q, nb - 1), 0)
    kcb, vcb = CB_AK * LANES // KV_COLS, CB_AV * LANES // KV_COLS
    col = lambda f, cb: (lambda i: (f(i)[0], cb))
    halo = lambda width, f, cb: pl.BlockSpec((BLOCK, width), col(f, cb))
    main = lambda width, cb: pl.BlockSpec((Q * BLOCK, width), col(cur, cb))
    return pl.pallas_call(
        functools.partial(_attn_kernel, seq=T),
        grid=(nb // Q,),
        in_specs=[
            pl.BlockSpec(memory_space=pltpu.SMEM),
            main(D_MODEL, CB_Q // 16),
            halo(KV_COLS, prev, kcb), main(KV_COLS, kcb), halo(KV_COLS, nxt, kcb),
            halo(KV_COLS, prev, vcb), main(KV_COLS, vcb), halo(KV_COLS, nxt, vcb),
            main(D_MODEL, CB_GA // 16),
            main(LANES, 0), main(LANES, 0),
            halo(LANES, prev, 0), halo(LANES, prev, 0),
            halo(LANES, nxt, 0), halo(LANES, nxt, 0),
        ],
        out_specs=main(D_MODEL, 0),
        out_shape=jax.ShapeDtypeStruct((T, D_MODEL), BF16),
        compiler_params=_cparams(("parallel",)),
        name="attention",
    )(sink, cols, cols, cols, cols, cols, cols, cols, cols, cos2, sin2, cos2, sin2, cos2, sin2)


def _head_sum(x, ones_bd):
    return _mm(x, ones_bd)


def _prep_kernel(r_ref, rp_ref, rn_ref, k_ref, kp_ref, kn_ref, v_ref, vp_ref, vn_ref,
                 lo_ref, lop_ref, lon_ref, mur_ref, muk_ref, muv_ref, mul_ref,
                 w0f_ref, w0b_ref, a0f_ref, a0b_ref, kk_ref, ka_ref, rk_ref,
                 wuf_ref, wub_ref, auf_ref, aub_ref, gup_ref, bd_ref, band_ref,
                 r_o, v_o, kk_o, lwf_o, bf_o, kf_o, lwb_o, bb_o, kb_o, g_o, bonus_o, act_ref):
    i = pl.program_id(0)
    first = i == 0
    last = i == pl.num_programs(0) - 1
    tm = r_ref.shape[0]

    band = band_ref[...]

    def shift(c_ref, p_ref, n_ref, mu):
        cb = c_ref[...]
        ph = jnp.where(first, 0.0, p_ref[...].astype(F32)).astype(BF16)
        nh = jnp.where(last, 0.0, n_ref[...].astype(F32)).astype(BF16)
        fill = jnp.zeros((SHIFT_BLOCK - 2 * PACKED_SUBLANES, cb.shape[1]), BF16)
        ext = jnp.concatenate([ph, cb, nh, fill], axis=0)
        both = jnp.concatenate(
            [jnp.dot(band, ext[b * SHIFT_BLOCK:(b + 2) * SHIFT_BLOCK], preferred_element_type=F32)
             for b in range(tm // SHIFT_BLOCK)], axis=0)
        return cb.astype(F32) * (1.0 - mu) + both * (0.5 * mu)

    r = shift(r_ref, rp_ref, rn_ref, mur_ref[...])
    k = shift(k_ref, kp_ref, kn_ref, muk_ref[...])
    v = shift(v_ref, vp_ref, vn_ref, muv_ref[...])
    P = LORA_PAD

    @pl.when(pl.program_id(1) == 0)
    def _():
        lo = shift(lo_ref, lop_ref, lon_ref, mul_ref[...])
        act_ref[:, 0:2 * P] = jnp.tanh(lo[:, 0:2 * P]).astype(BF16)
        act_ref[:, 2 * P:4 * P] = lo[:, 2 * P:4 * P].astype(BF16)
        act_ref[:, 4 * P:] = _sigmoid(lo[:, 4 * P:4 * P + GATE_LORA]).astype(BF16)

    ones_bd = bd_ref[...]

    kk = k * kk_ref[...]
    kk = kk * lax.rsqrt(jnp.maximum(_head_sum(kk * kk, ones_bd), 1e-24))
    k_a = ka_ref[...]

    def direction(wd, ad, w0, wu, a0, au):
        lw = -DECAY_SCALE * _sigmoid(w0 + _mm(wd, wu))
        a = _sigmoid(a0 + _mm(ad, au))
        k_mod = k * (1.0 + (a - 1.0) * k_a)
        return lw, kk * a, k_mod

    lwf, bf, kf = direction(act_ref[:, 0:P], act_ref[:, 2 * P:3 * P],
                            w0f_ref[...], wuf_ref[...], a0f_ref[...], auf_ref[...])
    lwb, bb, kb = direction(act_ref[:, P:2 * P], act_ref[:, 3 * P:4 * P],
                            w0b_ref[...], wub_ref[...], a0b_ref[...], aub_ref[...])
    g = _mm(act_ref[:, 4 * P:], gup_ref[...])
    bonus = _head_sum(r * (0.5 * (kf + kb)) * rk_ref[...], ones_bd) * v

    for ref, val in ((r_o, r), (v_o, v), (kk_o, kk), (lwf_o, lwf), (bf_o, bf), (kf_o, kf),
                     (lwb_o, lwb), (bb_o, bb), (kb_o, kb), (g_o, g), (bonus_o, bonus)):
        ref[...] = val.astype(ref.dtype)


def _rwkv_prep(cols, mu_rkv, mu_lora, vecs, w_up_f, w_up_b, a_up_f, a_up_b, g_up, ones_bd, band, tm=2048):
    T = cols.shape[0]
    nt = T // tm
    halo = PACKED_SUBLANES
    hb = tm // halo
    nhb = T // halo
    main = lambda cb: pl.BlockSpec((tm, LANES), lambda i, j: (i, cb + j))
    hprev = lambda cb: pl.BlockSpec((halo, LANES), lambda i, j: (jnp.maximum(i * hb - 1, 0), cb + j))
    hnext = lambda cb: pl.BlockSpec((halo, LANES), lambda i, j: (jnp.minimum((i + 1) * hb, nhb - 1), cb + j))
    lcb = CB_LORA * LANES // LORA_COLS
    vec = lambda off: pl.BlockSpec((1, LANES), lambda i, j: (0, off + j))
    up = lambda rows: pl.BlockSpec((rows, LANES), lambda i, j: (0, j))
    in_specs = []
    for cb in (CB_R, CB_K, CB_V):
        in_specs += [main(cb), hprev(cb), hnext(cb)]
    in_specs += [
        pl.BlockSpec((tm, LORA_COLS), lambda i, j: (i, lcb)),
        pl.BlockSpec((halo, LORA_COLS), lambda i, j: (jnp.maximum(i * hb - 1, 0), lcb)),
        pl.BlockSpec((halo, LORA_COLS), lambda i, j: (jnp.minimum((i + 1) * hb, nhb - 1), lcb)),
        vec(0), vec(N_PAIRS), vec(2 * N_PAIRS),
        pl.BlockSpec((1, LORA_COLS), lambda i, j: (0, 0)),
    ]
    in_specs += [vec(0)] * 7
    in_specs += [up(LORA_PAD)] * 4 + [up(GATE_LORA)]
    in_specs += [pl.BlockSpec((LANES, LANES), lambda i, j: (0, 0)),
                 pl.BlockSpec((SHIFT_BLOCK, 2 * SHIFT_BLOCK), lambda i, j: (0, 0))]
    dtypes = [BF16, BF16, BF16, F32, BF16, BF16, F32, BF16, BF16, BF16, BF16]
    return pl.pallas_call(
        _prep_kernel,
        grid=(nt, N_PAIRS),
        in_specs=in_specs,
        out_specs=[pl.BlockSpec((tm, LANES), lambda i, j: (i, j))] * 11,
        out_shape=[jax.ShapeDtypeStruct((T, RWKV_DIM), dt) for dt in dtypes],
        scratch_shapes=[pltpu.VMEM((tm, 4 * LORA_PAD + GATE_LORA), BF16)],
        compiler_params=_cparams(("parallel", "arbitrary")),
        name="rwkv_prep",
    )(cols, cols, cols, cols, cols, cols, cols, cols, cols, cols, cols, cols,
      mu_rkv, mu_rkv, mu_rkv, mu_lora, *vecs, w_up_f, w_up_b, a_up_f, a_up_b, g_up, ones_bd, band)


def _tri_inverse_all(mats, upper):
    L = mats[0].shape[0]
    row = lax.broadcasted_iota(jnp.int32, (L, L), 0)
    col = lax.broadcasted_iota(jnp.int32, (L, L), 1)
    hrow = lax.broadcasted_iota(jnp.int32, (L // 2, L), 0)
    hcol = lax.broadcasted_iota(jnp.int32, (L // 2, L), 1)
    same = lambda sh: (row >> sh) == (col >> sh)
    eye = jnp.where(row == col, 1.0, 0.0)
    ds = [eye + jnp.where(same(1), a, 0.0) for a in mats]
    sh = 1
    while (1 << sh) < L:
        m = 1 << sh
        dbs = [d.astype(BF16) for d in ds]
        if m < SUBLANES:
            level = same(sh + 1) & jnp.logical_not(same(sh))
            ts = [_mm(jnp.where(level, a, 0.0), db) for a, db in zip(mats, dbs)]
            ds = [d + _mm(db, t) for d, db, t in zip(ds, dbs, ts)]
        else:
            n_half = L // (2 * m)
            part = [slice(q * m, (q + 1) * m) for q in range(n_half)]
            blocks = lambda x: [x[q * m:(q + 1) * m] for q in range(2 * n_half)]
            zero = jnp.zeros((m, L), F32)
            act = lambda x, up: jnp.concatenate(blocks(x)[(0 if up else 1)::2], axis=0)
            level = {False: (hcol >> sh) == 2 * (hrow >> sh), True: (hcol >> sh) == 2 * (hrow >> sh) + 1}

            def spread(x, up, rest=None):
                out = []
                for q in range(n_half):
                    other = zero if rest is None else rest[2 * q + (1 if up else 0)]
                    out += [x[part[q]], other] if up else [other, x[part[q]]]
                return jnp.concatenate(out, axis=0)

            ts = [_mm(jnp.where(level[up], act(a, up), 0.0), db) for a, db, up in zip(mats, dbs, upper)]
            upd = [_mm(act(d, up), spread(t, up)) for d, t, up in zip(ds, ts, upper)]
            ds = [spread(act(d, up) + u, up, rest=blocks(d)) for d, u, up in zip(ds, upd, upper)]
        sh += 1
    return ds


def _chunk_all(insts):
    L = insts[0][0].shape[0]
    n_inst = len(insts)
    row = lax.broadcasted_iota(jnp.int32, (L, L), 0)
    col = lax.broadcasted_iota(jnp.int32, (L, L), 1)
    incl = {False: col <= row, True: col >= row}
    strict = {False: col < row, True: col > row}
    ones = {rev: jnp.where(incl[rev], 1.0, 0.0) for rev in (False, True)}
    incl2 = {rev: jnp.concatenate([incl[rev]] * 2, axis=1) for rev in (False, True)}
    strict2 = {rev: jnp.concatenate([strict[rev]] * 2, axis=1) for rev in (False, True)}
    lane = lax.broadcasted_iota(jnp.int32, (1, LANES), 1)
    h0 = lane < RWKV_HEAD_DIM
    srow = lax.broadcasted_iota(jnp.int32, (LANES, LANES), 0)
    scol = lax.broadcasted_iota(jnp.int32, (LANES, LANES), 1)
    same_head = (srow >= RWKV_HEAD_DIM) == (scol >= RWKV_HEAD_DIM)

    cums = [_mm_exact_lhs(ones[inst[7]], inst[3], 2) for inst in insts]
    pre = []
    for (r, v, kk, lw, b, k, s, rev), cum in zip(insts, cums):
        tot = cum[0:1] if rev else cum[L - 1:L]
        mid = L // 2 if rev else L // 2 - 1
        rho = cum[mid:mid + 1]
        e1 = jnp.exp(cum - rho)
        e2 = jnp.exp(rho - cum)
        er = jnp.exp(rho)
        et = jnp.exp(tot - rho)
        a_t = -kk * e1 * jnp.exp(-lw)
        r_t = r * e1
        b_t = b * e2
        k_t = k * e2
        pre.append((a_t, r_t, b_t, k_t, er, et))
    grams = []
    for a_t, r_t, b_t, k_t, _, _ in pre:
        zero = jnp.zeros_like(a_t)
        lhs = jnp.concatenate([jnp.where(h0, a_t, zero), jnp.where(h0, zero, a_t),
                               jnp.where(h0, r_t, zero), jnp.where(h0, zero, r_t)], axis=0)
        rhs = jnp.concatenate([b_t, k_t], axis=0)
        grams.append(_mm(lhs, rhs, NT))
    a_bk, r_bk = [], []
    for inst, gram in zip(insts, grams):
        rev = inst[7]
        a_bk.append([jnp.where(strict2[rev], gram[hh * L:(hh + 1) * L], 0.0) for hh in range(2)])
        r_bk.append([jnp.where(incl2[rev], gram[(2 + hh) * L:(3 + hh) * L], 0.0) for hh in range(2)])
    t_inv = _tri_inverse_all([a_bk[n][hh][:, :L] for n in range(n_inst) for hh in range(2)],
                             [insts[n][7] for n in range(n_inst) for hh in range(2)])
    xs = [_mm(jnp.concatenate([a_t * er, r_t * er], axis=0), inst[6], NT)
          for inst, (a_t, r_t, _, _, er, _) in zip(insts, pre)]
    pick = lambda t2: jnp.where(h0, t2[:L], t2[L:])
    akv = [_mm(jnp.concatenate([a_bk[n][0][:, L:], a_bk[n][1][:, L:]], axis=0), inst[1])
           for n, inst in enumerate(insts)]
    x = [xs[n][:L] + pick(akv[n]) for n in range(n_inst)]
    u = [pick(_mm(jnp.concatenate([t_inv[2 * n], t_inv[2 * n + 1]], axis=0), x[n])) for n in range(n_inst)]
    z = [jnp.concatenate([u[n], inst[1]], axis=0) for n, inst in enumerate(insts)]
    out = []
    for n, inst in enumerate(insts):
        _, _, b_t, k_t, er, et = pre[n]
        y = xs[n][L:] + pick(_mm(jnp.concatenate([r_bk[n][0], r_bk[n][1]], axis=0), z[n]))
        w = jnp.concatenate([b_t * et, k_t * et], axis=0)
        s_new = inst[6] * (er * et) + jnp.where(same_head, _mm(z[n], w, TN), 0.0)
        out.append((y, s_new))
    return out


def _scan_kernel(rf_ref, vf_ref, kkf_ref, lwf_ref, bf_ref, kf_ref,
                 rb_ref, vb_ref, kkb_ref, lwb_ref, bb_ref, kb_ref,
                 yf_ref, yb_ref, s_ref):
    @pl.when(pl.program_id(1) == 0)
    def _():
        s_ref[...] = jnp.zeros_like(s_ref)

    n_pairs = s_ref.shape[1]
    insts = []
    for p in range(n_pairs):
        cs = slice(p * LANES, (p + 1) * LANES)
        insts.append(tuple(ref[:, cs].astype(F32) for ref in (rf_ref, vf_ref, kkf_ref, lwf_ref, bf_ref, kf_ref))
                     + (s_ref[0, p], False))
        insts.append(tuple(ref[:, cs].astype(F32) for ref in (rb_ref, vb_ref, kkb_ref, lwb_ref, bb_ref, kb_ref))
                     + (s_ref[1, p], True))
    res = _chunk_all(insts)
    for p in range(n_pairs):
        cs = slice(p * LANES, (p + 1) * LANES)
        yf_ref[:, cs], s_ref[0, p] = res[2 * p]
        yb_ref[:, cs], s_ref[1, p] = res[2 * p + 1]


def _rwkv_scan(r, v, kk, lwf, bf, kf, lwb, bb, kb, pairs_per_step=16):
    T = r.shape[0]
    nc = T // CHUNK
    width = pairs_per_step * LANES
    fwd = pl.BlockSpec((CHUNK, width), lambda p, c: (c, p))
    bwd = pl.BlockSpec((CHUNK, width), lambda p, c: (nc - 1 - c, p))
    out = jax.ShapeDtypeStruct((T, RWKV_DIM), F32)
    return pl.pallas_call(
        _scan_kernel,
        grid=(N_PAIRS // pairs_per_step, nc),
        in_specs=[fwd] * 6 + [bwd] * 6,
        out_specs=[fwd, bwd],
        out_shape=[out, out],
        scratch_shapes=[pltpu.VMEM((2, pairs_per_step, LANES, LANES), F32)],
        compiler_params=_cparams(("parallel", "arbitrary")),
        name="rwkv_scan",
    )(r, v, kk, lwf, bf, kf, r, v, kk, lwb, bb, kb)


def _rms(x, gain):
    return x * lax.rsqrt(jnp.mean(x * x, axis=-1, keepdims=True) + NORM_EPS) * gain


def _outproj_kernel(yf_ref, yb_ref, bonus_ref, g_ref, att_ref, gr_ref, gain_ref, bias_ref, bd_ref,
                    w_ref, x_ref, gpost_ref, gpre_ref, h_ref, hn_ref, m_ref):
    ones_bd = bd_ref[...]
    inv_n = 1.0 / RWKV_HEAD_DIM
    for j in range(N_PAIRS):
        cs = slice(j * LANES, (j + 1) * LANES)
        y = yf_ref[:, cs] + yb_ref[:, cs]
        mean = _head_sum(y, ones_bd) * inv_n
        d = y - mean
        var = _head_sum(d * d, ones_bd) * inv_n
        yn = d * lax.rsqrt(var + GN_EPS) * gain_ref[:, cs] + bias_ref[:, cs]
        o_rwkv = (yn + bonus_ref[:, cs].astype(F32)) * g_ref[:, cs].astype(F32)
        merged = att_ref[:, cs].astype(F32) + _sigmoid(gr_ref[:, cs].astype(F32)) * o_rwkv
        m_ref[:, cs] = merged.astype(BF16)
    mix = jnp.dot(m_ref[...], w_ref[...], preferred_element_type=F32)
    h = x_ref[...] + _rms(mix, gpost_ref[...])
    h_ref[...] = h
    hn_ref[...] = _rms(h, gpre_ref[...]).astype(BF16)


def _outproj(yf, yb, bonus, g, att, cols, gain, bias, ones_bd, w_out, x, g_post, g_pre, tm=256):
    T = x.shape[0]
    row = pl.BlockSpec((tm, D_MODEL), lambda i: (i, 0))
    vec = pl.BlockSpec((1, D_MODEL), lambda i: (0, 0))
    return pl.pallas_call(
        _outproj_kernel,
        grid=(T // tm,),
        in_specs=[row, row, row, row, row,
                  pl.BlockSpec((tm, D_MODEL), lambda i: (i, CB_GR // 16)),
                  vec, vec, pl.BlockSpec((LANES, LANES), lambda i: (0, 0)),
                  pl.BlockSpec((D_MODEL, D_MODEL), lambda i: (0, 0), pipeline_mode=pl.Buffered(1)),
                  row, vec, vec],
        out_specs=[row, row],
        out_shape=[jax.ShapeDtypeStruct((T, D_MODEL), F32), jax.ShapeDtypeStruct((T, D_MODEL), BF16)],
        scratch_shapes=[pltpu.VMEM((tm, D_MODEL), BF16)],
        compiler_params=_cparams(("parallel",)),
        name="outproj",
    )(yf, yb, bonus, g, att, cols, gain, bias, ones_bd, w_out, x, g_post, g_pre)


def _ffn_kernel(hn_ref, wu_ref, wd_ref, h_ref, g_ref, o_ref):
    j = pl.program_id(1)

    @pl.when(j == 0)
    def _():
        o_ref[...] = jnp.zeros_like(o_ref)

    up = jnp.dot(hn_ref[...], wu_ref[...], preferred_element_type=F32)
    act = jnp.square(jnp.maximum(up, 0.0)).astype(BF16)
    o_ref[...] += jnp.dot(act, wd_ref[...], preferred_element_type=F32)

    @pl.when(j == pl.num_programs(1) - 1)
    def _():
        o_ref[...] = h_ref[...] + _rms(o_ref[...], g_ref[...])


def _ffn(hn, w_up, w_down, h, gain, tm=1024, tf=1024):
    T = h.shape[0]
    row = pl.BlockSpec((tm, D_MODEL), lambda i, j: (i, 0))
    row_once = pl.BlockSpec((tm, D_MODEL), lambda i, j: (i, 0), pipeline_mode=pl.Buffered(1))
    return pl.pallas_call(
        _ffn_kernel,
        grid=(T // tm, D_FF // tf),
        in_specs=[row,
                  pl.BlockSpec((D_MODEL, tf), lambda i, j: (0, j)),
                  pl.BlockSpec((tf, D_MODEL), lambda i, j: (j, 0)),
                  row_once,
                  pl.BlockSpec((1, D_MODEL), lambda i, j: (0, 0))],
        out_specs=row_once,
        out_shape=jax.ShapeDtypeStruct((T, D_MODEL), F32),
        compiler_params=_cparams(("parallel", "arbitrary")),
        name="ffn",
    )(hn, w_up, w_down, h, gain)


def _pad_cols(t, n):
    return jnp.pad(t, ((0, 0), (0, n - t.shape[1])))


def _pad_rows(t, n):
    return jnp.pad(t, ((0, n - t.shape[0]), (0, 0)))


def _split_cols(t, sizes):
    idx = [int(i) for i in np.cumsum(sizes)[:-1]]
    return jnp.split(t, idx, axis=-1)


def _permute_in_cols(t):
    shift_sizes = [RWKV_DIM] * 3 + [DECAY_LORA] * 2 + [ICLR_LORA] * 2 + [GATE_LORA]
    head, lora_in, gates = _split_cols(t, [CB_LORA * LANES, sum(shift_sizes[3:]), 2 * D_MODEL])
    wdf, wdb, adf, adb, gd = _split_cols(lora_in, shift_sizes[3:])
    lora = jnp.concatenate([_pad_cols(p, LORA_PAD) for p in (wdf, wdb, adf, adb)] + [gd], axis=1)
    return jnp.concatenate([head, _pad_cols(lora, LORA_COLS), gates], axis=1)


def _rope_tables(T):
    inv_freq = ROPE_THETA ** (-jnp.arange(0, ATT_HEAD_DIM, 2, dtype=F32) / ATT_HEAD_DIM)
    hi = (jnp.arange(T // BLOCK, dtype=F32) * BLOCK)[:, None, None] * inv_freq
    lo = jnp.arange(BLOCK, dtype=F32)[None, :, None] * inv_freq
    cos = (jnp.cos(hi) * jnp.cos(lo) - jnp.sin(hi) * jnp.sin(lo)).reshape(T, -1)
    sin = (jnp.sin(hi) * jnp.cos(lo) + jnp.cos(hi) * jnp.sin(lo)).reshape(T, -1)
    return jnp.concatenate([cos, cos], axis=1), jnp.concatenate([-sin, sin], axis=1)


def kernel(x, norm_pre_mix, w_in, mu_shift, attn_sink, w0_fwd, w_up_fwd, w0_bwd, w_up_bwd, a0_fwd, a_up_fwd, a0_bwd, a_up_bwd, g_up, k_k, k_a, r_k, ln_x_gain, ln_x_bias, w_out, norm_post_mix, norm_pre_ffn, w_ffn_up, w_ffn_down, norm_post_ffn):
    B, T, _ = x.shape
    depth = w_in.shape[0]
    cos2, sin2 = _rope_tables(T)
    lane = np.arange(LANES)
    ones_bd = jnp.asarray((lane[:, None] // RWKV_HEAD_DIM) == (lane[None, :] // RWKV_HEAD_DIM), BF16)
    tok = np.arange(SHIFT_BLOCK)[:, None] + PACKED_SUBLANES
    win = np.arange(2 * SHIFT_BLOCK)[None, :]
    band = jnp.asarray((win == tok - 1) | (win == tok + 1), BF16)
    row = lambda t: t.reshape(1, -1)
    outs = []
    for bi in range(B):
        h = x[bi]
        for l in range(depth):
            w_perm = _permute_in_cols(w_in[l].astype(BF16))
            mu = mu_shift[l].reshape(1, -1)
            shift_sizes = [RWKV_DIM] * 3 + [DECAY_LORA] * 2 + [ICLR_LORA] * 2 + [GATE_LORA]
            mr, mk, mv, m1, m2, m3, m4, mg = _split_cols(mu, shift_sizes)
            mu_rkv = jnp.concatenate([mr, mk, mv], axis=1)
            mu_lora = _pad_cols(jnp.concatenate([_pad_cols(p, LORA_PAD) for p in (m1, m2, m3, m4)] + [mg], axis=1),
                                LORA_COLS)
            cols = _inproj(h, row(norm_pre_mix[l]), w_perm)
            att = _attention(cols, attn_sink[l], cos2, sin2)
            vecs = [row(t[l]) for t in (w0_fwd, w0_bwd, a0_fwd, a0_bwd, k_k, k_a, r_k)]
            r, v, kk, lwf, bf, kf, lwb, bb, kb, g, bonus = _rwkv_prep(
                cols, mu_rkv, mu_lora, vecs,
                *[_pad_rows(t[l], LORA_PAD).astype(BF16) for t in (w_up_fwd, w_up_bwd, a_up_fwd, a_up_bwd)],
                g_up[l].astype(BF16), ones_bd, band)
            yf, yb = _rwkv_scan(r, v, kk, lwf, bf, kf, lwb, bb, kb)
            h, hn = _outproj(yf, yb, bonus, g, att, cols, row(ln_x_gain[l]), row(ln_x_bias[l]), ones_bd,
                             w_out[l].astype(BF16), h, row(norm_post_mix[l]), row(norm_pre_ffn[l]))
            h = _ffn(hn, w_ffn_up[l].astype(BF16), w_ffn_down[l].astype(BF16), h, row(norm_post_ffn[l]))
        outs.append(h)
    return jnp.stack(outs, axis=0)
```

```python
import functools

import jax
import jax.numpy as jnp
import numpy as np
from jax import lax
from jax.experimental import pallas as pl
from jax.experimental.pallas import tpu as pltpu

F32 = jnp.float32
BF16 = jnp.bfloat16
LANES = 128
SUBLANES = 8
PACKED_SUBLANES = 16
SHIFT_BLOCK = 128

D_MODEL = 2048
ATT_HEAD_DIM = 128
ATT_KV_HEADS = 4
ATT_GROUP = 4
WINDOW = 128
BLOCK = 128
ATT_Q_BLOCKS = 4
ROPE_THETA = 10000.0
RWKV_HEAD_DIM = 64
RWKV_DIM = 2048
N_PAIRS = RWKV_DIM // LANES
DECAY_LORA = 96
ICLR_LORA = 96
GATE_LORA = 256
LORA_PAD = 128
LORA_COLS = 1024
D_FF = 4 * D_MODEL
NORM_EPS = 1e-6
GN_EPS = 64e-5
MASK_VALUE = -1e30
DECAY_SCALE = float(np.exp(-0.5))
KV_COLS = ATT_KV_HEADS * ATT_HEAD_DIM

CB_Q, CB_GA, CB_GR, CB_R, CB_K, CB_V, CB_AK, CB_AV, CB_LORA = 0, 16, 32, 48, 64, 80, 96, 100, 104

CHUNK = 128
VMEM_LIMIT = 56 * 1024 * 1024

NN = (((1,), (0,)), ((), ()))
NT = (((1,), (1,)), ((), ()))
TN = (((0,), (0,)), ((), ()))


def _mm(a, b, dims=NN):
    return lax.dot_general(a.astype(BF16), b.astype(BF16), dims, preferred_element_type=F32)


def _split_bf16(x, parts):
    out = []
    for _ in range(parts - 1):
        hi = x.astype(BF16)
        out.append(hi)
        x = x - hi.astype(F32)
    out.append(x.astype(BF16))
    return out


def _mm_exact_lhs(a, b, parts):
    a = a.astype(BF16)
    acc = None
    for term in _split_bf16(b, parts):
        p = lax.dot_general(a, term, NN, preferred_element_type=F32)
        acc = p if acc is None else acc + p
    return acc


def _sigmoid(x):
    return 0.5 * jnp.tanh(0.5 * x) + 0.5


def _cparams(sem):
    return pltpu.CompilerParams(dimension_semantics=sem, vmem_limit_bytes=VMEM_LIMIT)


def _inproj_kernel(x_ref, g_ref, w_ref, o_ref, xn_ref):
    @pl.when(pl.program_id(1) == 0)
    def _():
        x = x_ref[...]
        ms = jnp.mean(x * x, axis=-1, keepdims=True)
        xn_ref[...] = (x * lax.rsqrt(ms + NORM_EPS) * g_ref[...]).astype(BF16)

    o_ref[...] = jnp.dot(xn_ref[...], w_ref[...], preferred_element_type=F32).astype(o_ref.dtype)


def _inproj(x, gain, w, tm=1024, tn=2048):
    T = x.shape[0]
    n = w.shape[1]
    return pl.pallas_call(
        _inproj_kernel,
        grid=(T // tm, n // tn),
        in_specs=[
            pl.BlockSpec((tm, D_MODEL), lambda i, j: (i, 0)),
            pl.BlockSpec((1, D_MODEL), lambda i, j: (0, 0)),
            pl.BlockSpec((D_MODEL, tn), lambda i, j: (0, j)),
        ],
        out_specs=pl.BlockSpec((tm, tn), lambda i, j: (i, j)),
        out_shape=jax.ShapeDtypeStruct((T, n), BF16),
        scratch_shapes=[pltpu.VMEM((tm, D_MODEL), BF16)],
        compiler_params=_cparams(("parallel", "arbitrary")),
        name="inproj",
    )(x, gain, w)


def _rope(x, c, s):
    return x * c + pltpu.roll(x, ATT_HEAD_DIM // 2, 1) * s


def _attn_kernel(sink_ref, q_ref, kp_ref, kc_ref, kn_ref, vp_ref, vc_ref, vn_ref, gate_ref,
                 cc_ref, sc_ref, cp_ref, sp_ref, cn_ref, sn_ref, o_ref, *, seq):
    i = pl.program_id(0)
    cc, sc = cc_ref[...], sc_ref[...]
    qi = lax.broadcasted_iota(jnp.int32, (BLOCK, 3 * BLOCK), 0)
    sj = lax.broadcasted_iota(jnp.int32, (BLOCK, 3 * BLOCK), 1)
    in_window = jnp.abs(sj - BLOCK - qi) <= WINDOW
    rowg = lax.broadcasted_iota(jnp.int32, (ATT_GROUP * BLOCK, 1), 0) // BLOCK
    scale = ATT_HEAD_DIM ** -0.5
    cos_all = jnp.concatenate([cp_ref[...], cc, cn_ref[...]], axis=0)
    sin_all = jnp.concatenate([sp_ref[...], sc, sn_ref[...]], axis=0)
    for g in range(ATT_KV_HEADS):
        ks = slice(g * ATT_HEAD_DIM, (g + 1) * ATT_HEAD_DIM)
        k_all = jnp.concatenate([kp_ref[:, ks], kc_ref[:, ks], kn_ref[:, ks]], axis=0).astype(F32)
        k_all = _rope(k_all, cos_all, sin_all).astype(BF16)
        v_all = jnp.concatenate([vp_ref[:, ks], vc_ref[:, ks], vn_ref[:, ks]], axis=0)
        heads = [g * ATT_GROUP + hh for hh in range(ATT_GROUP)]
        sink = jnp.zeros((ATT_GROUP * BLOCK, 1), F32)
        for hh, h in enumerate(heads):
            sink = jnp.where(rowg == hh, sink_ref[h], sink)
        for qb in range(ATT_Q_BLOCKS):
            rows = slice(qb * BLOCK, (qb + 1) * BLOCK)
            win = slice(qb * BLOCK, (qb + 3) * BLOCK)
            kpos = (i * ATT_Q_BLOCKS + qb - 1) * BLOCK + sj
            valid = in_window & (kpos >= 0) & (kpos < seq)
            valid4 = jnp.concatenate([valid] * ATT_GROUP, axis=0)
            q4 = jnp.concatenate(
                [_rope(q_ref[rows, h * ATT_HEAD_DIM:(h + 1) * ATT_HEAD_DIM].astype(F32), cc[rows], sc[rows])
                 for h in heads], axis=0) * scale
            s = _mm(q4, k_all[win], NT)
            s = jnp.where(valid4, s, MASK_VALUE)
            m = jnp.maximum(jnp.max(s, axis=-1, keepdims=True), sink)
            p = jnp.exp(s - m)
            den = jnp.sum(p, axis=-1, keepdims=True) + jnp.exp(sink - m)
            o = _mm(p, v_all[win]) / den
            for hh, h in enumerate(heads):
                cs = slice(h * ATT_HEAD_DIM, (h + 1) * ATT_HEAD_DIM)
                o_ref[rows, cs] = (o[hh * BLOCK:(hh + 1) * BLOCK]
                                   * _sigmoid(gate_ref[rows, cs].astype(F32))).astype(o_ref.dtype)


def _attention(cols, sink, cos2, sin2):
    T = cols.shape[0]
    Q = ATT_Q_BLOCKS
    nb = T // BLOCK
    prev = lambda i: (jnp.maximum(i * Q - 1, 0), 0)
    cur = lambda i: (i, 0)
    nxt = lambda i: (jnp.minimum((i + 1) * Q, nb - 1), 0)
    kcb, vcb = CB_AK * LANES // KV_COLS, CB_AV * LANES // KV_COLS
    col = lambda f, cb: (lambda i: (f(i)[0], cb))
    halo = lambda width, f, cb: pl.BlockSpec((BLOCK, width), col(f, cb))
    main = lambda width, cb: pl.BlockSpec((Q * BLOCK, width), col(cur, cb))
    return pl.pallas_call(
        functools.partial(_attn_kernel, seq=T),
        grid=(nb // Q,),
        in_specs=[
            pl.BlockSpec(memory_space=pltpu.SMEM),
            main(D_MODEL, CB_Q // 16),
            halo(KV_COLS, prev, kcb), main(KV_COLS, kcb), halo(KV_COLS, nxt, kcb),
            halo(KV_COLS, prev, vcb), main(KV_COLS, vcb), halo(KV_COLS, nxt, vcb),
            main(D_MODEL, CB_GA // 16),
            main(LANES, 0), main(LANES, 0),
            halo(LANES, prev, 0), halo(LANES, prev, 0),
            halo(LANES, nxt, 0), halo(LANES, nxt, 0),
        ],
        out_specs=main(D_MODEL, 0),
        out_shape=jax.ShapeDtypeStruct((T, D_MODEL), BF16),
        compiler_params=_cparams(("parallel",)),
        name="attention",
    )(sink, cols, cols, cols, cols, cols, cols, cols, cols, cos2, sin2, cos2, sin2, cos2, sin2)


def _head_sum(x, ones_bd):
    return _mm(x, ones_bd)


def _prep_kernel(r_ref, rp_ref, rn_ref, k_ref, kp_ref, kn_ref, v_ref, vp_ref, vn_ref,
                 lo_ref, lop_ref, lon_ref, mur_ref, muk_ref, muv_ref, mul_ref,
                 w0f_ref, w0b_ref, a0f_ref, a0b_ref, kk_ref, ka_ref, rk_ref,
                 wuf_ref, wub_ref, auf_ref, aub_ref, gup_ref, bd_ref, band_ref,
                 r_o, v_o, kk_o, lwf_o, bf_o, kf_o, lwb_o, bb_o, kb_o, g_o, bonus_o, act_ref):
    i = pl.program_id(0)
    first = i == 0
    last = i == pl.num_programs(0) - 1
    tm = r_ref.shape[0]

    band = band_ref[...]

    def shift(c_ref, p_ref, n_ref, mu):
        cb = c_ref[...]
        ph = jnp.where(first, 0.0, p_ref[...].astype(F32)).astype(BF16)
        nh = jnp.where(last, 0.0, n_ref[...].astype(F32)).astype(BF16)
        fill = jnp.zeros((SHIFT_BLOCK - 2 * PACKED_SUBLANES, cb.shape[1]), BF16)
        ext = jnp.concatenate([ph, cb, nh, fill], axis=0)
        both = jnp.concatenate(
            [jnp.dot(band, ext[b * SHIFT_BLOCK:(b + 2) * SHIFT_BLOCK], preferred_element_type=F32)
             for b in range(tm // SHIFT_BLOCK)], axis=0)
        return cb.astype(F32) * (1.0 - mu) + both * (0.5 * mu)

    r = shift(r_ref, rp_ref, rn_ref, mur_ref[...])
    k = shift(k_ref, kp_ref, kn_ref, muk_ref[...])
    v = shift(v_ref, vp_ref, vn_ref, muv_ref[...])
    P = LORA_PAD

    @pl.when(pl.program_id(1) == 0)
    def _():
        lo = shift(lo_ref, lop_ref, lon_ref, mul_ref[...])
        act_ref[:, 0:2 * P] = jnp.tanh(lo[:, 0:2 * P]).astype(BF16)
        act_ref[:, 2 * P:4 * P] = lo[:, 2 * P:4 * P].astype(BF16)
        act_ref[:, 4 * P:] = _sigmoid(lo[:, 4 * P:4 * P + GATE_LORA]).astype(BF16)

    ones_bd = bd_ref[...]

    kk = k * kk_ref[...]
    kk = kk * lax.rsqrt(jnp.maximum(_head_sum(kk * kk, ones_bd), 1e-24))
    k_a = ka_ref[...]

    def direction(wd, ad, w0, wu, a0, au):
        lw = -DECAY_SCALE * _sigmoid(w0 + _mm(wd, wu))
        a = _sigmoid(a0 + _mm(ad, au))
        k_mod = k * (1.0 + (a - 1.0) * k_a)
        return lw, kk * a, k_mod

    lwf, bf, kf = direction(act_ref[:, 0:P], act_ref[:, 2 * P:3 * P],
                            w0f_ref[...], wuf_ref[...], a0f_ref[...], auf_ref[...])
    lwb, bb, kb = direction(act_ref[:, P:2 * P], act_ref[:, 3 * P:4 * P],
                            w0b_ref[...], wub_ref[...], a0b_ref[...], aub_ref[...])
    g = _mm(act_ref[:, 4 * P:], gup_ref[...])
    bonus = _head_sum(r * (0.5 * (kf + kb)) * rk_ref[...], ones_bd) * v

    for ref, val in ((r_o, r), (v_o, v), (kk_o, kk), (lwf_o, lwf), (bf_o, bf), (kf_o, kf),
                     (lwb_o, lwb), (bb_o, bb), (kb_o, kb), (g_o, g), (bonus_o, bonus)):
        ref[...] = val.astype(ref.dtype)


def _rwkv_prep(cols, mu_rkv, mu_lora, vecs, w_up_f, w_up_b, a_up_f, a_up_b, g_up, ones_bd, band, tm=2048):
    T = cols.shape[0]
    nt = T // tm
    halo = PACKED_SUBLANES
    hb = tm // halo
    nhb = T // halo
    main = lambda cb: pl.BlockSpec((tm, LANES), lambda i, j: (i, cb + j))
    hprev = lambda cb: pl.BlockSpec((halo, LANES), lambda i, j: (jnp.maximum(i * hb - 1, 0), cb + j))
    hnext = lambda cb: pl.BlockSpec((halo, LANES), lambda i, j: (jnp.minimum((i + 1) * hb, nhb - 1), cb + j))
    lcb = CB_LORA * LANES // LORA_COLS
    vec = lambda off: pl.BlockSpec((1, LANES), lambda i, j: (0, off + j))
    up = lambda rows: pl.BlockSpec((rows, LANES), lambda i, j: (0, j))
    in_specs = []
    for cb in (CB_R, CB_K, CB_V):
        in_specs += [main(cb), hprev(cb), hnext(cb)]
    in_specs += [
        pl.BlockSpec((tm, LORA_COLS), lambda i, j: (i, lcb)),
        pl.BlockSpec((halo, LORA_COLS), lambda i, j: (jnp.maximum(i * hb - 1, 0), lcb)),
        pl.BlockSpec((halo, LORA_COLS), lambda i, j: (jnp.minimum((i + 1) * hb, nhb - 1), lcb)),
        vec(0), vec(N_PAIRS), vec(2 * N_PAIRS),
        pl.BlockSpec((1, LORA_COLS), lambda i, j: (0, 0)),
    ]
    in_specs += [vec(0)] * 7
    in_specs += [up(LORA_PAD)] * 4 + [up(GATE_LORA)]
    in_specs += [pl.BlockSpec((LANES, LANES), lambda i, j: (0, 0)),
                 pl.BlockSpec((SHIFT_BLOCK, 2 * SHIFT_BLOCK), lambda i, j: (0, 0))]
    dtypes = [BF16, BF16, BF16, F32, BF16, BF16, F32, BF16, BF16, BF16, BF16]
    return pl.pallas_call(
        _prep_kernel,
        grid=(nt, N_PAIRS),
        in_specs=in_specs,
        out_specs=[pl.BlockSpec((tm, LANES), lambda i, j: (i, j))] * 11,
        out_shape=[jax.ShapeDtypeStruct((T, RWKV_DIM), dt) for dt in dtypes],
        scratch_shapes=[pltpu.VMEM((tm, 4 * LORA_PAD + GATE_LORA), BF16)],
        compiler_params=_cparams(("parallel", "arbitrary")),
        name="rwkv_prep",
    )(cols, cols, cols, cols, cols, cols, cols, cols, cols, cols, cols, cols,
      mu_rkv, mu_rkv, mu_rkv, mu_lora, *vecs, w_up_f, w_up_b, a_up_f, a_up_b, g_up, ones_bd, band)


def _tri_inverse_all(mats, upper):
    L = mats[0].shape[0]
    row = lax.broadcasted_iota(jnp.int32, (L, L), 0)
    col = lax.broadcasted_iota(jnp.int32, (L, L), 1)
    hrow = lax.broadcasted_iota(jnp.int32, (L // 2, L), 0)
    hcol = lax.broadcasted_iota(jnp.int32, (L // 2, L), 1)
    same = lambda sh: (row >> sh) == (col >> sh)
    eye = jnp.where(row == col, 1.0, 0.0)
    ds = [eye + jnp.where(same(1), a, 0.0) for a in mats]
    sh = 1
    while (1 << sh) < L:
        m = 1 << sh
        dbs = [d.astype(BF16) for d in ds]
        if m < SUBLANES:
            level = same(sh + 1) & jnp.logical_not(same(sh))
            ts = [_mm(jnp.where(level, a, 0.0), db) for a, db in zip(mats, dbs)]
            ds = [d + _mm(db, t) for d, db, t in zip(ds, dbs, ts)]
        else:
            n_half = L // (2 * m)
            part = [slice(q * m, (q + 1) * m) for q in range(n_half)]
            blocks = lambda x: [x[q * m:(q + 1) * m] for q in range(2 * n_half)]
            zero = jnp.zeros((m, L), F32)
            act = lambda x, up: jnp.concatenate(blocks(x)[(0 if up else 1)::2], axis=0)
            level = {False: (hcol >> sh) == 2 * (hrow >> sh), True: (hcol >> sh) == 2 * (hrow >> sh) + 1}

            def spread(x, up, rest=None):
                out = []
                for q in range(n_half):
                    other = zero if rest is None else rest[2 * q + (1 if up else 0)]
                    out += [x[part[q]], other] if up else [other, x[part[q]]]
                return jnp.concatenate(out, axis=0)

            ts = [_mm(jnp.where(level[up], act(a, up), 0.0), db) for a, db, up in zip(mats, dbs, upper)]
            upd = [_mm(act(d, up), spread(t, up)) for d, t, up in zip(ds, ts, upper)]
            ds = [spread(act(d, up) + u, up, rest=blocks(d)) for d, u, up in zip(ds, upd, upper)]
        sh += 1
    return ds


def _chunk_all(insts):
    L = insts[0][0].shape[0]
    n_inst = len(insts)
    row = lax.broadcasted_iota(jnp.int32, (L, L), 0)
    col = lax.broadcasted_iota(jnp.int32, (L, L), 1)
    incl = {False: col <= row, True: col >= row}
    strict = {False: col < row, True: col > row}
    ones = {rev: jnp.where(incl[rev], 1.0, 0.0) for rev in (False, True)}
    incl2 = {rev: jnp.concatenate([incl[rev]] * 2, axis=1) for rev in (False, True)}
    strict2 = {rev: jnp.concatenate([strict[rev]] * 2, axis=1) for rev in (False, True)}
    lane = lax.broadcasted_iota(jnp.int32, (1, LANES), 1)
    h0 = lane < RWKV_HEAD_DIM
    srow = lax.broadcasted_iota(jnp.int32, (LANES, LANES), 0)
    scol = lax.broadcasted_iota(jnp.int32, (LANES, LANES), 1)
    same_head = (srow >= RWKV_HEAD_DIM) == (scol >= RWKV_HEAD_DIM)

    cums = [_mm_exact_lhs(ones[inst[7]], inst[3], 2) for inst in insts]
    pre = []
    for (r, v, kk, lw, b, k, s, rev), cum in zip(insts, cums):
        tot = cum[0:1] if rev else cum[L - 1:L]
        mid = L // 2 if rev else L // 2 - 1
        rho = cum[mid:mid + 1]
        e1 = jnp.exp(cum - rho)
        e2 = jnp.exp(rho - cum)
        er = jnp.exp(rho)
        et = jnp.exp(tot - rho)
        a_t = -kk * e1 * jnp.exp(-lw)
        r_t = r * e1
        b_t = b * e2
        k_t = k * e2
        pre.append((a_t, r_t, b_t, k_t, er, et))
    grams = []
    for a_t, r_t, b_t, k_t, _, _ in pre:
        zero = jnp.zeros_like(a_t)
        lhs = jnp.concatenate([jnp.where(h0, a_t, zero), jnp.where(h0, zero, a_t),
                               jnp.where(h0, r_t, zero), jnp.where(h0, zero, r_t)], axis=0)
        rhs = jnp.concatenate([b_t, k_t], axis=0)
        grams.append(_mm(lhs, rhs, NT))
    a_bk, r_bk = [], []
    for inst, gram in zip(insts, grams):
        rev = inst[7]
        a_bk.append([jnp.where(strict2[rev], gram[hh * L:(hh + 1) * L], 0.0) for hh in range(2)])
        r_bk.append([jnp.where(incl2[rev], gram[(2 + hh) * L:(3 + hh) * L], 0.0) for hh in range(2)])
    t_inv = _tri_inverse_all([a_bk[n][hh][:, :L] for n in range(n_inst) for hh in range(2)],
                             [insts[n][7] for n in range(n_inst) for hh in range(2)])
    xs = [_mm(jnp.concatenate([a_t * er, r_t * er], axis=0), inst[6], NT)
          for inst, (a_t, r_t, _, _, er, _) in zip(insts, pre)]
    pick = lambda t2: jnp.where(h0, t2[:L], t2[L:])
    akv = [_mm(jnp.concatenate([a_bk[n][0][:, L:], a_bk[n][1][:, L:]], axis=0), inst[1])
           for n, inst in enumerate(insts)]
    x = [xs[n][:L] + pick(akv[n]) for n in range(n_inst)]
    u = [pick(_mm(jnp.concatenate([t_inv[2 * n], t_inv[2 * n + 1]], axis=0), x[n])) for n in range(n_inst)]
    z = [jnp.concatenate([u[n], inst[1]], axis=0) for n, inst in enumerate(insts)]
    out = []
    for n, inst in enumerate(insts):
        _, _, b_t, k_t, er, et = pre[n]
        y = xs[n][L:] + pick(_mm(jnp.concatenate([r_bk[n][0], r_bk[n][1]], axis=0), z[n]))
        w = jnp.concatenate([b_t * et, k_t * et], axis=0)
        s_new = inst[6] * (er * et) + jnp.where(same_head, _mm(z[n], w, TN), 0.0)
        out.append((y, s_new))
    return out


def _scan_kernel(r_ref, v_ref, kk_ref, lw_ref, b_ref, k_ref, y_ref, s_ref, *, reverse):
    @pl.when(pl.program_id(0) == 0)
    def _():
        s_ref[...] = jnp.zeros_like(s_ref)

    n_pairs = s_ref.shape[0]
    insts = []
    for p in range(n_pairs):
        cs = slice(p * LANES, (p + 1) * LANES)
        insts.append(tuple(ref[:, cs].astype(F32) for ref in (r_ref, v_ref, kk_ref, lw_ref, b_ref, k_ref))
                     + (s_ref[p], reverse))
    res = _chunk_all(insts)
    for p in range(n_pairs):
        cs = slice(p * LANES, (p + 1) * LANES)
        y_ref[:, cs], s_ref[p] = res[p]


def _rwkv_scan_direction(r, v, kk, lw, b, k, reverse):
    T = r.shape[0]
    nc = T // CHUNK
    blk = pl.BlockSpec((CHUNK, RWKV_DIM), (lambda c: (nc - 1 - c, 0)) if reverse else (lambda c: (c, 0)))
    return pl.pallas_call(
        functools.partial(_scan_kernel, reverse=reverse),
        grid=(nc,),
        in_specs=[blk] * 6,
        out_specs=blk,
        out_shape=jax.ShapeDtypeStruct((T, RWKV_DIM), F32),
        scratch_shapes=[pltpu.VMEM((N_PAIRS, LANES, LANES), F32)],
        compiler_params=_cparams(("arbitrary",)),
        name="rwkv_scan_bwd" if reverse else "rwkv_scan_fwd",
    )(r, v, kk, lw, b, k)


def _rwkv_scan(r, v, kk, lwf, bf, kf, lwb, bb, kb):
    return (_rwkv_scan_direction(r, v, kk, lwf, bf, kf, False),
            _rwkv_scan_direction(r, v, kk, lwb, bb, kb, True))


def _rms(x, gain):
    return x * lax.rsqrt(jnp.mean(x * x, axis=-1, keepdims=True) + NORM_EPS) * gain


def _outproj_kernel(yf_ref, yb_ref, bonus_ref, g_ref, att_ref, gr_ref, gain_ref, bias_ref, bd_ref,
                    w_ref, x_ref, gpost_ref, gpre_ref, h_ref, hn_ref, m_ref):
    ones_bd = bd_ref[...]
    inv_n = 1.0 / RWKV_HEAD_DIM
    for j in range(N_PAIRS):
        cs = slice(j * LANES, (j + 1) * LANES)
        y = yf_ref[:, cs] + yb_ref[:, cs]
        mean = _head_sum(y, ones_bd) * inv_n
        d = y - mean
        var = _head_sum(d * d, ones_bd) * inv_n
        yn = d * lax.rsqrt(var + GN_EPS) * gain_ref[:, cs] + bias_ref[:, cs]
        o_rwkv = (yn + bonus_ref[:, cs].astype(F32)) * g_ref[:, cs].astype(F32)
        merged = att_ref[:, cs].astype(F32) + _sigmoid(gr_ref[:, cs].astype(F32)) * o_rwkv
        m_ref[:, cs] = merged.astype(BF16)
    mix = jnp.dot(m_ref[...], w_ref[...], preferred_element_type=F32)
    h = x_ref[...] + _rms(mix, gpost_ref[...])
    h_ref[...] = h
    hn_ref[...] = _rms(h, gpre_ref[...]).astype(BF16)


def _outproj(yf, yb, bonus, g, att, cols, gain, bias, ones_bd, w_out, x, g_post, g_pre, tm=256):
    T = x.shape[0]
    row = pl.BlockSpec((tm, D_MODEL), lambda i: (i, 0))
    vec = pl.BlockSpec((1, D_MODEL), lambda i: (0, 0))
    return pl.pallas_call(
        _outproj_kernel,
        grid=(T // tm,),
        in_specs=[row, row, row, row, row,
                  pl.BlockSpec((tm, D_MODEL), lambda i: (i, CB_GR // 16)),
                  vec, vec, pl.BlockSpec((LANES, LANES), lambda i: (0, 0)),
                  pl.BlockSpec((D_MODEL, D_MODEL), lambda i: (0, 0), pipeline_mode=pl.Buffered(1)),
                  row, vec, vec],
        out_specs=[row, row],
        out_shape=[jax.ShapeDtypeStruct((T, D_MODEL), F32), jax.ShapeDtypeStruct((T, D_MODEL), BF16)],
        scratch_shapes=[pltpu.VMEM((tm, D_MODEL), BF16)],
        compiler_params=_cparams(("parallel",)),
        name="outproj",
    )(yf, yb, bonus, g, att, cols, gain, bias, ones_bd, w_out, x, g_post, g_pre)


def _ffn_kernel(hn_ref, wu_ref, wd_ref, h_ref, g_ref, o_ref):
    j = pl.program_id(1)

    @pl.when(j == 0)
    def _():
        o_ref[...] = jnp.zeros_like(o_ref)

    up = jnp.dot(hn_ref[...], wu_ref[...], preferred_element_type=F32)
    act = jnp.square(jnp.maximum(up, 0.0)).astype(BF16)
    o_ref[...] += jnp.dot(act, wd_ref[...], preferred_element_type=F32)

    @pl.when(j == pl.num_programs(1) - 1)
    def _():
        o_ref[...] = h_ref[...] + _rms(o_ref[...], g_ref[...])


def _ffn(hn, w_up, w_down, h, gain, tm=1024, tf=1024):
    T = h.shape[0]
    row = pl.BlockSpec((tm, D_MODEL), lambda i, j: (i, 0))
    row_once = pl.BlockSpec((tm, D_MODEL), lambda i, j: (i, 0), pipeline_mode=pl.Buffered(1))
    return pl.pallas_call(
        _ffn_kernel,
        grid=(T // tm, D_FF // tf),
        in_specs=[row,
                  pl.BlockSpec((D_MODEL, tf), lambda i, j: (0, j)),
                  pl.BlockSpec((tf, D_MODEL), lambda i, j: (j, 0)),
                  row_once,
                  pl.BlockSpec((1, D_MODEL), lambda i, j: (0, 0))],
        out_specs=row_once,
        out_shape=jax.ShapeDtypeStruct((T, D_MODEL), F32),
        compiler_params=_cparams(("parallel", "arbitrary")),
        name="ffn",
    )(hn, w_up, w_down, h, gain)


def _pad_cols(t, n):
    return jnp.pad(t, ((0, 0), (0, n - t.shape[1])))


def _pad_rows(t, n):
    return jnp.pad(t, ((0, n - t.shape[0]), (0, 0)))


def _split_cols(t, sizes):
    idx = [int(i) for i in np.cumsum(sizes)[:-1]]
    return jnp.split(t, idx, axis=-1)


def _permute_in_cols(t):
    shift_sizes = [RWKV_DIM] * 3 + [DECAY_LORA] * 2 + [ICLR_LORA] * 2 + [GATE_LORA]
    q, ak, av, rw, ga, gr = _split_cols(t, [D_MODEL, KV_COLS, KV_COLS, sum(shift_sizes), D_MODEL, D_MODEL])
    r, k, v, wdf, wdb, adf, adb, gd = _split_cols(rw, shift_sizes)
    lora = jnp.concatenate([_pad_cols(p, LORA_PAD) for p in (wdf, wdb, adf, adb)] + [gd], axis=1)
    return jnp.concatenate([q, ga, gr, r, k, v, ak, av, _pad_cols(lora, LORA_COLS)], axis=1)


def _rope_tables(T):
    inv_freq = ROPE_THETA ** (-jnp.arange(0, ATT_HEAD_DIM, 2, dtype=F32) / ATT_HEAD_DIM)
    hi = (jnp.arange(T // BLOCK, dtype=F32) * BLOCK)[:, None, None] * inv_freq
    lo = jnp.arange(BLOCK, dtype=F32)[None, :, None] * inv_freq
    cos = (jnp.cos(hi) * jnp.cos(lo) - jnp.sin(hi) * jnp.sin(lo)).reshape(T, -1)
    sin = (jnp.sin(hi) * jnp.cos(lo) + jnp.cos(hi) * jnp.sin(lo)).reshape(T, -1)
    return jnp.concatenate([cos, cos], axis=1), jnp.concatenate([-sin, sin], axis=1)


def kernel(x, norm_pre_mix, w_in, mu_shift, attn_sink, w0_fwd, w_up_fwd, w0_bwd, w_up_bwd, a0_fwd, a_up_fwd, a0_bwd, a_up_bwd, g_up, k_k, k_a, r_k, ln_x_gain, ln_x_bias, w_out, norm_post_mix, norm_pre_ffn, w_ffn_up, w_ffn_down, norm_post_ffn):
    B, T, _ = x.shape
    depth = w_in.shape[0]
    cos2, sin2 = _rope_tables(T)
    lane = np.arange(LANES)
    ones_bd = jnp.asarray((lane[:, None] // RWKV_HEAD_DIM) == (lane[None, :] // RWKV_HEAD_DIM), BF16)
    tok = np.arange(SHIFT_BLOCK)[:, None] + PACKED_SUBLANES
    win = np.arange(2 * SHIFT_BLOCK)[None, :]
    band = jnp.asarray((win == tok - 1) | (win == tok + 1), BF16)
    row = lambda t: t.reshape(1, -1)
    outs = []
    for bi in range(B):
        h = x[bi]
        for l in range(depth):
            w_perm = _permute_in_cols(w_in[l].astype(BF16))
            mu = mu_shift[l].reshape(1, -1)
            shift_sizes = [RWKV_DIM] * 3 + [DECAY_LORA] * 2 + [ICLR_LORA] * 2 + [GATE_LORA]
            mr, mk, mv, m1, m2, m3, m4, mg = _split_cols(mu, shift_sizes)
            mu_rkv = jnp.concatenate([mr, mk, mv], axis=1)
            mu_lora = _pad_cols(jnp.concatenate([_pad_cols(p, LORA_PAD) for p in (m1, m2, m3, m4)] + [mg], axis=1),
                                LORA_COLS)
            cols = _inproj(h, row(norm_pre_mix[l]), w_perm)
            att = _attention(cols, attn_sink[l], cos2, sin2)
            vecs = [row(t[l]) for t in (w0_fwd, w0_bwd, a0_fwd, a0_bwd, k_k, k_a, r_k)]
            r, v, kk, lwf, bf, kf, lwb, bb, kb, g, bonus = _rwkv_prep(
                cols, mu_rkv, mu_lora, vecs,
                *[_pad_rows(t[l], LORA_PAD).astype(BF16) for t in (w_up_fwd, w_up_bwd, a_up_fwd, a_up_bwd)],
                g_up[l].astype(BF16), ones_bd, band)
            yf, yb = _rwkv_scan(r, v, kk, lwf, bf, kf, lwb, bb, kb)
            h, hn = _outproj(yf, yb, bonus, g, att, cols, row(ln_x_gain[l]), row(ln_x_bias[l]), ones_bd,
                             w_out[l].astype(BF16), h, row(norm_post_mix[l]), row(norm_pre_ffn[l]))
            h = _ffn(hn, w_ffn_up[l].astype(BF16), w_ffn_down[l].astype(BF16), h, row(norm_post_ffn[l]))
        outs.append(h)
    return jnp.stack(outs, axis=0)
```
